```python
import jax, jax.numpy as jnp
from jax import lax
import numpy as np

D_MODEL = 2048
BATCH = 4
SEQ = 8192
DEPTH = 1
DEC_BATCH = 32
DEC_SEQ = 32
PAST_LEN = 2048

CHUNK = 64
N_META = 16
C_RW = 1024
RW_HEAD = 64
H_RW = C_RW // RW_HEAD
D_DECAY = 64
D_AAA = 64
D_GATE = 160
RW_COLS = 3 * C_RW + D_DECAY + D_AAA + D_GATE
C_SB = D_MODEL - C_RW
SB_HEAD = 64
H_SB = C_SB // SB_HEAD
Q_BLOCK = 128
C_MIX = C_RW + C_SB
C_IN = RW_COLS + 3 * C_SB
N_EXPERTS = 256
TOP_K = 8
N_GROUPS = 8
TOPK_GROUPS = 4
D_EXPERT = 512
D_SHARED = 512
ROUTED_SCALE = 2.5
EXPERT_BLOCK = 128
LN_EPS = 1e-5
GN_EPS = 64e-5
RMS_EPS = 1e-6
DN_ALPHA = (2 * DEPTH) ** 0.25
DN_BETA = (8 * DEPTH) ** -0.25

kernel_name = 'hymba_rwkv7_stickbreak_moe_stream_step'


def layer_norm(x, g, b):
    xf = x.astype(jnp.float32)
    mu = jnp.mean(xf, -1, keepdims=True)
    var = jnp.mean(jnp.square(xf - mu), -1, keepdims=True)
    return ((xf - mu) * lax.rsqrt(var + LN_EPS) * g + b).astype(x.dtype)


def wkv_scan(r, w, k, v, kk, a, s0):
    def step(S, inp):
        r_t, w_t, k_t, v_t, kk_t, a_t = inp
        sa = jnp.einsum('bhij,bhj->bhi', S, -kk_t)
        S = (S * w_t[:, :, None, :] + sa[..., None] * (kk_t * a_t)[:, :, None, :]
             + v_t[..., None] * k_t[:, :, None, :])
        return S, jnp.einsum('bhij,bhj->bhi', S, r_t)
    xs = tuple(jnp.moveaxis(t, 1, 0) for t in (r, w, k, v, kk, a))
    S, out = lax.scan(step, s0, xs)
    return jnp.moveaxis(out, 0, 1), S


def rwkv_mixer(p_rw, shift_prev, wkv0, lp):
    B, L, _ = p_rw.shape
    f32 = jnp.float32
    cur = p_rw.astype(f32)
    prev = jnp.concatenate([shift_prev.astype(f32), cur[:, :-1]], 1)
    xs = cur + (prev - cur) * lp['rw_mu']
    r, k, v, wd, ad, gd = jnp.split(
        xs, [C_RW, 2 * C_RW, 3 * C_RW, 3 * C_RW + D_DECAY, 3 * C_RW + D_DECAY + D_AAA], axis=-1)
    w_log = -jax.nn.softplus(-(lp['rw_w0'] + jnp.tanh(wd) @ lp['rw_w2'])) - 0.5
    decay = jnp.exp(-jnp.exp(w_log))
    a = jax.nn.sigmoid(lp['rw_a0'] + ad @ lp['rw_a2'])
    g = jax.nn.sigmoid(gd) @ lp['rw_g2']
    heads = lambda t: t.reshape(B, L, H_RW, RW_HEAD)
    kk = heads(k * lp['rw_k_k'])
    kk = kk / jnp.maximum(jnp.sqrt(jnp.sum(jnp.square(kk), -1, keepdims=True)), 1e-12)
    k = k * (1.0 + (a - 1.0) * lp['rw_k_a'])
    r, k, v, decay, a = heads(r), heads(k), heads(v), heads(decay), heads(a)
    out, wkv = wkv_scan(r, decay, k, v, kk, a, wkv0.astype(f32))
    mu = jnp.mean(out, -1, keepdims=True)
    var = jnp.mean(jnp.square(out - mu), -1, keepdims=True)
    out = ((out - mu) * lax.rsqrt(var + GN_EPS)).reshape(B, L, C_RW) * lp['rw_gn_g'] + lp['rw_gn_b']
    bonus = jnp.sum(r * k * lp['rw_r_k'], -1, keepdims=True) * v
    out = (out + bonus.reshape(B, L, C_RW)) * g
    return out.astype(p_rw.dtype), wkv.astype(p_rw.dtype)


def sb_block(q, q_pos, k, v):
    f32 = jnp.float32
    z = jnp.einsum('bqhd,bkhd->bhqk', q.astype(f32), k.astype(f32)) * (SB_HEAD ** -0.5)
    causal = jnp.arange(k.shape[1])[None, :] < q_pos[:, None]
    log_1mb = jnp.where(causal, -jax.nn.softplus(z), 0.0)
    rest = lax.cumsum(log_1mb, axis=3, reverse=True) - log_1mb
    att = jnp.where(causal, jnp.exp(jax.nn.log_sigmoid(z) + rest), 0.0)
    return jnp.einsum('bhqk,bkhd->bqhd', att, v.astype(f32)).astype(q.dtype)


def stick_breaking(q, k_all, v_all, q_start):
    B, L = q.shape[:2]
    if L <= Q_BLOCK:
        return sb_block(q, q_start + jnp.arange(L), k_all, v_all)
    nb = -(-L // Q_BLOCK)
    qp = jnp.pad(q, ((0, 0), (0, nb * Q_BLOCK - L), (0, 0), (0, 0)))
    qp = qp.reshape(B, nb, Q_BLOCK, H_SB, SB_HEAD).transpose(1, 0, 2, 3, 4)
    pos = (q_start + jnp.arange(nb * Q_BLOCK)).reshape(nb, Q_BLOCK)
    out = lax.map(lambda qa: sb_block(qa[0], qa[1], k_all, v_all), (qp, pos))
    return out.transpose(1, 0, 2, 3, 4).reshape(B, nb * Q_BLOCK, H_SB, SB_HEAD)[:, :L]


def routed_experts(xf, idx, gate, w_gate, w_up, w_down):
    T, D = xf.shape
    A = T * TOP_K
    e_flat = idx.reshape(A)
    tok_flat = jnp.arange(A, dtype=jnp.int32) // TOP_K
    order = jnp.argsort(e_flat)
    e_sorted = e_flat[order]
    counts = jnp.zeros((N_EXPERTS,), jnp.int32).at[e_flat].add(1)
    padded = (counts + EXPERT_BLOCK - 1) // EXPERT_BLOCK * EXPERT_BLOCK
    pad_end = jnp.cumsum(padded)
    pad_start = pad_end - padded
    raw_start = jnp.cumsum(counts) - counts
    dest = pad_start[e_sorted] + jnp.arange(A, dtype=jnp.int32) - raw_start[e_sorted]
    n_rows = -(-(A + N_EXPERTS * (EXPERT_BLOCK - 1)) // EXPERT_BLOCK) * EXPERT_BLOCK
    n_blocks = n_rows // EXPERT_BLOCK
    row_tok = jnp.full((n_rows,), T, jnp.int32).at[dest].set(tok_flat[order])
    row_w = jnp.zeros((n_rows,), xf.dtype).at[dest].set(gate.reshape(A)[order].astype(xf.dtype))
    blk_start = jnp.arange(n_blocks, dtype=jnp.int32) * EXPERT_BLOCK
    blk_exp = jnp.minimum(jnp.searchsorted(pad_end, blk_start, side='right'), N_EXPERTS - 1)
    x_pad = jnp.concatenate([xf, jnp.zeros((1, D), xf.dtype)], 0)

    def body(y, inp):
        toks, wts, e = inp
        xb = x_pad[toks]
        h = jax.nn.silu(xb @ w_gate[e]) * (xb @ w_up[e])
        return y.at[toks].add((h @ w_down[e]) * wts[:, None]), None

    y, _ = lax.scan(body, jnp.zeros((T + 1, D), xf.dtype),
                    (row_tok.reshape(n_blocks, EXPERT_BLOCK), row_w.reshape(n_blocks, EXPERT_BLOCK), blk_exp))
    return y[:T]


def moe_ffn(x, lp):
    B, L, D = x.shape
    T = B * L
    f32 = jnp.float32
    xf = x.reshape(T, D)
    scores = jax.nn.sigmoid((xf @ lp['w_router']).astype(f32))
    sel = scores + lp['router_bias'].astype(f32)
    grp = sel.reshape(T, N_GROUPS, N_EXPERTS // N_GROUPS)
    grp_score = jnp.sum(lax.top_k(grp, 2)[0], -1)
    _, grp_idx = lax.top_k(grp_score, TOPK_GROUPS)
    grp_keep = jnp.sum(jax.nn.one_hot(grp_idx, N_GROUPS, dtype=f32), 1) > 0
    sel = jnp.where(jnp.repeat(grp_keep, N_EXPERTS // N_GROUPS, axis=1), sel, -jnp.inf)
    _, idx = lax.top_k(sel, TOP_K)
    gate = jnp.take_along_axis(scores, idx, -1)
    gate = gate / jnp.sum(gate, -1, keepdims=True) * ROUTED_SCALE
    routed = routed_experts(xf, idx, gate, lp['w_exp_gate'], lp['w_exp_up'], lp['w_exp_down'])
    shared = (jax.nn.silu(xf @ lp['w_sh_gate']) * (xf @ lp['w_sh_up'])) @ lp['w_sh_down']
    return (routed + shared).reshape(B, L, D)


def trunk_layer(x, shift_prev, wkv0, k_past, v_past, lp):
    B, L, _ = x.shape
    proj = x @ lp['w_in']
    p_rw = proj[..., :RW_COLS]
    q, k, v = jnp.split(proj[..., RW_COLS:], 3, axis=-1)
    q = q.reshape(B, L, H_SB, SB_HEAD)
    k = k.reshape(B, L, H_SB, SB_HEAD)
    v = v.reshape(B, L, H_SB, SB_HEAD)
    rw_out, wkv_new = rwkv_mixer(p_rw, shift_prev, wkv0, lp)
    k_all = jnp.concatenate([k_past.astype(k.dtype), k], 1)
    v_all = jnp.concatenate([v_past.astype(v.dtype), v], 1)
    sb = stick_breaking(q, k_all, v_all, k_past.shape[1]).astype(jnp.float32)
    sb = sb * lax.rsqrt(jnp.mean(jnp.square(sb), -1, keepdims=True) + RMS_EPS)
    sb = (sb.reshape(B, L, C_SB) * lp['sb_norm_g']).astype(x.dtype)
    mix = jnp.concatenate([rw_out, sb], -1) @ lp['w_out']
    x = layer_norm(DN_ALPHA * x + mix, lp['ln1_g'], lp['ln1_b'])
    x = layer_norm(DN_ALPHA * x + moe_ffn(x, lp), lp['ln2_g'], lp['ln2_b'])
    return x, (k, v, wkv_new, p_rw[:, -1:])


def setup_inputs(seed: int = 0) -> dict:
    key = jax.random.key(seed)
    ks = iter(jax.random.split(key, 48))
    nrm = lambda shape, scale: jax.random.normal(next(ks), shape, jnp.float32) * scale
    L = DEPTH
    return {
        'x_prompt': nrm((BATCH, SEQ, D_MODEL), 1.0),
        'x_sample': nrm((DEC_BATCH, DEC_SEQ, D_MODEL), 1.0),
        'cache_sb_k': nrm((L, DEC_BATCH, PAST_LEN, H_SB, SB_HEAD), 1.0),
        'cache_sb_v': nrm((L, DEC_BATCH, PAST_LEN, H_SB, SB_HEAD), 1.0),
        'state_rwkv_wkv': nrm((L, DEC_BATCH, H_RW, RW_HEAD, RW_HEAD), 0.1),
        'state_rwkv_shift': nrm((L, DEC_BATCH, 1, RW_COLS), 1.0),
        'meta_tokens': nrm((N_META, D_MODEL), 1.0),
        'ln0_g': 1.0 + nrm((D_MODEL,), 0.02),
        'ln0_b': nrm((D_MODEL,), 0.02),
        'w_in': nrm((L, D_MODEL, C_IN), D_MODEL ** -0.5),
        'rw_mu': jax.random.uniform(next(ks), (L, RW_COLS), jnp.float32),
        'rw_w0': nrm((L, C_RW), 1.0) - 1.0,
        'rw_w2': nrm((L, D_DECAY, C_RW), D_DECAY ** -0.5),
        'rw_a0': nrm((L, C_RW), 0.1),
        'rw_a2': nrm((L, D_AAA, C_RW), D_AAA ** -0.5),
        'rw_g2': nrm((L, D_GATE, C_RW), D_GATE ** -0.5),
        'rw_k_k': 0.85 + nrm((L, C_RW), 0.05),
        'rw_k_a': 1.0 + nrm((L, C_RW), 0.05),
        'rw_r_k': nrm((L, H_RW, RW_HEAD), 0.1),
        'rw_gn_g': 1.0 + nrm((L, C_RW), 0.02),
        'rw_gn_b': nrm((L, C_RW), 0.02),
        'sb_norm_g': 1.0 + nrm((L, C_SB), 0.02),
        'w_out': nrm((L, C_MIX, D_MODEL), C_MIX ** -0.5 * DN_BETA),
        'ln1_g': 1.0 + nrm((L, D_MODEL), 0.02),
        'ln1_b': nrm((L, D_MODEL), 0.02),
        'w_router': nrm((L, D_MODEL, N_EXPERTS), D_MODEL ** -0.5),
        'router_bias': nrm((L, N_EXPERTS), 0.01),
        'w_exp_gate': nrm((L, N_EXPERTS, D_MODEL, D_EXPERT), D_MODEL ** -0.5),
        'w_exp_up': nrm((L, N_EXPERTS, D_MODEL, D_EXPERT), D_MODEL ** -0.5),
        'w_exp_down': nrm((L, N_EXPERTS, D_EXPERT, D_MODEL), D_EXPERT ** -0.5 * DN_BETA),
        'w_sh_gate': nrm((L, D_MODEL, D_SHARED), D_MODEL ** -0.5),
        'w_sh_up': nrm((L, D_MODEL, D_SHARED), D_MODEL ** -0.5),
        'w_sh_down': nrm((L, D_SHARED, D_MODEL), D_SHARED ** -0.5 * DN_BETA),
        'ln2_g': 1.0 + nrm((L, D_MODEL), 0.02),
        'ln2_b': nrm((L, D_MODEL), 0.02),
    }


def reference(x_prompt, x_sample, cache_sb_k, cache_sb_v, state_rwkv_wkv, state_rwkv_shift,
              meta_tokens, ln0_g, ln0_b, w_in, rw_mu, rw_w0, rw_w2, rw_a0, rw_a2, rw_g2,
              rw_k_k, rw_k_a, rw_r_k, rw_gn_g, rw_gn_b, sb_norm_g, w_out, ln1_g, ln1_b,
              w_router, router_bias, w_exp_gate, w_exp_up, w_exp_down,
              w_sh_gate, w_sh_up, w_sh_down, ln2_g, ln2_b):
    assert x_sample.shape[1] <= CHUNK
    B, _, D = x_prompt.shape
    dt = x_prompt.dtype
    meta = jnp.broadcast_to(meta_tokens[None].astype(dt), (B, N_META, D))
    hp = layer_norm(jnp.concatenate([meta, x_prompt], 1), ln0_g, ln0_b)
    hs = layer_norm(x_sample, ln0_g, ln0_b)
    empty_kv = jnp.zeros((B, 0, H_SB, SB_HEAD), dt)
    zero_wkv = jnp.zeros((B, H_RW, RW_HEAD, RW_HEAD), dt)
    zero_shift = jnp.zeros((B, 1, RW_COLS), dt)
    kp, vp, wp, sp, kd, vd, wd, sd = [], [], [], [], [], [], [], []
    for l in range(DEPTH):
        lp = {
            'w_in': w_in[l], 'rw_mu': rw_mu[l], 'rw_w0': rw_w0[l], 'rw_w2': rw_w2[l],
            'rw_a0': rw_a0[l], 'rw_a2': rw_a2[l], 'rw_g2': rw_g2[l], 'rw_k_k': rw_k_k[l],
            'rw_k_a': rw_k_a[l], 'rw_r_k': rw_r_k[l], 'rw_gn_g': rw_gn_g[l], 'rw_gn_b': rw_gn_b[l],
            'sb_norm_g': sb_norm_g[l], 'w_out': w_out[l], 'ln1_g': ln1_g[l], 'ln1_b': ln1_b[l],
            'w_router': w_router[l], 'router_bias': router_bias[l], 'w_exp_gate': w_exp_gate[l],
            'w_exp_up': w_exp_up[l], 'w_exp_down': w_exp_down[l], 'w_sh_gate': w_sh_gate[l],
            'w_sh_up': w_sh_up[l], 'w_sh_down': w_sh_down[l], 'ln2_g': ln2_g[l], 'ln2_b': ln2_b[l],
        }
        hp, (k_n, v_n, wkv_n, sh_n) = trunk_layer(hp, zero_shift, zero_wkv, empty_kv, empty_kv, lp)
        kp.append(k_n); vp.append(v_n); wp.append(wkv_n); sp.append(sh_n)
        hs, (k_n, v_n, wkv_n, sh_n) = trunk_layer(hs, state_rwkv_shift[l], state_rwkv_wkv[l],
                                                  cache_sb_k[l], cache_sb_v[l], lp)
        kd.append(k_n); vd.append(v_n); wd.append(wkv_n); sd.append(sh_n)
    y_prompt = hp[:, N_META:]
    return (y_prompt, hs, jnp.stack(kp), jnp.stack(vp), jnp.stack(wp), jnp.stack(sp),
            jnp.stack(kd), jnp.stack(vd), jnp.stack(wd), jnp.stack(sd))
```

```python
import functools

import jax
import jax.numpy as jnp
from jax import lax
from jax.experimental import pallas as pl
from jax.experimental.pallas import tpu as pltpu

F32 = jnp.float32
BF16 = jnp.bfloat16
HIGHEST = lax.Precision.HIGHEST

N_META = 16
HEAD = 64
C_RW = 1024
C_SB = 1024
D_DECAY = 64
D_AAA = 64
D_GATE = 160
LOW_PAD = 512
TOP_K = 8
N_GROUPS = 8
TOPK_GROUPS = 4
ROUTED_SCALE = 2.5
LN_EPS = 1e-5
GN_EPS = 64e-5
RMS_EPS = 1e-6
DEPTH = 1
DN_ALPHA = (2 * DEPTH) ** 0.25

LANES = 128
VMEM_LIMIT = 56 * 1024 * 1024
RW_HEADS_PER_STEP = 4
EXPERT_BLOCK = 256


def _cparams(sem):
    return pltpu.CompilerParams(dimension_semantics=sem, vmem_limit_bytes=VMEM_LIMIT)


def _layer_norm(x, g, b):
    mu = jnp.mean(x, -1, keepdims=True)
    xc = x - mu
    var = jnp.mean(xc * xc, -1, keepdims=True)
    return xc * lax.rsqrt(var + LN_EPS) * g + b


def _sigmoid(x):
    return 1.0 / (1.0 + jnp.exp(-x))


def _softplus(x):
    return jnp.maximum(x, 0.0) + jnp.log(1.0 + jnp.exp(-jnp.abs(x)))


def _dot(a, b, precision=None):
    return jnp.dot(a, b, preferred_element_type=F32, precision=precision)


def _dot_nt(a, b, precision=None):
    return lax.dot_general(a, b, (((1,), (1,)), ((), ())), preferred_element_type=F32, precision=precision)


def _dot_tn(a, b, precision=None):
    return lax.dot_general(a, b, (((0,), (0,)), ((), ())), preferred_element_type=F32, precision=precision)


def _in_proj_kernel(x_ref, g_ref, b_ref, w_ref, wlow_ref, rkv_ref, q_ref, k_ref, v_ref, low_ref, xn_ref):
    j = pl.program_id(1)

    @pl.when(j == 0)
    def _():
        xn = _layer_norm(x_ref[...], g_ref[...], b_ref[...]).astype(BF16)
        xn_ref[...] = xn
        low_ref[...] = _dot(xn, wlow_ref[...])

    y = _dot(xn_ref[...], w_ref[...])

    @pl.when(j < 3)
    def _():
        rkv_ref[...] = y

    @pl.when(j == 3)
    def _():
        q_ref[...] = y

    @pl.when(j == 4)
    def _():
        k_ref[...] = y

    @pl.when(j == 5)
    def _():
        v_ref[...] = y


def _in_proj(x, ln_g, ln_b, w6, wlow, tm):
    t, d = x.shape
    assert t % tm == 0
    c = C_RW
    grid = (t // tm, 6)
    col = lambda i, j: (i, 0)
    return pl.pallas_call(
        _in_proj_kernel,
        grid=grid,
        in_specs=[
            pl.BlockSpec((tm, d), col),
            pl.BlockSpec((1, d), lambda i, j: (0, 0)),
            pl.BlockSpec((1, d), lambda i, j: (0, 0)),
            pl.BlockSpec((d, c), lambda i, j: (0, j)),
            pl.BlockSpec((d, LOW_PAD), lambda i, j: (0, 0)),
        ],
        out_specs=[
            pl.BlockSpec((tm, c), lambda i, j: (i, jnp.minimum(j, 2))),
            pl.BlockSpec((tm, c), col),
            pl.BlockSpec((tm, c), col),
            pl.BlockSpec((tm, c), col),
            pl.BlockSpec((tm, LOW_PAD), col),
        ],
        out_shape=[
            jax.ShapeDtypeStruct((t, 3 * c), F32),
            jax.ShapeDtypeStruct((t, c), F32),
            jax.ShapeDtypeStruct((t, c), F32),
            jax.ShapeDtypeStruct((t, c), F32),
            jax.ShapeDtypeStruct((t, LOW_PAD), F32),
        ],
        scratch_shapes=[pltpu.VMEM((tm, d), BF16)],
        compiler_params=_cparams(("parallel", "arbitrary")),
        name="in_proj",
    )(x, ln_g, ln_b, w6, wlow)


def _rwkv_kernel(r_ref, k_ref, v_ref, low_ref, sr_ref, sk_ref, sv_ref, slow_ref,
                 mur_ref, muk_ref, muv_ref, mulow_ref,
                 w0_ref, a0_ref, kk_ref, ka_ref, rk_ref, gng_ref, gnb_ref,
                 w2_ref, a2_ref, g2_ref, bd_ref, wkv0_ref,
                 out_ref, wkv_ref,
                 s_ref, pr_ref, pk_ref, pv_ref, plow_ref, *, tc, hps, n_dbl):
    c = pl.program_id(2)
    nc = pl.num_programs(2)

    @pl.when(c == 0)
    def _():
        s_ref[...] = wkv0_ref[...]
        pr_ref[...] = sr_ref[...]
        pk_ref[...] = sk_ref[...]
        pv_ref[...] = sv_ref[...]
        plow_ref[...] = slow_ref[...]

    def token_shift(cur_ref, prev_ref, mu_ref):
        cur = cur_ref[...]
        row = lax.broadcasted_iota(jnp.int32, cur.shape, 0)
        prev = jnp.where(row == 0, prev_ref[...], pltpu.roll(cur, 1, 0))
        prev_ref[...] = cur[tc - 1:tc, :]
        return cur + (prev - cur) * mu_ref[...]

    r = token_shift(r_ref, pr_ref, mur_ref)
    k = token_shift(k_ref, pk_ref, muk_ref)
    v = token_shift(v_ref, pv_ref, muv_ref)
    xl = token_shift(low_ref, plow_ref, mulow_ref)
    wd = xl[:, 0:LANES]
    ad = xl[:, LANES:2 * LANES]
    gd = xl[:, 2 * LANES:LOW_PAD]

    w_log = -_softplus(-(w0_ref[...] + _dot(jnp.tanh(wd), w2_ref[...], HIGHEST))) - 0.5
    lw = -jnp.exp(w_log)
    a = _sigmoid(a0_ref[...] + _dot(ad, a2_ref[...], HIGHEST))
    g = _dot(_sigmoid(gd), g2_ref[...], HIGHEST)

    bd = bd_ref[...]
    kk = k * kk_ref[...]
    kk = kk / jnp.maximum(jnp.sqrt(_dot(kk * kk, bd, HIGHEST)), 1e-12)
    k2 = k * (1.0 + (a - 1.0) * ka_ref[...])
    bonus = _dot(r * k2 * rk_ref[...], bd, HIGHEST) * v

    ti = lax.broadcasted_iota(jnp.int32, (tc, tc), 0)
    si = lax.broadcasted_iota(jnp.int32, (tc, tc), 1)
    incl = si <= ti
    strict = si < ti
    cs = _dot(incl.astype(F32), lw, HIGHEST)
    c_last = cs[tc - 1:tc, :]
    e_pos = jnp.exp(cs)
    e_neg = jnp.exp(-cs)
    e_end = jnp.exp(c_last - cs)
    b = kk * a
    r_t = r * e_pos
    al_t = -kk * jnp.exp(cs - lw)
    be_t = b * e_neg
    k_t = k2 * e_neg
    be_h = b * e_end
    k_h = k2 * e_end
    g_last = jnp.exp(c_last)

    eye_t = (si == ti).astype(F32)
    di = lax.broadcasted_iota(jnp.int32, (HEAD, HEAD), 0)
    dj = lax.broadcasted_iota(jnp.int32, (HEAD, HEAD), 1)
    eye_h = di == dj

    outs = []
    for h in range(hps):
        sl = slice(h * HEAD, (h + 1) * HEAD)
        al_h, r_h, v_h = al_t[:, sl], r_t[:, sl], v[:, sl]
        bet, kt = be_t[:, sl], k_t[:, sl]
        a_mat = jnp.where(strict, _dot_nt(al_h, bet, HIGHEST), 0.0)
        b_mat = jnp.where(strict, _dot_nt(al_h, kt, HIGHEST), 0.0)
        ar_mat = jnp.where(incl, _dot_nt(r_h, bet, HIGHEST), 0.0)
        br_mat = jnp.where(incl, _dot_nt(r_h, kt, HIGHEST), 0.0)
        m = eye_t + a_mat
        pw = a_mat
        for _ in range(n_dbl):
            pw = _dot(pw, pw, HIGHEST)
            m = m + _dot(m, pw, HIGHEST)
        w_m = _dot(m, al_h, HIGHEST)
        u_loc = _dot(m, _dot(b_mat, v_h, HIGHEST), HIGHEST)
        r_hat = r_h + _dot(ar_mat, w_m, HIGHEST)
        o_loc = _dot(ar_mat, u_loc, HIGHEST) + _dot(br_mat, v_h, HIGHEST)
        p_mat = _dot_tn(w_m, be_h[:, sl], HIGHEST) + jnp.where(eye_h, g_last[:, sl], 0.0)
        q_mat = _dot_tn(u_loc, be_h[:, sl], HIGHEST) + _dot_tn(v_h, k_h[:, sl], HIGHEST)
        s0 = s_ref[h]
        o = o_loc + _dot_nt(r_hat, s0, HIGHEST)
        s_ref[h] = _dot(s0, p_mat, HIGHEST) + q_mat
        mu = jnp.mean(o, -1, keepdims=True)
        oc = o - mu
        var = jnp.mean(oc * oc, -1, keepdims=True)
        outs.append(oc * lax.rsqrt(var + GN_EPS))
    on = jnp.concatenate(outs, axis=1)
    out_ref[...] = ((on * gng_ref[...] + gnb_ref[...] + bonus) * g).astype(out_ref.dtype)

    @pl.when(c == nc - 1)
    def _():
        wkv_ref[...] = s_ref[...]


def _rwkv(rkv, low, shift0, wkv0, prm, tc):
    bsz, length, _ = rkv.shape
    b0 = shift0.shape[0]
    assert length % tc == 0 and b0 in (1, bsz)
    hps = RW_HEADS_PER_STEP
    w = hps * HEAD
    n_h = C_RW // HEAD
    ng = n_h // hps
    nblk = C_RW // w
    n_dbl = max((tc - 1).bit_length() - 1, 0)
    sb = (lambda b: b) if b0 == bsz else (lambda b: 0)
    seq = lambda off: pl.BlockSpec((None, tc, w), lambda b, g, c: (b, c, off * nblk + g))
    row0 = lambda off: pl.BlockSpec((None, 1, w), lambda b, g, c: (sb(b), 0, off * nblk + g))
    vec = lambda off: pl.BlockSpec((1, w), lambda b, g, c: (0, off * nblk + g))
    low_blk = (3 * C_RW) // LOW_PAD
    bd = (jnp.arange(w)[:, None] // HEAD == jnp.arange(w)[None, :] // HEAD).astype(F32)
    kernel = functools.partial(_rwkv_kernel, tc=tc, hps=hps, n_dbl=n_dbl)
    return pl.pallas_call(
        kernel,
        grid=(bsz, ng, length // tc),
        in_specs=[
            seq(0), seq(1), seq(2),
            pl.BlockSpec((None, tc, LOW_PAD), lambda b, g, c: (b, c, 0)),
            row0(0), row0(1), row0(2),
            pl.BlockSpec((None, 1, LOW_PAD), lambda b, g, c: (sb(b), 0, low_blk)),
            vec(0), vec(1), vec(2),
            pl.BlockSpec((1, LOW_PAD), lambda b, g, c: (0, low_blk)),
            vec(0), vec(0), vec(0), vec(0), vec(0), vec(0), vec(0),
            pl.BlockSpec((LANES, w), lambda b, g, c: (0, g)),
            pl.BlockSpec((LANES, w), lambda b, g, c: (0, g)),
            pl.BlockSpec((2 * LANES, w), lambda b, g, c: (0, g)),
            pl.BlockSpec((w, w), lambda b, g, c: (0, 0)),
            pl.BlockSpec((None, hps, HEAD, HEAD), lambda b, g, c: (sb(b), g, 0, 0)),
        ],
        out_specs=[
            pl.BlockSpec((None, tc, w), lambda b, g, c: (b, c, g)),
            pl.BlockSpec((None, hps, HEAD, HEAD), lambda b, g, c: (b, g, 0, 0)),
        ],
        out_shape=[
            jax.ShapeDtypeStruct((bsz, length, C_RW), BF16),
            jax.ShapeDtypeStruct((bsz, n_h, HEAD, HEAD), F32),
        ],
        scratch_shapes=[
            pltpu.VMEM((hps, HEAD, HEAD), F32),
            pltpu.VMEM((1, w), F32), pltpu.VMEM((1, w), F32), pltpu.VMEM((1, w), F32),
            pltpu.VMEM((1, LOW_PAD), F32),
        ],
        compiler_params=_cparams(("parallel", "parallel", "arbitrary")),
        name="rwkv",
    )(rkv, rkv, rkv, low, shift0, shift0, shift0, shift0,
      prm["mu"], prm["mu"], prm["mu"], prm["mu"],
      prm["w0"], prm["a0"], prm["k_k"], prm["k_a"], prm["r_k"], prm["gn_g"], prm["gn_b"],
      prm["w2"], prm["a2"], prm["g2"], bd, wkv0)


def _attn_kernel(q_ref, k_ref, v_ref, kp_ref, vp_ref, tri_ref, trip_ref, g_ref, out_ref, *, tq, pb, n_pre):
    i = pl.program_id(2)
    q0 = pl.multiple_of(i * tq, tq)
    lane = lax.broadcasted_iota(jnp.int32, (tq, LANES), 1)
    first = lane < HEAD
    q2 = q_ref[pl.ds(q0, tq), :] * (HEAD ** -0.5)
    q_heads = (jnp.where(first, q2, 0.0).astype(BF16), jnp.where(first, 0.0, q2).astype(BF16))
    tri = tri_ref[...]
    trip = trip_ref[...]

    def block(qh, kb, vb, tri_m, carry, acc, mask):
        z = _dot_nt(qh, kb)
        sp = _softplus(z)
        l1 = -sp if mask is None else jnp.where(mask, -sp, 0.0)
        hi = l1.astype(BF16)
        lo = (l1 - hi.astype(F32)).astype(BF16)
        suffix = _dot(hi, tri_m) + _dot(lo, tri_m)
        logit = (z - sp) + (suffix - l1) + carry
        p = jnp.exp(logit)
        if mask is not None:
            p = jnp.where(mask, p, 0.0)
        acc = acc + _dot(p.astype(BF16), vb)
        return carry + suffix[:, 0:1], acc

    kd = k_ref[pl.ds(q0, tq), :].astype(BF16)
    vd = v_ref[pl.ds(q0, tq), :].astype(BF16)
    row = lax.broadcasted_iota(jnp.int32, (tq, tq), 0)
    colm = lax.broadcasted_iota(jnp.int32, (tq, tq), 1)
    causal = colm < row
    state = []
    for qh in q_heads:
        state.extend(block(qh, kd, vd, tri, jnp.zeros((tq, 1), F32), jnp.zeros((tq, LANES), F32), causal))

    def off_diag(t, st):
        k0 = pl.multiple_of((i - 1 - t) * tq, tq)
        kb = k_ref[pl.ds(k0, tq), :].astype(BF16)
        vb = v_ref[pl.ds(k0, tq), :].astype(BF16)
        c_a, acc_a = block(q_heads[0], kb, vb, tri, st[0], st[1], None)
        c_b, acc_b = block(q_heads[1], kb, vb, tri, st[2], st[3], None)
        return (c_a, acc_a, c_b, acc_b)

    state = lax.fori_loop(0, i, off_diag, tuple(state))

    def prefix(t, st):
        k0 = pl.multiple_of((n_pre - 1 - t) * pb, pb)
        kb = kp_ref[pl.ds(k0, pb), :].astype(BF16)
        vb = vp_ref[pl.ds(k0, pb), :].astype(BF16)
        c_a, acc_a = block(q_heads[0], kb, vb, trip, st[0], st[1], None)
        c_b, acc_b = block(q_heads[1], kb, vb, trip, st[2], st[3], None)
        return (c_a, acc_a, c_b, acc_b)

    state = lax.fori_loop(0, n_pre, prefix, state)

    o = jnp.where(first, state[1], state[3])
    sq = o * o
    s_a = jnp.sum(jnp.where(first, sq, 0.0), -1, keepdims=True)
    s_b = jnp.sum(sq, -1, keepdims=True) - s_a
    inv = jnp.where(first, lax.rsqrt(s_a / HEAD + RMS_EPS), lax.rsqrt(s_b / HEAD + RMS_EPS))
    out_ref[...] = (o * inv * g_ref[...]).astype(out_ref.dtype)


def _attention(q, k, v, kp, vp, gain, tq, pb):
    bsz, length, c = q.shape
    b0, plen, _ = kp.shape
    assert length % tq == 0 and plen % pb == 0 and b0 in (1, bsz)
    sb = (lambda b: b) if b0 == bsz else (lambda b: 0)
    n_pre = plen // pb
    tri = (jnp.arange(tq)[:, None] >= jnp.arange(tq)[None, :]).astype(BF16)
    trip = (jnp.arange(pb)[:, None] >= jnp.arange(pb)[None, :]).astype(BF16)
    full = pl.BlockSpec((None, length, LANES), lambda b, h, i: (b, 0, h))
    pre = pl.BlockSpec((None, plen, LANES), lambda b, h, i: (sb(b), 0, h))
    kernel = functools.partial(_attn_kernel, tq=tq, pb=pb, n_pre=n_pre)
    return pl.pallas_call(
        kernel,
        grid=(bsz, c // LANES, length // tq),
        in_specs=[
            full, full, full, pre, pre,
            pl.BlockSpec((tq, tq), lambda b, h, i: (0, 0)),
            pl.BlockSpec((pb, pb), lambda b, h, i: (0, 0)),
            pl.BlockSpec((1, LANES), lambda b, h, i: (0, h)),
        ],
        out_specs=pl.BlockSpec((None, tq, LANES), lambda b, h, i: (b, i, h)),
        out_shape=jax.ShapeDtypeStruct((bsz, length, c), BF16),
        compiler_params=_cparams(("parallel", "parallel", "arbitrary")),
        name="attn",
    )(q, k, v, kp, vp, tri, trip, gain)


def _out_proj_kernel(x_ref, rw_ref, sb_ref, wa_ref, wb_ref, g0_ref, b0_ref, g1_ref, b1_ref,
                     wrh_ref, wrl_ref, h_ref, hb_ref, lg_ref):
    xn = _layer_norm(x_ref[...], g0_ref[...], b0_ref[...])
    mix = _dot(rw_ref[...], wa_ref[...]) + _dot(sb_ref[...], wb_ref[...])
    h = _layer_norm(DN_ALPHA * xn + mix, g1_ref[...], b1_ref[...])
    h_ref[...] = h
    hi = h.astype(BF16)
    hb_ref[...] = hi
    lo = (h - hi.astype(F32)).astype(BF16)
    lg_ref[...] = _dot(hi, wrh_ref[...]) + _dot(lo, wrh_ref[...]) + _dot(hi, wrl_ref[...])


def _out_proj(x, rw, sbo, wa, wb, g0, b0, g1, b1, wrh, wrl, tm):
    t, d = x.shape
    ne = wrh.shape[1]
    assert t % tm == 0
    row = lambda w: pl.BlockSpec((tm, w), lambda i: (i, 0))
    const = lambda a: pl.BlockSpec(a.shape, lambda i: (0, 0))
    return pl.pallas_call(
        _out_proj_kernel,
        grid=(t // tm,),
        in_specs=[row(d), row(C_RW), row(C_SB), const(wa), const(wb), const(g0), const(b0), const(g1), const(b1),
                  const(wrh), const(wrl)],
        out_specs=[row(d), row(d), row(ne)],
        out_shape=[jax.ShapeDtypeStruct((t, d), F32), jax.ShapeDtypeStruct((t, d), BF16),
                   jax.ShapeDtypeStruct((t, ne), F32)],
        compiler_params=_cparams(("parallel",)),
        name="out_proj",
    )(x, rw, sbo, wa, wb, g0, b0, g1, b1, wrh, wrl)


def _expert_kernel(be_ref, nv_ref, xs_ref, rw_ref, wg_ref, wu_ref, wd_ref, out_ref,
                   wgb_ref, wub_ref, wdb_ref, prev_ref):
    i = pl.program_id(0)
    e = be_ref[i]

    @pl.when(i == 0)
    def _():
        prev_ref[0] = -1

    @pl.when(i < nv_ref[0])
    def _():
        @pl.when(e != prev_ref[0])
        def _():
            wgb_ref[...] = wg_ref[...].astype(BF16)
            wub_ref[...] = wu_ref[...].astype(BF16)
            wdb_ref[...] = wd_ref[...].astype(BF16)
            prev_ref[0] = e

        x = xs_ref[...]
        hg = _dot(x, wgb_ref[...])
        hu = _dot(x, wub_ref[...])
        hid = (hg * _sigmoid(hg) * hu).astype(BF16)
        out_ref[...] = (_dot(hid, wdb_ref[...]) * rw_ref[...]).astype(out_ref.dtype)

    @pl.when(i >= nv_ref[0])
    def _():
        out_ref[...] = jnp.zeros_like(out_ref)


def _experts(xs, row_w, blk_exp, n_valid, w_gate, w_up, w_down):
    n_rows, d = xs.shape
    _, _, de = w_gate.shape
    blk = EXPERT_BLOCK
    nb = n_rows // blk
    live = lambda i, be, nv: jnp.minimum(i, nv[0] - 1)
    grid_spec = pltpu.PrefetchScalarGridSpec(
        num_scalar_prefetch=2,
        grid=(nb,),
        in_specs=[
            pl.BlockSpec((blk, d), lambda i, be, nv: (live(i, be, nv), 0)),
            pl.BlockSpec((blk, 1), lambda i, be, nv: (live(i, be, nv), 0)),
            pl.BlockSpec((None, d, de), lambda i, be, nv: (be[i], 0, 0)),
            pl.BlockSpec((None, d, de), lambda i, be, nv: (be[i], 0, 0)),
            pl.BlockSpec((None, de, d), lambda i, be, nv: (be[i], 0, 0)),
        ],
        out_specs=pl.BlockSpec((blk, d), lambda i, be, nv: (i, 0)),
        scratch_shapes=[pltpu.VMEM((d, de), BF16), pltpu.VMEM((d, de), BF16), pltpu.VMEM((de, d), BF16),
                        pltpu.SMEM((1,), jnp.int32)],
    )
    return pl.pallas_call(
        _expert_kernel,
        grid_spec=grid_spec,
        out_shape=jax.ShapeDtypeStruct((n_rows, d), BF16),
        compiler_params=_cparams(("arbitrary",)),
        name="experts",
    )(blk_exp, n_valid, xs, row_w, w_gate, w_up, w_down)


def _final_kernel(h_ref, hb_ref, routed_ref, wg_ref, wu_ref, wd_ref, g_ref, b_ref, out_ref):
    x = hb_ref[...]
    hg = _dot(x, wg_ref[...])
    hu = _dot(x, wu_ref[...])
    shared = _dot((hg * _sigmoid(hg) * hu).astype(BF16), wd_ref[...])
    y = DN_ALPHA * h_ref[...] + routed_ref[...] + shared
    out_ref[...] = _layer_norm(y, g_ref[...], b_ref[...])


def _final(h, hb, routed, wg, wu, wd, g, b, tm):
    t, d = h.shape
    assert t % tm == 0
    row = pl.BlockSpec((tm, d), lambda i: (i, 0))
    const = lambda a: pl.BlockSpec(a.shape, lambda i: (0, 0))
    return pl.pallas_call(
        _final_kernel,
        grid=(t // tm,),
        in_specs=[row, row, row, const(wg), const(wu), const(wd), const(g), const(b)],
        out_specs=row,
        out_shape=jax.ShapeDtypeStruct((t, d), F32),
        compiler_params=_cparams(("parallel",)),
        name="final",
    )(h, hb, routed, wg, wu, wd, g, b)


def _route(logits, router_bias):
    t, ne = logits.shape
    scores = jax.nn.sigmoid(logits)
    sel = scores + router_bias.astype(F32)
    grp = sel.reshape(t, N_GROUPS, ne // N_GROUPS)
    grp_score = jnp.sum(lax.top_k(grp, 2)[0], -1)
    _, grp_idx = lax.top_k(grp_score, TOPK_GROUPS)
    grp_keep = jnp.sum(jax.nn.one_hot(grp_idx, N_GROUPS, dtype=F32), 1) > 0
    sel = jnp.where(jnp.repeat(grp_keep, ne // N_GROUPS, axis=1), sel, -jnp.inf)
    _, idx = lax.top_k(sel, TOP_K)
    gate = jnp.take_along_axis(scores, idx, -1)
    gate = gate / jnp.sum(gate, -1, keepdims=True) * ROUTED_SCALE
    return idx, gate


def _dispatch_plan(idx, gate, ne):
    t = idx.shape[0]
    blk = EXPERT_BLOCK
    n_assign = t * TOP_K
    e_flat = idx.reshape(n_assign).astype(jnp.int32)
    order = jnp.argsort(e_flat)
    e_sorted = e_flat[order]
    counts = jnp.zeros((ne,), jnp.int32).at[e_flat].add(1)
    padded = (counts + blk - 1) // blk * blk
    pad_end = jnp.cumsum(padded)
    pad_start = pad_end - padded
    raw_start = jnp.cumsum(counts) - counts
    dest = pad_start[e_sorted] + jnp.arange(n_assign, dtype=jnp.int32) - raw_start[e_sorted]
    nb = n_assign // blk + ne
    n_rows = nb * blk
    row_tok = jnp.zeros((n_rows,), jnp.int32).at[dest].set((order // TOP_K).astype(jnp.int32))
    row_w = jnp.zeros((n_rows,), F32).at[dest].set(gate.reshape(n_assign)[order])
    blk_start = jnp.arange(nb, dtype=jnp.int32) * blk
    blk_exp = jnp.minimum(jnp.searchsorted(pad_end, blk_start, side='right'), ne - 1).astype(jnp.int32)
    n_valid = (pad_end[-1] // blk).astype(jnp.int32).reshape(1)
    pos = jnp.zeros((n_assign,), jnp.int32).at[order].set(dest).reshape(t, TOP_K)
    return row_tok, row_w.reshape(n_rows, 1), blk_exp, n_valid, pos


def _pick_tile(n, cap):
    t = cap
    while n % t:
        t //= 2
    return t


def kernel(x_prompt, x_sample, cache_sb_k, cache_sb_v, state_rwkv_wkv, state_rwkv_shift, meta_tokens, ln0_g, ln0_b, w_in, rw_mu, rw_w0, rw_w2, rw_a0, rw_a2, rw_g2, rw_k_k, rw_k_a, rw_r_k, rw_gn_g, rw_gn_b, sb_norm_g, w_out, ln1_g, ln1_b, w_router, router_bias, w_exp_gate, w_exp_up, w_exp_down, w_sh_gate, w_sh_up, w_sh_down, ln2_g, ln2_b):
    bsz, seq, d = x_prompt.shape
    dbs, dseq, _ = x_sample.shape
    past = cache_sb_k.shape[2]
    n_h = C_RW // HEAD
    ne = w_router.shape[-1]
    rw_cols = 3 * C_RW + D_DECAY + D_AAA + D_GATE
    o_wd, o_ad, o_gd = 3 * C_RW, 3 * C_RW + D_DECAY, 3 * C_RW + D_DECAY + D_AAA

    def pad_low(a):
        z = lambda n: jnp.zeros(a.shape[:-1] + (n,), a.dtype)
        return jnp.concatenate([a[..., :o_wd], a[..., o_wd:o_ad], z(LANES - D_DECAY), a[..., o_ad:o_gd],
                                z(LANES - D_AAA), a[..., o_gd:rw_cols], z(2 * LANES - D_GATE)], -1)

    def unpad_low(rkv_row, low_row):
        return jnp.concatenate([rkv_row, low_row[..., 0:D_DECAY], low_row[..., LANES:LANES + D_AAA],
                                low_row[..., 2 * LANES:2 * LANES + D_GATE]], -1)

    wi = w_in[0]
    w_rw = pad_low(wi[:, :rw_cols])
    w6 = jnp.concatenate([w_rw[:, :3 * C_RW], wi[:, rw_cols:]], 1).astype(BF16)
    wlow = w_rw[:, 3 * C_RW:].astype(BF16)
    pad_rows = lambda a, n: jnp.concatenate([a, jnp.zeros((n - a.shape[0], a.shape[1]), a.dtype)], 0)
    row = lambda a: a.reshape(1, -1)
    prm = {
        "mu": pad_low(rw_mu[0]).reshape(1, -1),
        "w0": row(rw_w0[0]), "a0": row(rw_a0[0]), "k_k": row(rw_k_k[0]), "k_a": row(rw_k_a[0]),
        "r_k": row(rw_r_k[0]), "gn_g": row(rw_gn_g[0]), "gn_b": row(rw_gn_b[0]),
        "w2": pad_rows(rw_w2[0], LANES), "a2": pad_rows(rw_a2[0], LANES), "g2": pad_rows(rw_g2[0], 2 * LANES),
    }
    g0, b0 = row(ln0_g), row(ln0_b)
    g1, b1 = row(ln1_g[0]), row(ln1_b[0])
    g2, b2 = row(ln2_g[0]), row(ln2_b[0])
    sb_gain = row(sb_norm_g[0])
    wo = w_out[0].astype(BF16)
    wo_a, wo_b = wo[:C_RW], wo[C_RW:]
    wr = w_router[0]
    wr_hi = wr.astype(BF16)
    wr_lo = (wr - wr_hi.astype(F32)).astype(BF16)
    wsg, wsu, wsd = w_sh_gate[0].astype(BF16), w_sh_up[0].astype(BF16), w_sh_down[0].astype(BF16)

    xp = x_prompt.reshape(bsz * seq, d)
    xs = x_sample.reshape(dbs * dseq, d)
    xm = meta_tokens.astype(x_prompt.dtype)
    rkv_m, _, k_m, v_m, low_m = _in_proj(xm, g0, b0, w6, wlow, N_META)
    rkv_p, q_p, k_p, v_p, low_p = _in_proj(xp, g0, b0, w6, wlow, _pick_tile(bsz * seq, 512))
    rkv_s, q_s, k_s, v_s, low_s = _in_proj(xs, g0, b0, w6, wlow, _pick_tile(dbs * dseq, 512))

    zero_shift = jnp.zeros((1, 1, 3 * C_RW + LOW_PAD), F32)
    zero_wkv = jnp.zeros((1, n_h, HEAD, HEAD), F32)
    _, wkv_m = _rwkv(rkv_m[None], low_m[None], zero_shift, zero_wkv, prm, N_META)
    shift_m = jnp.concatenate([rkv_m[-1:], low_m[-1:]], -1)[None]
    rw_p, wkv_p = _rwkv(rkv_p.reshape(bsz, seq, -1), low_p.reshape(bsz, seq, -1), shift_m, wkv_m, prm,
                        _pick_tile(seq, 64))
    shift_s0 = pad_low(state_rwkv_shift[0])
    rw_s, wkv_s = _rwkv(rkv_s.reshape(dbs, dseq, -1), low_s.reshape(dbs, dseq, -1), shift_s0,
                        state_rwkv_wkv[0], prm, dseq)

    c3 = lambda a, b_, l_: a.reshape(b_, l_, C_SB)
    sb_p = _attention(c3(q_p, bsz, seq), c3(k_p, bsz, seq), c3(v_p, bsz, seq), k_m[None], v_m[None], sb_gain,
                      _pick_tile(seq, 256), N_META)
    sb_s = _attention(c3(q_s, dbs, dseq), c3(k_s, dbs, dseq), c3(v_s, dbs, dseq),
                      cache_sb_k[0].reshape(dbs, past, C_SB), cache_sb_v[0].reshape(dbs, past, C_SB), sb_gain,
                      dseq, _pick_tile(past, 256))

    x_all = jnp.concatenate([xp, xs], 0)
    rw_all = jnp.concatenate([rw_p.reshape(bsz * seq, C_RW), rw_s.reshape(dbs * dseq, C_RW)], 0)
    sb_all = jnp.concatenate([sb_p.reshape(bsz * seq, C_SB), sb_s.reshape(dbs * dseq, C_SB)], 0)
    t_all = x_all.shape[0]
    tm = _pick_tile(t_all, 256)
    h, hb, logits = _out_proj(x_all, rw_all, sb_all, wo_a, wo_b, g0, b0, g1, b1, wr_hi, wr_lo, tm)

    idx, gate = _route(logits, router_bias[0])
    row_tok, row_w, blk_exp, n_valid, pos = _dispatch_plan(idx, gate, ne)
    expert_in = jnp.take(hb, row_tok, axis=0)
    expert_out = _experts(expert_in, row_w, blk_exp, n_valid, w_exp_gate[0], w_exp_up[0], w_exp_down[0])
    routed = jnp.sum(jnp.take(expert_out, pos, axis=0).astype(F32), axis=1)
    y = _final(h, hb, routed, wsg, wsu, wsd, g2, b2, tm)

    y_prompt = y[:bsz * seq].reshape(bsz, seq, d)
    y_sample = y[bsz * seq:].reshape(dbs, dseq, d)
    heads = lambda a, b_, l_: a.reshape(b_, l_, C_SB // HEAD, HEAD)

    def with_meta(m, p):
        m4 = jnp.broadcast_to(heads(m, 1, N_META), (bsz, N_META, C_SB // HEAD, HEAD))
        return jnp.concatenate([m4, heads(p, bsz, seq)], 1)[None]

    k_prompt = with_meta(k_m, k_p)
    v_prompt = with_meta(v_m, v_p)
    last = lambda a, b_, l_: a.reshape(b_, l_, -1)[:, -1:]
    shift_prompt = unpad_low(last(rkv_p, bsz, seq), last(low_p, bsz, seq))[None]
    shift_sample = unpad_low(last(rkv_s, dbs, dseq), last(low_s, dbs, dseq))[None]
    return (y_prompt, y_sample, k_prompt, v_prompt, wkv_p[None], shift_prompt,
            heads(k_s, dbs, dseq)[None], heads(v_s, dbs, dseq)[None], wkv_s[None], shift_sample)
```

```python
import functools

import jax
import jax.numpy as jnp
from jax import lax
from jax.experimental import pallas as pl
from jax.experimental.pallas import tpu as pltpu

F32 = jnp.float32
BF16 = jnp.bfloat16
HIGHEST = lax.Precision.HIGHEST

N_META = 16
HEAD = 64
C_RW = 1024
C_SB = 1024
D_DECAY = 64
D_AAA = 64
D_GATE = 160
LOW_PAD = 512
TOP_K = 8
N_GROUPS = 8
TOPK_GROUPS = 4
ROUTED_SCALE = 2.5
LN_EPS = 1e-5
GN_EPS = 64e-5
RMS_EPS = 1e-6
DEPTH = 1
DN_ALPHA = (2 * DEPTH) ** 0.25

LANES = 128
VMEM_LIMIT = 56 * 1024 * 1024
RW_HEADS_PER_STEP = 16
EXPERT_BLOCK = 256


def _cparams(sem):
    return pltpu.CompilerParams(dimension_semantics=sem, vmem_limit_bytes=VMEM_LIMIT)


def _layer_norm(x, g, b):
    mu = jnp.mean(x, -1, keepdims=True)
    xc = x - mu
    var = jnp.mean(xc * xc, -1, keepdims=True)
    return xc * lax.rsqrt(var + LN_EPS) * g + b


def _sigmoid(x):
    return 1.0 / (1.0 + jnp.exp(-x))


def _softplus(x):
    return jnp.maximum(x, 0.0) + jnp.log(1.0 + jnp.exp(-jnp.abs(x)))


def _dot(a, b, precision=None):
    return jnp.dot(a, b, preferred_element_type=F32, precision=precision)


def _dot_nt(a, b, precision=None):
    return lax.dot_general(a, b, (((1,), (1,)), ((), ())), preferred_element_type=F32, precision=precision)


def _dot_tn(a, b, precision=None):
    return lax.dot_general(a, b, (((0,), (0,)), ((), ())), preferred_element_type=F32, precision=precision)


def _split2(x):
    hi = x.astype(BF16)
    return hi, (x - hi.astype(F32)).astype(BF16)


def _split3(x):
    hi = x.astype(BF16)
    r1 = x - hi.astype(F32)
    mid = r1.astype(BF16)
    return hi, mid, (r1 - mid.astype(F32)).astype(BF16)


def _dot_x3(a, b, dot=_dot):
    ah, al = _split2(a)
    bh, bl = _split2(b)
    return dot(ah, bh) + dot(al, bh) + dot(ah, bl)


def _dot_bf(a, b, dot=_dot):
    return dot(a.astype(BF16), b.astype(BF16))


RW_PREC = {"tri": "bf", "dbl": "bf", "wu": "bf", "ro": "bf", "pq": "bf", "state": "x3"}


def _mm(site, a, b, dot=_dot):
    mode = RW_PREC[site]
    if mode == "bf":
        return _dot_bf(a, b, dot)
    if mode == "x3":
        return _dot_x3(a, b, dot)
    return dot(a, b, HIGHEST)


def _in_proj_kernel(x_ref, g_ref, b_ref, w_ref, wlow_ref, rkv_ref, q_ref, k_ref, v_ref, low_ref, xn_ref):
    j = pl.program_id(1)

    @pl.when(j == 0)
    def _():
        xn = _layer_norm(x_ref[...], g_ref[...], b_ref[...]).astype(BF16)
        xn_ref[...] = xn
        low_ref[...] = _dot(xn, wlow_ref[...])

    y = _dot(xn_ref[...], w_ref[...])

    @pl.when(j < 3)
    def _():
        rkv_ref[...] = y

    @pl.when(j == 3)
    def _():
        q_ref[...] = y

    @pl.when(j == 4)
    def _():
        k_ref[...] = y

    @pl.when(j == 5)
    def _():
        v_ref[...] = y


def _in_proj(x, ln_g, ln_b, w6, wlow, tm):
    t, d = x.shape
    assert t % tm == 0
    c = C_RW
    grid = (t // tm, 6)
    col = lambda i, j: (i, 0)
    return pl.pallas_call(
        _in_proj_kernel,
        grid=grid,
        in_specs=[
            pl.BlockSpec((tm, d), col),
            pl.BlockSpec((1, d), lambda i, j: (0, 0)),
            pl.BlockSpec((1, d), lambda i, j: (0, 0)),
            pl.BlockSpec((d, c), lambda i, j: (0, j)),
            pl.BlockSpec((d, LOW_PAD), lambda i, j: (0, 0)),
        ],
        out_specs=[
            pl.BlockSpec((tm, c), lambda i, j: (i, jnp.minimum(j, 2))),
            pl.BlockSpec((tm, c), col),
            pl.BlockSpec((tm, c), col),
            pl.BlockSpec((tm, c), col),
            pl.BlockSpec((tm, LOW_PAD), col),
        ],
        out_shape=[
            jax.ShapeDtypeStruct((t, 3 * c), F32),
            jax.ShapeDtypeStruct((t, c), F32),
            jax.ShapeDtypeStruct((t, c), F32),
            jax.ShapeDtypeStruct((t, c), F32),
            jax.ShapeDtypeStruct((t, LOW_PAD), F32),
        ],
        scratch_shapes=[pltpu.VMEM((tm, d), BF16)],
        compiler_params=_cparams(("parallel", "arbitrary")),
        name="in_proj",
    )(x, ln_g, ln_b, w6, wlow)


def _rwkv_kernel(r_ref, k_ref, v_ref, low_ref, sr_ref, sk_ref, sv_ref, slow_ref,
                 mur_ref, muk_ref, muv_ref, mulow_ref,
                 w0_ref, a0_ref, kk_ref, ka_ref, rk_ref, gng_ref, gnb_ref,
                 w2_ref, a2_ref, g2_ref, bd_ref, tri_ref, wkv0_ref,
                 out_ref, wkv_ref,
                 s_ref, pr_ref, pk_ref, pv_ref, plow_ref, *, tc, hps, n_dbl):
    c = pl.program_id(2)
    nc = pl.num_programs(2)

    @pl.when(c == 0)
    def _():
        s_ref[...] = wkv0_ref[...]
        pr_ref[...] = sr_ref[...]
        pk_ref[...] = sk_ref[...]
        pv_ref[...] = sv_ref[...]
        plow_ref[...] = slow_ref[...]

    def token_shift(cur_ref, prev_ref, mu_ref):
        cur = cur_ref[...]
        row = lax.broadcasted_iota(jnp.int32, cur.shape, 0)
        prev = jnp.where(row == 0, prev_ref[...], pltpu.roll(cur, 1, 0))
        prev_ref[...] = cur[tc - 1:tc, :]
        return cur + (prev - cur) * mu_ref[...]

    r = token_shift(r_ref, pr_ref, mur_ref)
    k = token_shift(k_ref, pk_ref, muk_ref)
    v = token_shift(v_ref, pv_ref, muv_ref)
    xl = token_shift(low_ref, plow_ref, mulow_ref)
    wd = xl[:, 0:LANES]
    ad = xl[:, LANES:2 * LANES]
    gd = xl[:, 2 * LANES:LOW_PAD]

    w_log = -_softplus(-(w0_ref[...] + _dot_x3(jnp.tanh(wd), w2_ref[...]))) - 0.5
    lw = -jnp.exp(w_log)
    a = _sigmoid(a0_ref[...] + _dot_x3(ad, a2_ref[...]))
    g = _dot_x3(_sigmoid(gd), g2_ref[...])

    bd = bd_ref[...]
    nb_lane = bd.shape[0]

    def head_sum(x):
        parts = []
        for j in range(x.shape[1] // nb_lane):
            hi, lo = _split2(x[:, j * nb_lane:(j + 1) * nb_lane])
            parts.append(_dot(hi, bd) + _dot(lo, bd))
        return parts[0] if len(parts) == 1 else jnp.concatenate(parts, axis=1)

    kk = k * kk_ref[...]
    kk = kk / jnp.maximum(jnp.sqrt(head_sum(kk * kk)), 1e-12)
    k2 = k * (1.0 + (a - 1.0) * ka_ref[...])
    bonus = head_sum(r * k2 * rk_ref[...]) * v

    ti = lax.broadcasted_iota(jnp.int32, (tc, tc), 0)
    si = lax.broadcasted_iota(jnp.int32, (tc, tc), 1)
    incl = si <= ti
    strict = si < ti
    tri = tri_ref[...]
    l1, l2, l3 = _split3(lw)
    cs = _dot(tri, l1) + _dot(tri, l2) + _dot(tri, l3)
    c_last = cs[tc - 1:tc, :]
    e_pos = jnp.exp(cs)
    e_neg = jnp.exp(-cs)
    e_end = jnp.exp(c_last - cs)
    b = kk * a
    r_t = r * e_pos
    al_t = -kk * jnp.exp(cs - lw)
    be_t = b * e_neg
    k_t = k2 * e_neg
    be_h = b * e_end
    k_h = k2 * e_end
    g_last = jnp.exp(c_last)

    eye_t = (si == ti).astype(F32)
    di = lax.broadcasted_iota(jnp.int32, (HEAD, HEAD), 0)
    dj = lax.broadcasted_iota(jnp.int32, (HEAD, HEAD), 1)
    eye_h = di == dj

    hs = range(hps)
    cut = lambda x: [x[:, h * HEAD:(h + 1) * HEAD] for h in hs]
    al_h, r_h, v_h, bet, kt, beh, kh, gl = (cut(x) for x in (al_t, r_t, v, be_t, k_t, be_h, k_h, g_last))
    a_mat = [jnp.where(strict, _mm("tri", al_h[h], bet[h], _dot_nt), 0.0) for h in hs]
    b_mat = [jnp.where(strict, _mm("tri", al_h[h], kt[h], _dot_nt), 0.0) for h in hs]
    ar_mat = [jnp.where(incl, _mm("tri", r_h[h], bet[h], _dot_nt), 0.0) for h in hs]
    br_mat = [jnp.where(incl, _mm("tri", r_h[h], kt[h], _dot_nt), 0.0) for h in hs]
    bv = [_mm("wu", b_mat[h], v_h[h]) for h in hs]
    m = [eye_t + a_mat[h] for h in hs]
    pw = a_mat
    for _ in range(n_dbl):
        pw = [_mm("dbl", pw[h], pw[h]) for h in hs]
        m = [m[h] + _mm("dbl", m[h], pw[h]) for h in hs]
    w_m = [_mm("wu", m[h], al_h[h]) for h in hs]
    u_loc = [_mm("wu", m[h], bv[h]) for h in hs]
    r_hat = [r_h[h] + _mm("ro", ar_mat[h], w_m[h]) for h in hs]
    o_loc = [_mm("ro", ar_mat[h], u_loc[h]) + _mm("ro", br_mat[h], v_h[h]) for h in hs]
    p_mat = [_mm("pq", w_m[h], beh[h], _dot_tn) + jnp.where(eye_h, gl[h], 0.0) for h in hs]
    q_mat = [_mm("pq", u_loc[h], beh[h], _dot_tn) + _mm("pq", v_h[h], kh[h], _dot_tn) for h in hs]
    s0 = [s_ref[h] for h in hs]
    o = [o_loc[h] + _mm("state", r_hat[h], s0[h], _dot_nt) for h in hs]
    for h in hs:
        s_ref[h] = _mm("state", s0[h], p_mat[h]) + q_mat[h]
    outs = []
    for h in hs:
        mu = jnp.mean(o[h], -1, keepdims=True)
        oc = o[h] - mu
        var = jnp.mean(oc * oc, -1, keepdims=True)
        outs.append(oc * lax.rsqrt(var + GN_EPS))
    on = jnp.concatenate(outs, axis=1)
    out_ref[...] = ((on * gng_ref[...] + gnb_ref[...] + bonus) * g).astype(out_ref.dtype)

    @pl.when(c == nc - 1)
    def _():
        wkv_ref[...] = s_ref[...]


def _rwkv(rkv, low, shift0, wkv0, prm, tc):
    bsz, length, _ = rkv.shape
    b0 = shift0.shape[0]
    assert length % tc == 0 and b0 in (1, bsz)
    hps = RW_HEADS_PER_STEP
    w = hps * HEAD
    n_h = C_RW // HEAD
    ng = n_h // hps
    nblk = C_RW // w
    n_dbl = max((tc - 1).bit_length() - 1, 0)
    sb = (lambda b: b) if b0 == bsz else (lambda b: 0)
    seq = lambda off: pl.BlockSpec((None, tc, w), lambda b, g, c: (b, c, off * nblk + g))
    row0 = lambda off: pl.BlockSpec((None, 1, w), lambda b, g, c: (sb(b), 0, off * nblk + g))
    vec = lambda off: pl.BlockSpec((1, w), lambda b, g, c: (0, off * nblk + g))
    low_blk = (3 * C_RW) // LOW_PAD
    nbd = 2 * LANES
    bd = (jnp.arange(nbd)[:, None] // HEAD == jnp.arange(nbd)[None, :] // HEAD).astype(BF16)
    tri = (jnp.arange(tc)[None, :] <= jnp.arange(tc)[:, None]).astype(BF16)
    kernel = functools.partial(_rwkv_kernel, tc=tc, hps=hps, n_dbl=n_dbl)
    return pl.pallas_call(
        kernel,
        grid=(bsz, ng, length // tc),
        in_specs=[
            seq(0), seq(1), seq(2),
            pl.BlockSpec((None, tc, LOW_PAD), lambda b, g, c: (b, c, 0)),
            row0(0), row0(1), row0(2),
            pl.BlockSpec((None, 1, LOW_PAD), lambda b, g, c: (sb(b), 0, low_blk)),
            vec(0), vec(1), vec(2),
            pl.BlockSpec((1, LOW_PAD), lambda b, g, c: (0, low_blk)),
            vec(0), vec(0), vec(0), vec(0), vec(0), vec(0), vec(0),
            pl.BlockSpec((LANES, w), lambda b, g, c: (0, g)),
            pl.BlockSpec((LANES, w), lambda b, g, c: (0, g)),
            pl.BlockSpec((2 * LANES, w), lambda b, g, c: (0, g)),
            pl.BlockSpec((nbd, nbd), lambda b, g, c: (0, 0)),
            pl.BlockSpec((tc, tc), lambda b, g, c: (0, 0)),
            pl.BlockSpec((None, hps, HEAD, HEAD), lambda b, g, c: (sb(b), g, 0, 0)),
        ],
        out_specs=[
            pl.BlockSpec((None, tc, w), lambda b, g, c: (b, c, g)),
            pl.BlockSpec((None, hps, HEAD, HEAD), lambda b, g, c: (b, g, 0, 0)),
        ],
        out_shape=[
            jax.ShapeDtypeStruct((bsz, length, C_RW), BF16),
            jax.ShapeDtypeStruct((bsz, n_h, HEAD, HEAD), F32),
        ],
        scratch_shapes=[
            pltpu.VMEM((hps, HEAD, HEAD), F32),
            pltpu.VMEM((1, w), F32), pltpu.VMEM((1, w), F32), pltpu.VMEM((1, w), F32),
            pltpu.VMEM((1, LOW_PAD), F32),
        ],
        compiler_params=_cparams(("parallel", "parallel", "arbitrary")),
        name="rwkv",
    )(rkv, rkv, rkv, low, shift0, shift0, shift0, shift0,
      prm["mu"], prm["mu"], prm["mu"], prm["mu"],
      prm["w0"], prm["a0"], prm["k_k"], prm["k_a"], prm["r_k"], prm["gn_g"], prm["gn_b"],
      prm["w2"], prm["a2"], prm["g2"], bd, tri, wkv0)


def _attn_kernel(q_ref, k_ref, v_ref, kp_ref, vp_ref, tri_ref, trip_ref, g_ref, out_ref, *scratch,
                 tq, pb, n_pre, pipe_main, pipe_pre):
    i = pl.program_id(2)
    q0 = pl.multiple_of(i * tq, tq)
    lane = lax.broadcasted_iota(jnp.int32, (tq, LANES), 1)
    first = lane < HEAD
    q2 = q_ref[pl.ds(q0, tq), :] * (HEAD ** -0.5)
    q_heads = (jnp.where(first, q2, 0.0).astype(BF16), jnp.where(first, 0.0, q2).astype(BF16))
    tri = tri_ref[...]
    trip = trip_ref[...]

    def suffix_sum(sp, tri_m):
        hi, lo = _split2(sp)
        return _dot(jnp.concatenate([hi, lo], axis=1), tri_m)

    def block(qh, kb, vb, tri_m, carry, acc, mask):
        z = _dot_nt(qh, kb)
        sp = _softplus(z)
        if mask is not None:
            sp = jnp.where(mask, sp, 0.0)
        suffix = suffix_sum(sp, tri_m)
        p = jnp.exp((z + carry) - suffix)
        if mask is not None:
            p = jnp.where(mask, p, 0.0)
        acc = acc + _dot(p.astype(BF16), vb)
        return carry - suffix[:, 0:1], acc

    def piped(kr, vr, bs, tri_m, count, block_of, bufs, state):
        z_s, d_s, tot_s = bufs
        z_s[...] = jnp.zeros_like(z_s)
        d_s[...] = jnp.zeros_like(d_s)
        tot_s[...] = jnp.zeros_like(tot_s)
        last = jnp.maximum(count - 1, 0)
        rows = lambda m: pl.ds(pl.multiple_of(block_of(jnp.clip(m, 0, last)) * bs, bs), bs)

        def body(n, st):
            st = list(st)
            live = n >= 2
            vb = vr[rows(n - 2), :].astype(BF16)
            for h in range(2):
                p = jnp.exp(d_s[h] + jnp.where(live, st[2 * h], -1e30))
                st[2 * h + 1] = st[2 * h + 1] + _dot(p.astype(BF16), vb)
                st[2 * h] = st[2 * h] - jnp.where(live, tot_s[h], 0.0)
            for h in range(2):
                z = z_s[h]
                suffix = suffix_sum(_softplus(z), tri_m)
                d_s[h] = z - suffix
                tot_s[h] = suffix[:, 0:1]
            kb = kr[rows(n), :].astype(BF16)
            for h in range(2):
                z_s[h] = _dot_nt(q_heads[h], kb)
            return tuple(st)

        return lax.fori_loop(0, jnp.where(count > 0, count + 2, 0), body, tuple(state))

    kd = k_ref[pl.ds(q0, tq), :].astype(BF16)
    vd = v_ref[pl.ds(q0, tq), :].astype(BF16)
    row = lax.broadcasted_iota(jnp.int32, (tq, tq), 0)
    colm = lax.broadcasted_iota(jnp.int32, (tq, tq), 1)
    causal = colm < row
    state = []
    for qh in q_heads:
        state.extend(block(qh, kd, vd, tri, jnp.zeros((tq, 1), F32), jnp.zeros((tq, LANES), F32), causal))

    if pipe_main:
        state = piped(k_ref, v_ref, tq, tri, i, lambda m: i - 1 - m, scratch[0:3], state)
    if pipe_pre:
        state = piped(kp_ref, vp_ref, pb, trip, n_pre, lambda m: n_pre - 1 - m, scratch[-3:], state)
    else:
        for t in range(n_pre):
            rows = pl.ds((n_pre - 1 - t) * pb, pb)
            kb = kp_ref[rows, :].astype(BF16)
            vb = vp_ref[rows, :].astype(BF16)
            state = list(state)
            state[0:2] = block(q_heads[0], kb, vb, trip, state[0], state[1], None)
            state[2:4] = block(q_heads[1], kb, vb, trip, state[2], state[3], None)

    o = jnp.where(first, state[1], state[3])
    sq = o * o
    s_a = jnp.sum(jnp.where(first, sq, 0.0), -1, keepdims=True)
    s_b = jnp.sum(sq, -1, keepdims=True) - s_a
    inv = jnp.where(first, lax.rsqrt(s_a / HEAD + RMS_EPS), lax.rsqrt(s_b / HEAD + RMS_EPS))
    out_ref[...] = (o * inv * g_ref[...]).astype(out_ref.dtype)


def _attention(q, k, v, kp, vp, gain, tq, pb):
    bsz, length, c = q.shape
    b0, plen, _ = kp.shape
    assert length % tq == 0 and plen % pb == 0 and b0 in (1, bsz)
    sb = (lambda b: b) if b0 == bsz else (lambda b: 0)
    n_pre = plen // pb
    n_main = length // tq
    pipe_main = n_main > 1
    pipe_pre = n_pre >= 3
    tri2 = lambda n: jnp.tile((jnp.arange(n)[:, None] >= jnp.arange(n)[None, :]).astype(BF16), (2, 1))
    stage_bufs = lambda bs: [pltpu.VMEM((2, tq, bs), F32), pltpu.VMEM((2, tq, bs), F32), pltpu.VMEM((2, tq, 1), F32)]
    full = pl.BlockSpec((None, length, LANES), lambda b, h, i: (b, 0, h))
    pre = pl.BlockSpec((None, plen, LANES), lambda b, h, i: (sb(b), 0, h))
    kernel = functools.partial(_attn_kernel, tq=tq, pb=pb, n_pre=n_pre, pipe_main=pipe_main, pipe_pre=pipe_pre)
    return pl.pallas_call(
        kernel,
        grid=(bsz, c // LANES, n_main),
        in_specs=[
            full, full, full, pre, pre,
            pl.BlockSpec((2 * tq, tq), lambda b, h, i: (0, 0)),
            pl.BlockSpec((2 * pb, pb), lambda b, h, i: (0, 0)),
            pl.BlockSpec((1, LANES), lambda b, h, i: (0, h)),
        ],
        out_specs=pl.BlockSpec((None, tq, LANES), lambda b, h, i: (b, i, h)),
        out_shape=jax.ShapeDtypeStruct((bsz, length, c), BF16),
        scratch_shapes=(stage_bufs(tq) if pipe_main else []) + (stage_bufs(pb) if pipe_pre else []),
        compiler_params=_cparams(("parallel", "parallel", "arbitrary")),
        name="attn",
    )(q, k, v, kp, vp, tri2(tq), tri2(pb), gain)


def _route_columns(logits_t, bias):
    ne, n = logits_t.shape
    gsz = ne // N_GROUPS
    neg = -jnp.inf
    scores = _sigmoid(logits_t)
    sel = scores + bias
    ig = lax.broadcasted_iota(jnp.int32, (gsz, n), 0)
    gs = []
    for g in range(N_GROUPS):
        xg = sel[g * gsz:(g + 1) * gsz, :]
        m1 = jnp.max(xg, axis=0, keepdims=True)
        first = jnp.min(jnp.where(xg == m1, ig, gsz), axis=0, keepdims=True)
        m2 = jnp.max(jnp.where(ig == first, neg, xg), axis=0, keepdims=True)
        gs.append(m1 + m2)
    kept = []
    for g in range(N_GROUPS):
        rank = jnp.zeros((1, n), jnp.int32)
        for g2 in range(N_GROUPS):
            if g2 == g:
                continue
            ahead = (gs[g2] >= gs[g]) if g2 < g else (gs[g2] > gs[g])
            rank = rank + ahead.astype(jnp.int32)
        kept.append(jnp.where(rank < TOPK_GROUPS, sel[g * gsz:(g + 1) * gsz, :], neg))
    cand = jnp.concatenate(kept, axis=0)
    ie = lax.broadcasted_iota(jnp.int32, (ne, n), 0)
    ids, gates = [], []
    for _ in range(TOP_K):
        m = jnp.max(cand, axis=0, keepdims=True)
        idx = jnp.min(jnp.where(cand == m, ie, ne), axis=0, keepdims=True)
        hit = ie == idx
        gates.append(jnp.sum(jnp.where(hit, scores, 0.0), axis=0, keepdims=True))
        ids.append(idx)
        cand = jnp.where(hit, neg, cand)
    gate = jnp.concatenate(gates, axis=0)
    gate = gate / jnp.sum(gate, axis=0, keepdims=True) * ROUTED_SCALE
    return jnp.concatenate(ids, axis=0), gate


def _out_proj_kernel(x_ref, rw_ref, sb_ref, wa_ref, wb_ref, g0_ref, b0_ref, g1_ref, b1_ref,
                     wrh_ref, wrl_ref, rb_ref, h_ref, hb_ref, idx_ref, gate_ref):
    xn = _layer_norm(x_ref[...], g0_ref[...], b0_ref[...])
    mix = _dot(rw_ref[...], wa_ref[...]) + _dot(sb_ref[...], wb_ref[...])
    h = _layer_norm(DN_ALPHA * xn + mix, g1_ref[...], b1_ref[...])
    h_ref[...] = h
    hi, lo = _split2(h)
    hb_ref[...] = hi
    wrh = wrh_ref[...]
    logits_t = _dot_nt(wrh, hi) + _dot_nt(wrh, lo) + _dot_nt(wrl_ref[...], hi)
    idx, gate = _route_columns(logits_t, rb_ref[...])
    idx_ref[...] = idx
    gate_ref[...] = gate


def _out_proj(x, rw, sbo, wa, wb, g0, b0, g1, b1, wrh, wrl, rbias, tm):
    t, d = x.shape
    assert t % tm == 0
    row = lambda w: pl.BlockSpec((tm, w), lambda i: (i, 0))
    col = pl.BlockSpec((TOP_K, tm), lambda i: (0, i))
    const = lambda a: pl.BlockSpec(a.shape, lambda i: (0, 0))
    return pl.pallas_call(
        _out_proj_kernel,
        grid=(t // tm,),
        in_specs=[row(d), row(C_RW), row(C_SB), const(wa), const(wb), const(g0), const(b0), const(g1), const(b1),
                  const(wrh), const(wrl), const(rbias)],
        out_specs=[row(d), row(d), col, col],
        out_shape=[jax.ShapeDtypeStruct((t, d), F32), jax.ShapeDtypeStruct((t, d), BF16),
                   jax.ShapeDtypeStruct((TOP_K, t), jnp.int32), jax.ShapeDtypeStruct((TOP_K, t), F32)],
        compiler_params=_cparams(("parallel",)),
        name="out_proj",
    )(x, rw, sbo, wa, wb, g0, b0, g1, b1, wrh, wrl, rbias)


def _expert_kernel(be_ref, nv_ref, xs_ref, rw_ref, wg_ref, wu_ref, wd_ref, out_ref,
                   wgb_ref, wub_ref, wdb_ref, prev_ref):
    i = pl.program_id(0)
    e = be_ref[i]

    @pl.when(i == 0)
    def _():
        prev_ref[0] = -1

    @pl.when(i < nv_ref[0])
    def _():
        @pl.when(e != prev_ref[0])
        def _():
            wgb_ref[...] = wg_ref[...].astype(BF16)
            wub_ref[...] = wu_ref[...].astype(BF16)
            wdb_ref[...] = wd_ref[...].astype(BF16)
            prev_ref[0] = e

        x = xs_ref[...]
        hg = _dot(x, wgb_ref[...])
        hu = _dot(x, wub_ref[...])
        hid = (hg * _sigmoid(hg) * hu).astype(BF16)
        out_ref[...] = (_dot(hid, wdb_ref[...]) * rw_ref[...]).astype(out_ref.dtype)

    @pl.when(i >= nv_ref[0])
    def _():
        out_ref[...] = jnp.zeros_like(out_ref)


def _experts(xs, row_w, blk_exp, n_valid, w_gate, w_up, w_down):
    n_rows, d = xs.shape
    _, _, de = w_gate.shape
    blk = EXPERT_BLOCK
    nb = n_rows // blk
    live = lambda i, be, nv: jnp.minimum(i, nv[0] - 1)
    grid_spec = pltpu.PrefetchScalarGridSpec(
        num_scalar_prefetch=2,
        grid=(nb,),
        in_specs=[
            pl.BlockSpec((blk, d), lambda i, be, nv: (live(i, be, nv), 0)),
            pl.BlockSpec((blk, 1), lambda i, be, nv: (live(i, be, nv), 0)),
            pl.BlockSpec((None, d, de), lambda i, be, nv: (be[i], 0, 0)),
            pl.BlockSpec((None, d, de), lambda i, be, nv: (be[i], 0, 0)),
            pl.BlockSpec((None, de, d), lambda i, be, nv: (be[i], 0, 0)),
        ],
        out_specs=pl.BlockSpec((blk, d), lambda i, be, nv: (i, 0)),
        scratch_shapes=[pltpu.VMEM((d, de), BF16), pltpu.VMEM((d, de), BF16), pltpu.VMEM((de, d), BF16),
                        pltpu.SMEM((1,), jnp.int32)],
    )
    return pl.pallas_call(
        _expert_kernel,
        grid_spec=grid_spec,
        out_shape=jax.ShapeDtypeStruct((n_rows, d), BF16),
        compiler_params=_cparams(("arbitrary",)),
        name="experts",
    )(blk_exp, n_valid, xs, row_w, w_gate, w_up, w_down)


def _final_kernel(h_ref, hb_ref, routed_ref, wg_ref, wu_ref, wd_ref, g_ref, b_ref, out_ref):
    x = hb_ref[...]
    hg = _dot(x, wg_ref[...])
    hu = _dot(x, wu_ref[...])
    shared = _dot((hg * _sigmoid(hg) * hu).astype(BF16), wd_ref[...])
    y = DN_ALPHA * h_ref[...] + routed_ref[...] + shared
    out_ref[...] = _layer_norm(y, g_ref[...], b_ref[...])


def _final(h, hb, routed, wg, wu, wd, g, b, tm):
    t, d = h.shape
    assert t % tm == 0
    row = pl.BlockSpec((tm, d), lambda i: (i, 0))
    const = lambda a: pl.BlockSpec(a.shape, lambda i: (0, 0))
    return pl.pallas_call(
        _final_kernel,
        grid=(t // tm,),
        in_specs=[row, row, row, const(wg), const(wu), const(wd), const(g), const(b)],
        out_specs=row,
        out_shape=jax.ShapeDtypeStruct((t, d), F32),
        compiler_params=_cparams(("parallel",)),
        name="final",
    )(h, hb, routed, wg, wu, wd, g, b)


def _dispatch_plan(idx_t, gate_t, ne):
    t = idx_t.shape[1]
    blk = EXPERT_BLOCK
    n_assign = t * TOP_K
    e_flat = idx_t.reshape(n_assign)
    e_sorted, order = lax.sort((e_flat, jnp.arange(n_assign, dtype=jnp.int32)), num_keys=1)
    counts = jnp.sum((e_flat[:, None] == jnp.arange(ne, dtype=jnp.int32)[None, :]).astype(jnp.int32), axis=0)
    padded = (counts + blk - 1) // blk * blk
    pad_end = jnp.cumsum(padded)
    pad_start = pad_end - padded
    raw_start = jnp.cumsum(counts) - counts
    dest = pad_start[e_sorted] + jnp.arange(n_assign, dtype=jnp.int32) - raw_start[e_sorted]
    nb = n_assign // blk + ne
    n_rows = nb * blk
    blk_start = jnp.arange(nb, dtype=jnp.int32) * blk
    blk_exp = jnp.minimum(jnp.sum((pad_end[None, :] <= blk_start[:, None]).astype(jnp.int32), axis=1), ne - 1)
    n_valid = (pad_end[-1] // blk).astype(jnp.int32).reshape(1)
    row_e = jnp.repeat(blk_exp, blk)
    within = jnp.arange(n_rows, dtype=jnp.int32) - pad_start[row_e]
    valid = within < counts[row_e]
    src = order[jnp.clip(raw_start[row_e] + within, 0, n_assign - 1)]
    row_tok = jnp.where(valid, src % t, 0)
    row_w = jnp.where(valid, gate_t.reshape(n_assign)[src], 0.0)
    pos = jnp.zeros((n_assign,), jnp.int32).at[order].set(dest).reshape(TOP_K, t)
    return row_tok, row_w.reshape(n_rows, 1), blk_exp, n_valid, pos


def _pick_tile(n, cap):
    t = cap
    while n % t:
        t //= 2
    return t


def kernel(x_prompt, x_sample, cache_sb_k, cache_sb_v, state_rwkv_wkv, state_rwkv_shift, meta_tokens, ln0_g, ln0_b, w_in, rw_mu, rw_w0, rw_w2, rw_a0, rw_a2, rw_g2, rw_k_k, rw_k_a, rw_r_k, rw_gn_g, rw_gn_b, sb_norm_g, w_out, ln1_g, ln1_b, w_router, router_bias, w_exp_gate, w_exp_up, w_exp_down, w_sh_gate, w_sh_up, w_sh_down, ln2_g, ln2_b):
    bsz, seq, d = x_prompt.shape
    dbs, dseq, _ = x_sample.shape
    past = cache_sb_k.shape[2]
    n_h = C_RW // HEAD
    ne = w_router.shape[-1]
    rw_cols = 3 * C_RW + D_DECAY + D_AAA + D_GATE
    o_wd, o_ad, o_gd = 3 * C_RW, 3 * C_RW + D_DECAY, 3 * C_RW + D_DECAY + D_AAA

    def pad_low(a):
        z = lambda n: jnp.zeros(a.shape[:-1] + (n,), a.dtype)
        return jnp.concatenate([a[..., :o_wd], a[..., o_wd:o_ad], z(LANES - D_DECAY), a[..., o_ad:o_gd],
                                z(LANES - D_AAA), a[..., o_gd:rw_cols], z(2 * LANES - D_GATE)], -1)

    def unpad_low(rkv_row, low_row):
        return jnp.concatenate([rkv_row, low_row[..., 0:D_DECAY], low_row[..., LANES:LANES + D_AAA],
                                low_row[..., 2 * LANES:2 * LANES + D_GATE]], -1)

    wi = w_in[0]
    w_rw = pad_low(wi[:, :rw_cols])
    w6 = jnp.concatenate([w_rw[:, :3 * C_RW], wi[:, rw_cols:]], 1).astype(BF16)
    wlow = w_rw[:, 3 * C_RW:].astype(BF16)
    pad_rows = lambda a, n: jnp.concatenate([a, jnp.zeros((n - a.shape[0], a.shape[1]), a.dtype)], 0)
    row = lambda a: a.reshape(1, -1)
    prm = {
        "mu": pad_low(rw_mu[0]).reshape(1, -1),
        "w0": row(rw_w0[0]), "a0": row(rw_a0[0]), "k_k": row(rw_k_k[0]), "k_a": row(rw_k_a[0]),
        "r_k": row(rw_r_k[0]), "gn_g": row(rw_gn_g[0]), "gn_b": row(rw_gn_b[0]),
        "w2": pad_rows(rw_w2[0], LANES), "a2": pad_rows(rw_a2[0], LANES), "g2": pad_rows(rw_g2[0], 2 * LANES),
    }
    g0, b0 = row(ln0_g), row(ln0_b)
    g1, b1 = row(ln1_g[0]), row(ln1_b[0])
    g2, b2 = row(ln2_g[0]), row(ln2_b[0])
    sb_gain = row(sb_norm_g[0])
    wo = w_out[0].astype(BF16)
    wo_a, wo_b = wo[:C_RW], wo[C_RW:]
    wr_hi, wr_lo = _split2(w_router[0].T)
    rbias = router_bias[0].astype(F32).reshape(ne, 1)
    wsg, wsu, wsd = w_sh_gate[0].astype(BF16), w_sh_up[0].astype(BF16), w_sh_down[0].astype(BF16)

    xp = x_prompt.reshape(bsz * seq, d)
    xs = x_sample.reshape(dbs * dseq, d)
    xm = meta_tokens.astype(x_prompt.dtype)
    rkv_m, _, k_m, v_m, low_m = _in_proj(xm, g0, b0, w6, wlow, N_META)
    rkv_p, q_p, k_p, v_p, low_p = _in_proj(xp, g0, b0, w6, wlow, _pick_tile(bsz * seq, 512))
    rkv_s, q_s, k_s, v_s, low_s = _in_proj(xs, g0, b0, w6, wlow, _pick_tile(dbs * dseq, 512))

    zero_shift = jnp.zeros((1, 1, 3 * C_RW + LOW_PAD), F32)
    zero_wkv = jnp.zeros((1, n_h, HEAD, HEAD), F32)
    _, wkv_m = _rwkv(rkv_m[None], low_m[None], zero_shift, zero_wkv, prm, N_META)
    shift_m = jnp.concatenate([rkv_m[-1:], low_m[-1:]], -1)[None]
    rw_p, wkv_p = _rwkv(rkv_p.reshape(bsz, seq, -1), low_p.reshape(bsz, seq, -1), shift_m, wkv_m, prm,
                        _pick_tile(seq, 64))
    shift_s0 = pad_low(state_rwkv_shift[0])
    rw_s, wkv_s = _rwkv(rkv_s.reshape(dbs, dseq, -1), low_s.reshape(dbs, dseq, -1), shift_s0,
                        state_rwkv_wkv[0], prm, dseq)

    c3 = lambda a, b_, l_: a.reshape(b_, l_, C_SB)
    sb_p = _attention(c3(q_p, bsz, seq), c3(k_p, bsz, seq), c3(v_p, bsz, seq), k_m[None], v_m[None], sb_gain,
                      _pick_tile(seq, 256), N_META)
    sb_s = _attention(c3(q_s, dbs, dseq), c3(k_s, dbs, dseq), c3(v_s, dbs, dseq),
                      cache_sb_k[0].reshape(dbs, past, C_SB), cache_sb_v[0].reshape(dbs, past, C_SB), sb_gain,
                      dseq, _pick_tile(past, 256))

    x_all = jnp.concatenate([xp, xs], 0)
    rw_all = jnp.concatenate([rw_p.reshape(bsz * seq, C_RW), rw_s.reshape(dbs * dseq, C_RW)], 0)
    sb_all = jnp.concatenate([sb_p.reshape(bsz * seq, C_SB), sb_s.reshape(dbs * dseq, C_SB)], 0)
    t_all = x_all.shape[0]
    tm = _pick_tile(t_all, 256)
    h, hb, idx_t, gate_t = _out_proj(x_all, rw_all, sb_all, wo_a, wo_b, g0, b0, g1, b1, wr_hi, wr_lo, rbias, tm)

    row_tok, row_w, blk_exp, n_valid, pos = _dispatch_plan(idx_t, gate_t, ne)
    expert_in = jnp.take(hb, row_tok, axis=0)
    expert_out = _experts(expert_in, row_w, blk_exp, n_valid, w_exp_gate[0], w_exp_up[0], w_exp_down[0])
    routed = jnp.sum(jnp.take(expert_out, pos, axis=0).astype(F32), axis=0)
    y = _final(h, hb, routed, wsg, wsu, wsd, g2, b2, tm)

    y_prompt = y[:bsz * seq].reshape(bsz, seq, d)
    y_sample = y[bsz * seq:].reshape(dbs, dseq, d)
    heads = lambda a, b_, l_: a.reshape(b_, l_, C_SB // HEAD, HEAD)

    def with_meta(m, p):
        m4 = jnp.broadcast_to(heads(m, 1, N_META), (bsz, N_META, C_SB // HEAD, HEAD))
        return jnp.concatenate([m4, heads(p, bsz, seq)], 1)[None]

    k_prompt = with_meta(k_m, k_p)
    v_prompt = with_meta(v_m, v_p)
    last = lambda a, b_, l_: a.reshape(b_, l_, -1)[:, -1:]
    shift_prompt = unpad_low(last(rkv_p, bsz, seq), last(low_p, bsz, seq))[None]
    shift_sample = unpad_low(last(rkv_s, dbs, dseq), last(low_s, dbs, dseq))[None]
    return (y_prompt, y_sample, k_prompt, v_prompt, wkv_p[None], shift_prompt,
            heads(k_s, dbs, dseq)[None], heads(v_s, dbs, dseq)[None], wkv_s[None], shift_sample)
```

```python
import functools

import jax
import jax.numpy as jnp
from jax import lax
from jax.experimental import pallas as pl
from jax.experimental.pallas import tpu as pltpu

F32 = jnp.float32
BF16 = jnp.bfloat16
HIGHEST = lax.Precision.HIGHEST

N_META = 16
HEAD = 64
C_RW = 1024
C_SB = 1024
D_DECAY = 64
D_AAA = 64
D_GATE = 160
LOW_PAD = 512
TOP_K = 8
N_GROUPS = 8
TOPK_GROUPS = 4
ROUTED_SCALE = 2.5
LN_EPS = 1e-5
GN_EPS = 64e-5
RMS_EPS = 1e-6
DEPTH = 1
DN_ALPHA = (2 * DEPTH) ** 0.25

LANES = 128
VMEM_LIMIT = 56 * 1024 * 1024
RW_HEADS_PER_STEP = 16
EXPERT_BLOCK = 256


def _cparams(sem):
    return pltpu.CompilerParams(dimension_semantics=sem, vmem_limit_bytes=VMEM_LIMIT)


def _layer_norm(x, g, b):
    mu = jnp.mean(x, -1, keepdims=True)
    xc = x - mu
    var = jnp.mean(xc * xc, -1, keepdims=True)
    return xc * lax.rsqrt(var + LN_EPS) * g + b


def _sigmoid(x):
    return 1.0 / (1.0 + jnp.exp(-x))


def _softplus(x):
    return jnp.maximum(x, 0.0) + jnp.log(1.0 + jnp.exp(-jnp.abs(x)))


def _dot(a, b, precision=None):
    return jnp.dot(a, b, preferred_element_type=F32, precision=precision)


def _dot_nt(a, b, precision=None):
    return lax.dot_general(a, b, (((1,), (1,)), ((), ())), preferred_element_type=F32, precision=precision)


def _dot_tn(a, b, precision=None):
    return lax.dot_general(a, b, (((0,), (0,)), ((), ())), preferred_element_type=F32, precision=precision)


def _split2(x):
    hi = x.astype(BF16)
    return hi, (x - hi.astype(F32)).astype(BF16)


def _split3(x):
    hi = x.astype(BF16)
    r1 = x - hi.astype(F32)
    mid = r1.astype(BF16)
    return hi, mid, (r1 - mid.astype(F32)).astype(BF16)


def _dot_x3(a, b, dot=_dot):
    ah, al = _split2(a)
    bh, bl = _split2(b)
    return dot(ah, bh) + dot(al, bh) + dot(ah, bl)


def _dot_bf(a, b, dot=_dot):
    return dot(a.astype(BF16), b.astype(BF16))


RW_PREC = {"tri": "bf", "dbl": "bf", "wu": "bf", "ro": "bf", "pq": "bf", "state": "x3"}


def _mm(site, a, b, dot=_dot):
    mode = RW_PREC[site]
    if mode == "bf":
        return _dot_bf(a, b, dot)
    if mode == "x3":
        return _dot_x3(a, b, dot)
    return dot(a, b, HIGHEST)


def _in_proj_kernel(x_ref, g_ref, b_ref, w_ref, wlow_ref, rkv_ref, q_ref, k_ref, v_ref, low_ref, xn_ref):
    j = pl.program_id(1)

    @pl.when(j == 0)
    def _():
        xn = _layer_norm(x_ref[...], g_ref[...], b_ref[...]).astype(BF16)
        xn_ref[...] = xn
        low_ref[...] = _dot(xn, wlow_ref[...])

    y = _dot(xn_ref[...], w_ref[...])

    @pl.when(j < 3)
    def _():
        rkv_ref[...] = y

    @pl.when(j == 3)
    def _():
        q_ref[...] = y

    @pl.when(j == 4)
    def _():
        k_ref[...] = y

    @pl.when(j == 5)
    def _():
        v_ref[...] = y


def _in_proj(x, ln_g, ln_b, w6, wlow, tm):
    t, d = x.shape
    assert t % tm == 0
    c = C_RW
    grid = (t // tm, 6)
    col = lambda i, j: (i, 0)
    return pl.pallas_call(
        _in_proj_kernel,
        grid=grid,
        in_specs=[
            pl.BlockSpec((tm, d), col),
            pl.BlockSpec((1, d), lambda i, j: (0, 0)),
            pl.BlockSpec((1, d), lambda i, j: (0, 0)),
            pl.BlockSpec((d, c), lambda i, j: (0, j)),
            pl.BlockSpec((d, LOW_PAD), lambda i, j: (0, 0)),
        ],
        out_specs=[
            pl.BlockSpec((tm, c), lambda i, j: (i, jnp.minimum(j, 2))),
            pl.BlockSpec((tm, c), col),
            pl.BlockSpec((tm, c), col),
            pl.BlockSpec((tm, c), col),
            pl.BlockSpec((tm, LOW_PAD), col),
        ],
        out_shape=[
            jax.ShapeDtypeStruct((t, 3 * c), F32),
            jax.ShapeDtypeStruct((t, c), F32),
            jax.ShapeDtypeStruct((t, c), F32),
            jax.ShapeDtypeStruct((t, c), F32),
            jax.ShapeDtypeStruct((t, LOW_PAD), F32),
        ],
        scratch_shapes=[pltpu.VMEM((tm, d), BF16)],
        compiler_params=_cparams(("parallel", "arbitrary")),
        name="in_proj",
    )(x, ln_g, ln_b, w6, wlow)


def _rwkv_kernel(r_ref, k_ref, v_ref, low_ref, sr_ref, sk_ref, sv_ref, slow_ref,
                 mur_ref, muk_ref, muv_ref, mulow_ref,
                 w0_ref, a0_ref, kk_ref, ka_ref, rk_ref, gng_ref, gnb_ref,
                 w2_ref, a2_ref, g2_ref, bd_ref, tri_ref, wkv0_ref,
                 out_ref, wkv_ref,
                 s_ref, pr_ref, pk_ref, pv_ref, plow_ref, *, tc, hps, n_dbl):
    c = pl.program_id(2)
    nc = pl.num_programs(2)

    @pl.when(c == 0)
    def _():
        s_ref[...] = wkv0_ref[...]
        pr_ref[...] = sr_ref[...]
        pk_ref[...] = sk_ref[...]
        pv_ref[...] = sv_ref[...]
        plow_ref[...] = slow_ref[...]

    def token_shift(cur_ref, prev_ref, mu_ref):
        cur = cur_ref[...]
        row = lax.broadcasted_iota(jnp.int32, cur.shape, 0)
        prev = jnp.where(row == 0, prev_ref[...], pltpu.roll(cur, 1, 0))
        prev_ref[...] = cur[tc - 1:tc, :]
        return cur + (prev - cur) * mu_ref[...]

    r = token_shift(r_ref, pr_ref, mur_ref)
    k = token_shift(k_ref, pk_ref, muk_ref)
    v = token_shift(v_ref, pv_ref, muv_ref)
    xl = token_shift(low_ref, plow_ref, mulow_ref)
    wd = xl[:, 0:LANES]
    ad = xl[:, LANES:2 * LANES]
    gd = xl[:, 2 * LANES:LOW_PAD]

    w_log = -_softplus(-(w0_ref[...] + _dot_x3(jnp.tanh(wd), w2_ref[...]))) - 0.5
    lw = -jnp.exp(w_log)
    a = _sigmoid(a0_ref[...] + _dot_x3(ad, a2_ref[...]))
    g = _dot_x3(_sigmoid(gd), g2_ref[...])

    bd = bd_ref[...]
    nb_lane = bd.shape[0]

    def head_sum(x):
        parts = []
        for j in range(x.shape[1] // nb_lane):
            hi, lo = _split2(x[:, j * nb_lane:(j + 1) * nb_lane])
            parts.append(_dot(hi, bd) + _dot(lo, bd))
        return parts[0] if len(parts) == 1 else jnp.concatenate(parts, axis=1)

    kk = k * kk_ref[...]
    kk = kk / jnp.maximum(jnp.sqrt(head_sum(kk * kk)), 1e-12)
    k2 = k * (1.0 + (a - 1.0) * ka_ref[...])
    bonus = head_sum(r * k2 * rk_ref[...]) * v

    ti = lax.broadcasted_iota(jnp.int32, (tc, tc), 0)
    si = lax.broadcasted_iota(jnp.int32, (tc, tc), 1)
    incl = si <= ti
    strict = si < ti
    tri = tri_ref[...]
    l1, l2, l3 = _split3(lw)
    cs = _dot(tri, l1) + _dot(tri, l2) + _dot(tri, l3)
    c_last = cs[tc - 1:tc, :]
    e_pos = jnp.exp(cs)
    e_neg = jnp.exp(-cs)
    e_end = jnp.exp(c_last - cs)
    b = kk * a
    r_t = r * e_pos
    al_t = -kk * jnp.exp(cs - lw)
    be_t = b * e_neg
    k_t = k2 * e_neg
    be_h = b * e_end
    k_h = k2 * e_end
    g_last = jnp.exp(c_last)

    eye_t = (si == ti).astype(F32)
    di = lax.broadcasted_iota(jnp.int32, (HEAD, HEAD), 0)
    dj = lax.broadcasted_iota(jnp.int32, (HEAD, HEAD), 1)
    eye_h = di == dj

    hs = range(hps)
    cut = lambda x: [x[:, h * HEAD:(h + 1) * HEAD] for h in hs]
    al_h, r_h, v_h, bet, kt, beh, kh, gl = (cut(x) for x in (al_t, r_t, v, be_t, k_t, be_h, k_h, g_last))
    a_mat = [jnp.where(strict, _mm("tri", al_h[h], bet[h], _dot_nt), 0.0) for h in hs]
    b_mat = [jnp.where(strict, _mm("tri", al_h[h], kt[h], _dot_nt), 0.0) for h in hs]
    ar_mat = [jnp.where(incl, _mm("tri", r_h[h], bet[h], _dot_nt), 0.0) for h in hs]
    br_mat = [jnp.where(incl, _mm("tri", r_h[h], kt[h], _dot_nt), 0.0) for h in hs]
    bv = [_mm("wu", b_mat[h], v_h[h]) for h in hs]
    m = [eye_t + a_mat[h] for h in hs]
    pw = a_mat
    for _ in range(n_dbl):
        pw = [_mm("dbl", pw[h], pw[h]) for h in hs]
        m = [m[h] + _mm("dbl", m[h], pw[h]) for h in hs]
    w_m = [_mm("wu", m[h], al_h[h]) for h in hs]
    u_loc = [_mm("wu", m[h], bv[h]) for h in hs]
    r_hat = [r_h[h] + _mm("ro", ar_mat[h], w_m[h]) for h in hs]
    o_loc = [_mm("ro", ar_mat[h], u_loc[h]) + _mm("ro", br_mat[h], v_h[h]) for h in hs]
    p_mat = [_mm("pq", w_m[h], beh[h], _dot_tn) + jnp.where(eye_h, gl[h], 0.0) for h in hs]
    q_mat = [_mm("pq", u_loc[h], beh[h], _dot_tn) + _mm("pq", v_h[h], kh[h], _dot_tn) for h in hs]
    s0 = [s_ref[h] for h in hs]
    o = [o_loc[h] + _mm("state", r_hat[h], s0[h], _dot_nt) for h in hs]
    for h in hs:
        s_ref[h] = _mm("state", s0[h], p_mat[h]) + q_mat[h]
    outs = []
    for h in hs:
        mu = jnp.mean(o[h], -1, keepdims=True)
        oc = o[h] - mu
        var = jnp.mean(oc * oc, -1, keepdims=True)
        outs.append(oc * lax.rsqrt(var + GN_EPS))
    on = jnp.concatenate(outs, axis=1)
    out_ref[...] = ((on * gng_ref[...] + gnb_ref[...] + bonus) * g).astype(out_ref.dtype)

    @pl.when(c == nc - 1)
    def _():
        wkv_ref[...] = s_ref[...]


def _rwkv(rkv, low, shift0, wkv0, prm, tc):
    bsz, length, _ = rkv.shape
    b0 = shift0.shape[0]
    assert length % tc == 0 and b0 in (1, bsz)
    hps = RW_HEADS_PER_STEP
    w = hps * HEAD
    n_h = C_RW // HEAD
    ng = n_h // hps
    nblk = C_RW // w
    n_dbl = max((tc - 1).bit_length() - 1, 0)
    sb = (lambda b: b) if b0 == bsz else (lambda b: 0)
    seq = lambda off: pl.BlockSpec((None, tc, w), lambda b, g, c: (b, c, off * nblk + g))
    row0 = lambda off: pl.BlockSpec((None, 1, w), lambda b, g, c: (sb(b), 0, off * nblk + g))
    vec = lambda off: pl.BlockSpec((1, w), lambda b, g, c: (0, off * nblk + g))
    low_blk = (3 * C_RW) // LOW_PAD
    nbd = 2 * LANES
    bd = (jnp.arange(nbd)[:, None] // HEAD == jnp.arange(nbd)[None, :] // HEAD).astype(BF16)
    tri = (jnp.arange(tc)[None, :] <= jnp.arange(tc)[:, None]).astype(BF16)
    kernel = functools.partial(_rwkv_kernel, tc=tc, hps=hps, n_dbl=n_dbl)
    return pl.pallas_call(
        kernel,
        grid=(bsz, ng, length // tc),
        in_specs=[
            seq(0), seq(1), seq(2),
            pl.BlockSpec((None, tc, LOW_PAD), lambda b, g, c: (b, c, 0)),
            row0(0), row0(1), row0(2),
            pl.BlockSpec((None, 1, LOW_PAD), lambda b, g, c: (sb(b), 0, low_blk)),
            vec(0), vec(1), vec(2),
            pl.BlockSpec((1, LOW_PAD), lambda b, g, c: (0, low_blk)),
            vec(0), vec(0), vec(0), vec(0), vec(0), vec(0), vec(0),
            pl.BlockSpec((LANES, w), lambda b, g, c: (0, g)),
            pl.BlockSpec((LANES, w), lambda b, g, c: (0, g)),
            pl.BlockSpec((2 * LANES, w), lambda b, g, c: (0, g)),
            pl.BlockSpec((nbd, nbd), lambda b, g, c: (0, 0)),
            pl.BlockSpec((tc, tc), lambda b, g, c: (0, 0)),
            pl.BlockSpec((None, hps, HEAD, HEAD), lambda b, g, c: (sb(b), g, 0, 0)),
        ],
        out_specs=[
            pl.BlockSpec((None, tc, w), lambda b, g, c: (b, c, g)),
            pl.BlockSpec((None, hps, HEAD, HEAD), lambda b, g, c: (b, g, 0, 0)),
        ],
        out_shape=[
            jax.ShapeDtypeStruct((bsz, length, C_RW), BF16),
            jax.ShapeDtypeStruct((bsz, n_h, HEAD, HEAD), F32),
        ],
        scratch_shapes=[
            pltpu.VMEM((hps, HEAD, HEAD), F32),
            pltpu.VMEM((1, w), F32), pltpu.VMEM((1, w), F32), pltpu.VMEM((1, w), F32),
            pltpu.VMEM((1, LOW_PAD), F32),
        ],
        compiler_params=_cparams(("parallel", "parallel", "arbitrary")),
        name="rwkv",
    )(rkv, rkv, rkv, low, shift0, shift0, shift0, shift0,
      prm["mu"], prm["mu"], prm["mu"], prm["mu"],
      prm["w0"], prm["a0"], prm["k_k"], prm["k_a"], prm["r_k"], prm["gn_g"], prm["gn_b"],
      prm["w2"], prm["a2"], prm["g2"], bd, tri, wkv0)


def _attn_kernel(q_ref, k_ref, v_ref, kp_ref, vp_ref, tri_ref, trip_ref, g_ref, out_ref, *scratch,
                 tq, pb, n_pre, pipe_main, pipe_pre):
    i = pl.program_id(2)
    q0 = pl.multiple_of(i * tq, tq)
    lane = lax.broadcasted_iota(jnp.int32, (tq, LANES), 1)
    first = lane < HEAD
    q2 = q_ref[pl.ds(q0, tq), :] * (HEAD ** -0.5)
    q_heads = (jnp.where(first, q2, 0.0).astype(BF16), jnp.where(first, 0.0, q2).astype(BF16))
    tri = tri_ref[...]
    trip = trip_ref[...]

    def suffix_sum(sp, tri_m):
        hi, lo = _split2(sp)
        return _dot(jnp.concatenate([hi, lo], axis=1), tri_m)

    def block(qh, kb, vb, tri_m, carry, acc, mask):
        z = _dot_nt(qh, kb)
        sp = _softplus(z)
        if mask is not None:
            sp = jnp.where(mask, sp, 0.0)
        suffix = suffix_sum(sp, tri_m)
        p = jnp.exp((z + carry) - suffix)
        if mask is not None:
            p = jnp.where(mask, p, 0.0)
        acc = acc + _dot(p.astype(BF16), vb)
        return carry - suffix[:, 0:1], acc

    def piped(kr, vr, bs, tri_m, count, block_of, bufs, state):
        z_s, d_s, tot_s = bufs
        z_s[...] = jnp.zeros_like(z_s)
        d_s[...] = jnp.zeros_like(d_s)
        tot_s[...] = jnp.zeros_like(tot_s)
        last = jnp.maximum(count - 1, 0)
        rows = lambda m: pl.ds(pl.multiple_of(block_of(jnp.clip(m, 0, last)) * bs, bs), bs)

        def body(n, st):
            st = list(st)
            live = n >= 2
            vb = vr[rows(n - 2), :].astype(BF16)
            for h in range(2):
                p = jnp.exp(d_s[h] + jnp.where(live, st[2 * h], -1e30))
                st[2 * h + 1] = st[2 * h + 1] + _dot(p.astype(BF16), vb)
                st[2 * h] = st[2 * h] - jnp.where(live, tot_s[h], 0.0)
            for h in range(2):
                z = z_s[h]
                suffix = suffix_sum(_softplus(z), tri_m)
                d_s[h] = z - suffix
                tot_s[h] = suffix[:, 0:1]
            kb = kr[rows(n), :].astype(BF16)
            for h in range(2):
                z_s[h] = _dot_nt(q_heads[h], kb)
            return tuple(st)

        return lax.fori_loop(0, jnp.where(count > 0, count + 2, 0), body, tuple(state))

    kd = k_ref[pl.ds(q0, tq), :].astype(BF16)
    vd = v_ref[pl.ds(q0, tq), :].astype(BF16)
    row = lax.broadcasted_iota(jnp.int32, (tq, tq), 0)
    colm = lax.broadcasted_iota(jnp.int32, (tq, tq), 1)
    causal = colm < row
    state = []
    for qh in q_heads:
        state.extend(block(qh, kd, vd, tri, jnp.zeros((tq, 1), F32), jnp.zeros((tq, LANES), F32), causal))

    if pipe_main:
        state = piped(k_ref, v_ref, tq, tri, i, lambda m: i - 1 - m, scratch[0:3], state)
    if pipe_pre:
        state = piped(kp_ref, vp_ref, pb, trip, n_pre, lambda m: n_pre - 1 - m, scratch[-3:], state)
    else:
        for t in range(n_pre):
            rows = pl.ds((n_pre - 1 - t) * pb, pb)
            kb = kp_ref[rows, :].astype(BF16)
            vb = vp_ref[rows, :].astype(BF16)
            state = list(state)
            state[0:2] = block(q_heads[0], kb, vb, trip, state[0], state[1], None)
            state[2:4] = block(q_heads[1], kb, vb, trip, state[2], state[3], None)

    o = jnp.where(first, state[1], state[3])
    sq = o * o
    s_a = jnp.sum(jnp.where(first, sq, 0.0), -1, keepdims=True)
    s_b = jnp.sum(sq, -1, keepdims=True) - s_a
    inv = jnp.where(first, lax.rsqrt(s_a / HEAD + RMS_EPS), lax.rsqrt(s_b / HEAD + RMS_EPS))
    out_ref[...] = (o * inv * g_ref[...]).astype(out_ref.dtype)


def _attention(q, k, v, kp, vp, gain, tq, pb):
    bsz, length, c = q.shape
    b0, plen, _ = kp.shape
    assert length % tq == 0 and plen % pb == 0 and b0 in (1, bsz)
    sb = (lambda b: b) if b0 == bsz else (lambda b: 0)
    n_pre = plen // pb
    n_main = length // tq
    pipe_main = n_main > 1
    pipe_pre = n_pre >= 3
    tri2 = lambda n: jnp.tile((jnp.arange(n)[:, None] >= jnp.arange(n)[None, :]).astype(BF16), (2, 1))
    stage_bufs = lambda bs: [pltpu.VMEM((2, tq, bs), F32), pltpu.VMEM((2, tq, bs), F32), pltpu.VMEM((2, tq, 1), F32)]
    full = pl.BlockSpec((None, length, LANES), lambda b, h, i: (b, 0, h))
    pre = pl.BlockSpec((None, plen, LANES), lambda b, h, i: (sb(b), 0, h))
    kernel = functools.partial(_attn_kernel, tq=tq, pb=pb, n_pre=n_pre, pipe_main=pipe_main, pipe_pre=pipe_pre)
    return pl.pallas_call(
        kernel,
        grid=(bsz, c // LANES, n_main),
        in_specs=[
            full, full, full, pre, pre,
            pl.BlockSpec((2 * tq, tq), lambda b, h, i: (0, 0)),
            pl.BlockSpec((2 * pb, pb), lambda b, h, i: (0, 0)),
            pl.BlockSpec((1, LANES), lambda b, h, i: (0, h)),
        ],
        out_specs=pl.BlockSpec((None, tq, LANES), lambda b, h, i: (b, i, h)),
        out_shape=jax.ShapeDtypeStruct((bsz, length, c), BF16),
        scratch_shapes=(stage_bufs(tq) if pipe_main else []) + (stage_bufs(pb) if pipe_pre else []),
        compiler_params=_cparams(("parallel", "parallel", "arbitrary")),
        name="attn",
    )(q, k, v, kp, vp, tri2(tq), tri2(pb), gain)


def _route_columns(logits_t, bias):
    ne, n = logits_t.shape
    gsz = ne // N_GROUPS
    neg = -jnp.inf
    scores = _sigmoid(logits_t)
    sel = scores + bias
    ig = lax.broadcasted_iota(jnp.int32, (gsz, n), 0)
    gs = []
    for g in range(N_GROUPS):
        xg = sel[g * gsz:(g + 1) * gsz, :]
        m1 = jnp.max(xg, axis=0, keepdims=True)
        first = jnp.min(jnp.where(xg == m1, ig, gsz), axis=0, keepdims=True)
        m2 = jnp.max(jnp.where(ig == first, neg, xg), axis=0, keepdims=True)
        gs.append(m1 + m2)
    kept = []
    for g in range(N_GROUPS):
        rank = jnp.zeros((1, n), jnp.int32)
        for g2 in range(N_GROUPS):
            if g2 == g:
                continue
            ahead = (gs[g2] >= gs[g]) if g2 < g else (gs[g2] > gs[g])
            rank = rank + ahead.astype(jnp.int32)
        kept.append(jnp.where(rank < TOPK_GROUPS, sel[g * gsz:(g + 1) * gsz, :], neg))
    cand = jnp.concatenate(kept, axis=0)
    ie = lax.broadcasted_iota(jnp.int32, (ne, n), 0)
    ids, gates = [], []
    for _ in range(TOP_K):
        m = jnp.max(cand, axis=0, keepdims=True)
        idx = jnp.min(jnp.where(cand == m, ie, ne), axis=0, keepdims=True)
        hit = ie == idx
        gates.append(jnp.sum(jnp.where(hit, scores, 0.0), axis=0, keepdims=True))
        ids.append(idx)
        cand = jnp.where(hit, neg, cand)
    gate = jnp.concatenate(gates, axis=0)
    gate = gate / jnp.sum(gate, axis=0, keepdims=True) * ROUTED_SCALE
    return jnp.concatenate(ids, axis=0), gate


def _out_proj_kernel(x_ref, rw_ref, sb_ref, wa_ref, wb_ref, g0_ref, b0_ref, g1_ref, b1_ref,
                     wrh_ref, wrl_ref, rb_ref, h_ref, hb_ref, idx_ref, gate_ref):
    xn = _layer_norm(x_ref[...], g0_ref[...], b0_ref[...])
    mix = _dot(rw_ref[...], wa_ref[...]) + _dot(sb_ref[...], wb_ref[...])
    h = _layer_norm(DN_ALPHA * xn + mix, g1_ref[...], b1_ref[...])
    h_ref[...] = h
    hi, lo = _split2(h)
    hb_ref[...] = hi
    wrh = wrh_ref[...]
    logits_t = _dot_nt(wrh, hi) + _dot_nt(wrh, lo) + _dot_nt(wrl_ref[...], hi)
    idx, gate = _route_columns(logits_t, rb_ref[...])
    idx_ref[...] = idx
    gate_ref[...] = gate


def _out_proj(x, rw, sbo, wa, wb, g0, b0, g1, b1, wrh, wrl, rbias, tm):
    t, d = x.shape
    assert t % tm == 0
    row = lambda w: pl.BlockSpec((tm, w), lambda i: (i, 0))
    col = pl.BlockSpec((TOP_K, tm), lambda i: (0, i))
    const = lambda a: pl.BlockSpec(a.shape, lambda i: (0, 0))
    return pl.pallas_call(
        _out_proj_kernel,
        grid=(t // tm,),
        in_specs=[row(d), row(C_RW), row(C_SB), const(wa), const(wb), const(g0), const(b0), const(g1), const(b1),
                  const(wrh), const(wrl), const(rbias)],
        out_specs=[row(d), row(d), col, col],
        out_shape=[jax.ShapeDtypeStruct((t, d), F32), jax.ShapeDtypeStruct((t, d), BF16),
                   jax.ShapeDtypeStruct((TOP_K, t), jnp.int32), jax.ShapeDtypeStruct((TOP_K, t), F32)],
        compiler_params=_cparams(("parallel",)),
        name="out_proj",
    )(x, rw, sbo, wa, wb, g0, b0, g1, b1, wrh, wrl, rbias)


def _expert_kernel(be_ref, nv_ref, tok_ref, nxt_ref, rw_ref, h_hbm, wg_ref, wu_ref, wd_ref, out_ref,
                   xbuf, sems, wgb_ref, wub_ref, wdb_ref, prev_ref):
    i = pl.program_id(0)
    nv = nv_ref[0]
    e = be_ref[i]
    slot = lax.rem(i, 2)
    blk = xbuf.shape[1]

    def row_copy(src_row, r, s):
        return pltpu.make_async_copy(h_hbm.at[pl.ds(src_row, 1), :], xbuf.at[s, pl.ds(r, 1), :], sems.at[s])

    @pl.when(i == 0)
    def _():
        prev_ref[0] = -1
        for r in range(blk):
            row_copy(tok_ref[0, r], r, 0).start()

    @pl.when(i < nv)
    def _():
        @pl.when(e != prev_ref[0])
        def _():
            wgb_ref[...] = wg_ref[...].astype(BF16)
            wub_ref[...] = wu_ref[...].astype(BF16)
            wdb_ref[...] = wd_ref[...].astype(BF16)
            prev_ref[0] = e

        for r in range(blk):
            row_copy(0, r, slot).wait()
        for r in range(blk):
            row_copy(nxt_ref[0, r], r, 1 - slot).start()
        x = xbuf[slot].astype(BF16)
        hg = _dot(x, wgb_ref[...])
        hu = _dot(x, wub_ref[...])
        hid = (hg * _sigmoid(hg) * hu).astype(BF16)
        out_ref[...] = (_dot(hid, wdb_ref[...]) * rw_ref[...]).astype(out_ref.dtype)

        @pl.when(i == nv - 1)
        def _():
            for r in range(blk):
                row_copy(0, r, 1 - slot).wait()

    @pl.when(i >= nv)
    def _():
        out_ref[...] = jnp.zeros_like(out_ref)


def _experts(h, row_tok, row_w, blk_exp, n_valid, w_gate, w_up, w_down):
    _, d = h.shape
    _, _, de = w_gate.shape
    blk = EXPERT_BLOCK
    nb = row_tok.shape[0] // blk
    tok3 = row_tok.reshape(nb, 1, blk)
    live = lambda i, nv: jnp.minimum(i, nv[0] - 1)
    idx_spec = lambda step: pl.BlockSpec((None, 1, blk), lambda i, be, nv: (live(i + step, nv), 0, 0),
                                         memory_space=pltpu.SMEM)
    grid_spec = pltpu.PrefetchScalarGridSpec(
        num_scalar_prefetch=2,
        grid=(nb,),
        in_specs=[
            idx_spec(0), idx_spec(1),
            pl.BlockSpec((blk, 1), lambda i, be, nv: (live(i, nv), 0)),
            pl.BlockSpec(memory_space=pl.ANY),
            pl.BlockSpec((None, d, de), lambda i, be, nv: (be[i], 0, 0)),
            pl.BlockSpec((None, d, de), lambda i, be, nv: (be[i], 0, 0)),
            pl.BlockSpec((None, de, d), lambda i, be, nv: (be[i], 0, 0)),
        ],
        out_specs=pl.BlockSpec((blk, d), lambda i, be, nv: (i, 0)),
        scratch_shapes=[pltpu.VMEM((2, blk, d), F32), pltpu.SemaphoreType.DMA((2,)),
                        pltpu.VMEM((d, de), BF16), pltpu.VMEM((d, de), BF16), pltpu.VMEM((de, d), BF16),
                        pltpu.SMEM((1,), jnp.int32)],
    )
    return pl.pallas_call(
        _expert_kernel,
        grid_spec=grid_spec,
        out_shape=jax.ShapeDtypeStruct((nb * blk, d), BF16),
        compiler_params=_cparams(("arbitrary",)),
        name="experts",
    )(blk_exp, n_valid, tok3, tok3, row_w, h, w_gate, w_up, w_down)


def _final_kernel(h_ref, hb_ref, routed_ref, wg_ref, wu_ref, wd_ref, g_ref, b_ref, out_ref):
    x = hb_ref[...]
    hg = _dot(x, wg_ref[...])
    hu = _dot(x, wu_ref[...])
    shared = _dot((hg * _sigmoid(hg) * hu).astype(BF16), wd_ref[...])
    y = DN_ALPHA * h_ref[...] + routed_ref[...] + shared
    out_ref[...] = _layer_norm(y, g_ref[...], b_ref[...])


def _final(h, hb, routed, wg, wu, wd, g, b, tm):
    t, d = h.shape
    assert t % tm == 0
    row = pl.BlockSpec((tm, d), lambda i: (i, 0))
    const = lambda a: pl.BlockSpec(a.shape, lambda i: (0, 0))
    return pl.pallas_call(
        _final_kernel,
        grid=(t // tm,),
        in_specs=[row, row, row, const(wg), const(wu), const(wd), const(g), const(b)],
        out_specs=row,
        out_shape=jax.ShapeDtypeStruct((t, d), F32),
        compiler_params=_cparams(("parallel",)),
        name="final",
    )(h, hb, routed, wg, wu, wd, g, b)


def _dispatch_plan(idx_t, gate_t, ne):
    t = idx_t.shape[1]
    blk = EXPERT_BLOCK
    n_assign = t * TOP_K
    e_flat = idx_t.reshape(n_assign)
    iota = jnp.arange(n_assign, dtype=jnp.int32)
    experts = jnp.arange(ne, dtype=jnp.int32)
    e_sorted, order, g_sorted = lax.sort((e_flat, iota, gate_t.reshape(n_assign)), num_keys=1)
    counts = jnp.sum((e_flat[:, None] == experts[None, :]).astype(jnp.int32), axis=0)
    padded = (counts + blk - 1) // blk * blk
    pad_end = jnp.cumsum(padded)
    pad_start = pad_end - padded
    raw_start = jnp.cumsum(counts) - counts
    gap = pad_start - raw_start
    dest = iota + jnp.sum(jnp.where(e_sorted[:, None] == experts[None, :], gap[None, :], 0), axis=1)
    nb = n_assign // blk + ne
    blk_start = jnp.arange(nb, dtype=jnp.int32) * blk
    blk_exp = jnp.minimum(jnp.sum((pad_end[None, :] <= blk_start[:, None]).astype(jnp.int32), axis=1), ne - 1)
    n_valid = (pad_end[-1] // blk).astype(jnp.int32).reshape(1)
    off = blk_start - pad_start[blk_exp]
    start = jnp.clip(raw_start[blk_exp] + off, 0, n_assign)
    valid = (off[:, None] + jnp.arange(blk, dtype=jnp.int32)[None, :]) < counts[blk_exp][:, None]
    tail = jnp.zeros((blk,), jnp.int32)
    runs = lambda a: jax.vmap(lambda s: lax.dynamic_slice(a, (s,), (blk,)))(start)
    row_tok = jnp.where(valid, runs(jnp.concatenate([order, tail])) % t, 0)
    row_w = jnp.where(valid, runs(jnp.concatenate([g_sorted, tail.astype(F32)])), 0.0)
    _, pos = lax.sort((order, dest), num_keys=1)
    return row_tok.reshape(nb * blk), row_w.reshape(nb * blk, 1), blk_exp, n_valid, pos.reshape(TOP_K, t)


def _pick_tile(n, cap):
    t = cap
    while n % t:
        t //= 2
    return t


def kernel(x_prompt, x_sample, cache_sb_k, cache_sb_v, state_rwkv_wkv, state_rwkv_shift, meta_tokens, ln0_g, ln0_b, w_in, rw_mu, rw_w0, rw_w2, rw_a0, rw_a2, rw_g2, rw_k_k, rw_k_a, rw_r_k, rw_gn_g, rw_gn_b, sb_norm_g, w_out, ln1_g, ln1_b, w_router, router_bias, w_exp_gate, w_exp_up, w_exp_down, w_sh_gate, w_sh_up, w_sh_down, ln2_g, ln2_b):
    bsz, seq, d = x_prompt.shape
    dbs, dseq, _ = x_sample.shape
    past = cache_sb_k.shape[2]
    n_h = C_RW // HEAD
    ne = w_router.shape[-1]
    rw_cols = 3 * C_RW + D_DECAY + D_AAA + D_GATE
    o_wd, o_ad, o_gd = 3 * C_RW, 3 * C_RW + D_DECAY, 3 * C_RW + D_DECAY + D_AAA

    def pad_low(a):
        z = lambda n: jnp.zeros(a.shape[:-1] + (n,), a.dtype)
        return jnp.concatenate([a[..., :o_wd], a[..., o_wd:o_ad], z(LANES - D_DECAY), a[..., o_ad:o_gd],
                                z(LANES - D_AAA), a[..., o_gd:rw_cols], z(2 * LANES - D_GATE)], -1)

    def unpad_low(rkv_row, low_row):
        return jnp.concatenate([rkv_row, low_row[..., 0:D_DECAY], low_row[..., LANES:LANES + D_AAA],
                                low_row[..., 2 * LANES:2 * LANES + D_GATE]], -1)

    wi = w_in[0]
    w_rw = pad_low(wi[:, :rw_cols])
    w6 = jnp.concatenate([w_rw[:, :3 * C_RW], wi[:, rw_cols:]], 1).astype(BF16)
    wlow = w_rw[:, 3 * C_RW:].astype(BF16)
    pad_rows = lambda a, n: jnp.concatenate([a, jnp.zeros((n - a.shape[0], a.shape[1]), a.dtype)], 0)
    row = lambda a: a.reshape(1, -1)
    prm = {
        "mu": pad_low(rw_mu[0]).reshape(1, -1),
        "w0": row(rw_w0[0]), "a0": row(rw_a0[0]), "k_k": row(rw_k_k[0]), "k_a": row(rw_k_a[0]),
        "r_k": row(rw_r_k[0]), "gn_g": row(rw_gn_g[0]), "gn_b": row(rw_gn_b[0]),
        "w2": pad_rows(rw_w2[0], LANES), "a2": pad_rows(rw_a2[0], LANES), "g2": pad_rows(rw_g2[0], 2 * LANES),
    }
    g0, b0 = row(ln0_g), row(ln0_b)
    g1, b1 = row(ln1_g[0]), row(ln1_b[0])
    g2, b2 = row(ln2_g[0]), row(ln2_b[0])
    sb_gain = row(sb_norm_g[0])
    wo = w_out[0].astype(BF16)
    wo_a, wo_b = wo[:C_RW], wo[C_RW:]
    wr_hi, wr_lo = _split2(w_router[0].T)
    rbias = router_bias[0].astype(F32).reshape(ne, 1)
    wsg, wsu, wsd = w_sh_gate[0].astype(BF16), w_sh_up[0].astype(BF16), w_sh_down[0].astype(BF16)

    xp = x_prompt.reshape(bsz * seq, d)
    xs = x_sample.reshape(dbs * dseq, d)
    xm = meta_tokens.astype(x_prompt.dtype)
    rkv_m, _, k_m, v_m, low_m = _in_proj(xm, g0, b0, w6, wlow, N_META)
    rkv_p, q_p, k_p, v_p, low_p = _in_proj(xp, g0, b0, w6, wlow, _pick_tile(bsz * seq, 512))
    rkv_s, q_s, k_s, v_s, low_s = _in_proj(xs, g0, b0, w6, wlow, _pick_tile(dbs * dseq, 512))

    zero_shift = jnp.zeros((1, 1, 3 * C_RW + LOW_PAD), F32)
    zero_wkv = jnp.zeros((1, n_h, HEAD, HEAD), F32)
    _, wkv_m = _rwkv(rkv_m[None], low_m[None], zero_shift, zero_wkv, prm, N_META)
    shift_m = jnp.concatenate([rkv_m[-1:], low_m[-1:]], -1)[None]
    rw_p, wkv_p = _rwkv(rkv_p.reshape(bsz, seq, -1), low_p.reshape(bsz, seq, -1), shift_m, wkv_m, prm,
                        _pick_tile(seq, 64))
    shift_s0 = pad_low(state_rwkv_shift[0])
    rw_s, wkv_s = _rwkv(rkv_s.reshape(dbs, dseq, -1), low_s.reshape(dbs, dseq, -1), shift_s0,
                        state_rwkv_wkv[0], prm, dseq)

    c3 = lambda a, b_, l_: a.reshape(b_, l_, C_SB)
    sb_p = _attention(c3(q_p, bsz, seq), c3(k_p, bsz, seq), c3(v_p, bsz, seq), k_m[None], v_m[None], sb_gain,
                      _pick_tile(seq, 256), N_META)
    sb_s = _attention(c3(q_s, dbs, dseq), c3(k_s, dbs, dseq), c3(v_s, dbs, dseq),
                      cache_sb_k[0].reshape(dbs, past, C_SB), cache_sb_v[0].reshape(dbs, past, C_SB), sb_gain,
                      dseq, _pick_tile(past, 256))

    x_all = jnp.concatenate([xp, xs], 0)
    rw_all = jnp.concatenate([rw_p.reshape(bsz * seq, C_RW), rw_s.reshape(dbs * dseq, C_RW)], 0)
    sb_all = jnp.concatenate([sb_p.reshape(bsz * seq, C_SB), sb_s.reshape(dbs * dseq, C_SB)], 0)
    t_all = x_all.shape[0]
    tm = _pick_tile(t_all, 256)
    h, hb, idx_t, gate_t = _out_proj(x_all, rw_all, sb_all, wo_a, wo_b, g0, b0, g1, b1, wr_hi, wr_lo, rbias, tm)

    row_tok, row_w, blk_exp, n_valid, pos = _dispatch_plan(idx_t, gate_t, ne)
    expert_out = _experts(h, row_tok, row_w, blk_exp, n_valid, w_exp_gate[0], w_exp_up[0], w_exp_down[0])
    routed = jnp.sum(jnp.take(expert_out, pos, axis=0).astype(F32), axis=0)
    y = _final(h, hb, routed, wsg, wsu, wsd, g2, b2, tm)

    y_prompt = y[:bsz * seq].reshape(bsz, seq, d)
    y_sample = y[bsz * seq:].reshape(dbs, dseq, d)
    heads = lambda a, b_, l_: a.reshape(b_, l_, C_SB // HEAD, HEAD)

    def with_meta(m, p):
        m4 = jnp.broadcast_to(heads(m, 1, N_META), (bsz, N_META, C_SB // HEAD, HEAD))
        return jnp.concatenate([m4, heads(p, bsz, seq)], 1)[None]

    k_prompt = with_meta(k_m, k_p)
    v_prompt = with_meta(v_m, v_p)
    last = lambda a, b_, l_: a.reshape(b_, l_, -1)[:, -1:]
    shift_prompt = unpad_low(last(rkv_p, bsz, seq), last(low_p, bsz, seq))[None]
    shift_sample = unpad_low(last(rkv_s, dbs, dseq), last(low_s, dbs, dseq))[None]
    return (y_prompt, y_sample, k_prompt, v_prompt, wkv_p[None], shift_prompt,
            heads(k_s, dbs, dseq)[None], heads(v_s, dbs, dseq)[None], wkv_s[None], shift_sample)
```

```python
import functools

import jax
import jax.numpy as jnp
from jax import lax
from jax.experimental import pallas as pl
from jax.experimental.pallas import tpu as pltpu

F32 = jnp.float32
BF16 = jnp.bfloat16
HIGHEST = lax.Precision.HIGHEST

N_META = 16
HEAD = 64
C_RW = 1024
C_SB = 1024
D_DECAY = 64
D_AAA = 64
D_GATE = 160
LOW_PAD = 512
TOP_K = 8
N_GROUPS = 8
TOPK_GROUPS = 4
ROUTED_SCALE = 2.5
LN_EPS = 1e-5
GN_EPS = 64e-5
RMS_EPS = 1e-6
DEPTH = 1
DN_ALPHA = (2 * DEPTH) ** 0.25
LOG2E = 1.4426950408889634

LANES = 128
VMEM_LIMIT = 56 * 1024 * 1024
RW_HEADS_PER_STEP = 16
EXPERT_BLOCK = 256


def _cparams(sem):
    return pltpu.CompilerParams(dimension_semantics=sem, vmem_limit_bytes=VMEM_LIMIT)


def _layer_norm(x, g, b):
    mu = jnp.mean(x, -1, keepdims=True)
    xc = x - mu
    var = jnp.mean(xc * xc, -1, keepdims=True)
    return xc * lax.rsqrt(var + LN_EPS) * g + b


def _sigmoid(x):
    return 1.0 / (1.0 + jnp.exp(-x))


def _softplus(x):
    return jnp.maximum(x, 0.0) + jnp.log(1.0 + jnp.exp(-jnp.abs(x)))


def _softplus2(x2):
    neg_abs = lax.bitcast_convert_type(lax.bitcast_convert_type(x2, jnp.uint32) | jnp.uint32(0x80000000), F32)
    return jnp.maximum(x2, 0.0) + jnp.log(1.0 + jnp.exp2(neg_abs)) * LOG2E


def _dot(a, b, precision=None):
    return jnp.dot(a, b, preferred_element_type=F32, precision=precision)


def _dot_nt(a, b, precision=None):
    return lax.dot_general(a, b, (((1,), (1,)), ((), ())), preferred_element_type=F32, precision=precision)


def _dot_tn(a, b, precision=None):
    return lax.dot_general(a, b, (((0,), (0,)), ((), ())), preferred_element_type=F32, precision=precision)


def _split2(x):
    hi = x.astype(BF16)
    return hi, (x - hi.astype(F32)).astype(BF16)


def _split3(x):
    hi = x.astype(BF16)
    r1 = x - hi.astype(F32)
    mid = r1.astype(BF16)
    return hi, mid, (r1 - mid.astype(F32)).astype(BF16)


def _dot_x3(a, b, dot=_dot):
    ah, al = _split2(a)
    bh, bl = _split2(b)
    return dot(ah, bh) + dot(al, bh) + dot(ah, bl)


def _dot_bf(a, b, dot=_dot):
    return dot(a.astype(BF16), b.astype(BF16))


RW_PREC = {"tri": "bf", "dbl": "bf", "wu": "bf", "ro": "bf", "pq": "bf", "state": "x3"}


def _mm(site, a, b, dot=_dot):
    mode = RW_PREC[site]
    if mode == "bf":
        return _dot_bf(a, b, dot)
    if mode == "x3":
        return _dot_x3(a, b, dot)
    return dot(a, b, HIGHEST)


def _in_proj_kernel(x_ref, g_ref, b_ref, w_ref, wlow_ref, rkv_ref, q_ref, k_ref, v_ref, low_ref, xn_ref):
    j = pl.program_id(1)

    @pl.when(j == 0)
    def _():
        xn = _layer_norm(x_ref[...], g_ref[...], b_ref[...]).astype(BF16)
        xn_ref[...] = xn
        low_ref[...] = _dot(xn, wlow_ref[...])

    y = _dot(xn_ref[...], w_ref[...])

    @pl.when(j < 3)
    def _():
        rkv_ref[...] = y

    @pl.when(j == 3)
    def _():
        q_ref[...] = y

    @pl.when(j == 4)
    def _():
        k_ref[...] = y

    @pl.when(j == 5)
    def _():
        v_ref[...] = y


def _in_proj(x, ln_g, ln_b, w6, wlow, tm):
    t, d = x.shape
    assert t % tm == 0
    c = C_RW
    grid = (t // tm, 6)
    col = lambda i, j: (i, 0)
    return pl.pallas_call(
        _in_proj_kernel,
        grid=grid,
        in_specs=[
            pl.BlockSpec((tm, d), col),
            pl.BlockSpec((1, d), lambda i, j: (0, 0)),
            pl.BlockSpec((1, d), lambda i, j: (0, 0)),
            pl.BlockSpec((d, c), lambda i, j: (0, j)),
            pl.BlockSpec((d, LOW_PAD), lambda i, j: (0, 0)),
        ],
        out_specs=[
            pl.BlockSpec((tm, c), lambda i, j: (i, jnp.minimum(j, 2))),
            pl.BlockSpec((tm, c), col),
            pl.BlockSpec((tm, c), col),
            pl.BlockSpec((tm, c), col),
            pl.BlockSpec((tm, LOW_PAD), col),
        ],
        out_shape=[
            jax.ShapeDtypeStruct((t, 3 * c), F32),
            jax.ShapeDtypeStruct((t, c), F32),
            jax.ShapeDtypeStruct((t, c), F32),
            jax.ShapeDtypeStruct((t, c), F32),
            jax.ShapeDtypeStruct((t, LOW_PAD), F32),
        ],
        scratch_shapes=[pltpu.VMEM((tm, d), BF16)],
        compiler_params=_cparams(("parallel", "arbitrary")),
        name="in_proj",
    )(x, ln_g, ln_b, w6, wlow)


def _rwkv_kernel(r_ref, k_ref, v_ref, low_ref, sr_ref, sk_ref, sv_ref, slow_ref,
                 mur_ref, muk_ref, muv_ref, mulow_ref,
                 w0_ref, a0_ref, kk_ref, ka_ref, rk_ref, gng_ref, gnb_ref,
                 w2_ref, a2_ref, g2_ref, bd_ref, tri_ref, wkv0_ref,
                 out_ref, wkv_ref,
                 s_ref, pr_ref, pk_ref, pv_ref, plow_ref, *, tc, hps, n_dbl):
    c = pl.program_id(2)
    nc = pl.num_programs(2)

    @pl.when(c == 0)
    def _():
        s_ref[...] = wkv0_ref[...]
        pr_ref[...] = sr_ref[...]
        pk_ref[...] = sk_ref[...]
        pv_ref[...] = sv_ref[...]
        plow_ref[...] = slow_ref[...]

    def token_shift(cur_ref, prev_ref, mu_ref):
        cur = cur_ref[...]
        row = lax.broadcasted_iota(jnp.int32, cur.shape, 0)
        prev = jnp.where(row == 0, prev_ref[...], pltpu.roll(cur, 1, 0))
        prev_ref[...] = cur[tc - 1:tc, :]
        return cur + (prev - cur) * mu_ref[...]

    r = token_shift(r_ref, pr_ref, mur_ref)
    k = token_shift(k_ref, pk_ref, muk_ref)
    v = token_shift(v_ref, pv_ref, muv_ref)
    xl = token_shift(low_ref, plow_ref, mulow_ref)
    wd = xl[:, 0:LANES]
    ad = xl[:, LANES:2 * LANES]
    gd = xl[:, 2 * LANES:LOW_PAD]

    w_log = -_softplus(-(w0_ref[...] + _dot_x3(jnp.tanh(wd), w2_ref[...]))) - 0.5
    lw = -jnp.exp(w_log)
    a = _sigmoid(a0_ref[...] + _dot_x3(ad, a2_ref[...]))
    g = _dot_x3(_sigmoid(gd), g2_ref[...])

    bd = bd_ref[...]
    nb_lane = bd.shape[0]

    def head_sum(x):
        parts = []
        for j in range(x.shape[1] // nb_lane):
            hi, lo = _split2(x[:, j * nb_lane:(j + 1) * nb_lane])
            parts.append(_dot(hi, bd) + _dot(lo, bd))
        return parts[0] if len(parts) == 1 else jnp.concatenate(parts, axis=1)

    kk = k * kk_ref[...]
    kk = kk / jnp.maximum(jnp.sqrt(head_sum(kk * kk)), 1e-12)
    k2 = k * (1.0 + (a - 1.0) * ka_ref[...])
    bonus = head_sum(r * k2 * rk_ref[...]) * v

    ti = lax.broadcasted_iota(jnp.int32, (tc, tc), 0)
    si = lax.broadcasted_iota(jnp.int32, (tc, tc), 1)
    incl = si <= ti
    strict = si < ti
    tri = tri_ref[...]
    l1, l2, l3 = _split3(lw)
    cs = _dot(tri, l1) + _dot(tri, l2) + _dot(tri, l3)
    c_last = cs[tc - 1:tc, :]
    e_pos = jnp.exp(cs)
    e_neg = jnp.exp(-cs)
    e_end = jnp.exp(c_last - cs)
    b = kk * a
    r_t = r * e_pos
    al_t = -kk * jnp.exp(cs - lw)
    be_t = b * e_neg
    k_t = k2 * e_neg
    be_h = b * e_end
    k_h = k2 * e_end
    g_last = jnp.exp(c_last)

    eye_t = (si == ti).astype(F32)
    di = lax.broadcasted_iota(jnp.int32, (HEAD, HEAD), 0)
    dj = lax.broadcasted_iota(jnp.int32, (HEAD, HEAD), 1)
    eye_h = di == dj

    hs = range(hps)
    cut = lambda x: [x[:, h * HEAD:(h + 1) * HEAD] for h in hs]
    al_h, r_h, v_h, bet, kt, beh, kh, gl = (cut(x) for x in (al_t, r_t, v, be_t, k_t, be_h, k_h, g_last))
    a_mat = [jnp.where(strict, _mm("tri", al_h[h], bet[h], _dot_nt), 0.0) for h in hs]
    b_mat = [jnp.where(strict, _mm("tri", al_h[h], kt[h], _dot_nt), 0.0) for h in hs]
    ar_mat = [jnp.where(incl, _mm("tri", r_h[h], bet[h], _dot_nt), 0.0) for h in hs]
    br_mat = [jnp.where(incl, _mm("tri", r_h[h], kt[h], _dot_nt), 0.0) for h in hs]
    bv = [_mm("wu", b_mat[h], v_h[h]) for h in hs]
    m = [eye_t + a_mat[h] for h in hs]
    pw = a_mat
    for _ in range(n_dbl):
        pw = [_mm("dbl", pw[h], pw[h]) for h in hs]
        m = [m[h] + _mm("dbl", m[h], pw[h]) for h in hs]
    w_m = [_mm("wu", m[h], al_h[h]) for h in hs]
    u_loc = [_mm("wu", m[h], bv[h]) for h in hs]
    r_hat = [r_h[h] + _mm("ro", ar_mat[h], w_m[h]) for h in hs]
    o_loc = [_mm("ro", ar_mat[h], u_loc[h]) + _mm("ro", br_mat[h], v_h[h]) for h in hs]
    p_mat = [_mm("pq", w_m[h], beh[h], _dot_tn) + jnp.where(eye_h, gl[h], 0.0) for h in hs]
    q_mat = [_mm("pq", u_loc[h], beh[h], _dot_tn) + _mm("pq", v_h[h], kh[h], _dot_tn) for h in hs]
    s0 = [s_ref[h] for h in hs]
    o = [o_loc[h] + _mm("state", r_hat[h], s0[h], _dot_nt) for h in hs]
    for h in hs:
        s_ref[h] = _mm("state", s0[h], p_mat[h]) + q_mat[h]
    outs = []
    for h in hs:
        mu = jnp.mean(o[h], -1, keepdims=True)
        oc = o[h] - mu
        var = jnp.mean(oc * oc, -1, keepdims=True)
        outs.append(oc * lax.rsqrt(var + GN_EPS))
    on = jnp.concatenate(outs, axis=1)
    out_ref[...] = ((on * gng_ref[...] + gnb_ref[...] + bonus) * g).astype(out_ref.dtype)

    @pl.when(c == nc - 1)
    def _():
        wkv_ref[...] = s_ref[...]


def _rwkv(rkv, low, shift0, wkv0, prm, tc):
    bsz, length, _ = rkv.shape
    b0 = shift0.shape[0]
    assert length % tc == 0 and b0 in (1, bsz)
    hps = RW_HEADS_PER_STEP
    w = hps * HEAD
    n_h = C_RW // HEAD
    ng = n_h // hps
    nblk = C_RW // w
    n_dbl = max((tc - 1).bit_length() - 1, 0)
    sb = (lambda b: b) if b0 == bsz else (lambda b: 0)
    seq = lambda off: pl.BlockSpec((None, tc, w), lambda b, g, c: (b, c, off * nblk + g))
    row0 = lambda off: pl.BlockSpec((None, 1, w), lambda b, g, c: (sb(b), 0, off * nblk + g))
    vec = lambda off: pl.BlockSpec((1, w), lambda b, g, c: (0, off * nblk + g))
    low_blk = (3 * C_RW) // LOW_PAD
    nbd = 2 * LANES
    bd = (jnp.arange(nbd)[:, None] // HEAD == jnp.arange(nbd)[None, :] // HEAD).astype(BF16)
    tri = (jnp.arange(tc)[None, :] <= jnp.arange(tc)[:, None]).astype(BF16)
    kernel = functools.partial(_rwkv_kernel, tc=tc, hps=hps, n_dbl=n_dbl)
    return pl.pallas_call(
        kernel,
        grid=(bsz, ng, length // tc),
        in_specs=[
            seq(0), seq(1), seq(2),
            pl.BlockSpec((None, tc, LOW_PAD), lambda b, g, c: (b, c, 0)),
            row0(0), row0(1), row0(2),
            pl.BlockSpec((None, 1, LOW_PAD), lambda b, g, c: (sb(b), 0, low_blk)),
            vec(0), vec(1), vec(2),
            pl.BlockSpec((1, LOW_PAD), lambda b, g, c: (0, low_blk)),
            vec(0), vec(0), vec(0), vec(0), vec(0), vec(0), vec(0),
            pl.BlockSpec((LANES, w), lambda b, g, c: (0, g)),
            pl.BlockSpec((LANES, w), lambda b, g, c: (0, g)),
            pl.BlockSpec((2 * LANES, w), lambda b, g, c: (0, g)),
            pl.BlockSpec((nbd, nbd), lambda b, g, c: (0, 0)),
            pl.BlockSpec((tc, tc), lambda b, g, c: (0, 0)),
            pl.BlockSpec((None, hps, HEAD, HEAD), lambda b, g, c: (sb(b), g, 0, 0)),
        ],
        out_specs=[
            pl.BlockSpec((None, tc, w), lambda b, g, c: (b, c, g)),
            pl.BlockSpec((None, hps, HEAD, HEAD), lambda b, g, c: (b, g, 0, 0)),
        ],
        out_shape=[
            jax.ShapeDtypeStruct((bsz, length, C_RW), BF16),
            jax.ShapeDtypeStruct((bsz, n_h, HEAD, HEAD), F32),
        ],
        scratch_shapes=[
            pltpu.VMEM((hps, HEAD, HEAD), F32),
            pltpu.VMEM((1, w), F32), pltpu.VMEM((1, w), F32), pltpu.VMEM((1, w), F32),
            pltpu.VMEM((1, LOW_PAD), F32),
        ],
        compiler_params=_cparams(("parallel", "parallel", "arbitrary")),
        name="rwkv",
    )(rkv, rkv, rkv, low, shift0, shift0, shift0, shift0,
      prm["mu"], prm["mu"], prm["mu"], prm["mu"],
      prm["w0"], prm["a0"], prm["k_k"], prm["k_a"], prm["r_k"], prm["gn_g"], prm["gn_b"],
      prm["w2"], prm["a2"], prm["g2"], bd, tri, wkv0)


def _attn_kernel(q_ref, k_ref, v_ref, kp_ref, vp_ref, tri_ref, trip_ref, g_ref, out_ref, *scratch,
                 tq, pb, n_pre, pipe_main, pipe_pre):
    i = pl.program_id(2)
    q0 = pl.multiple_of(i * tq, tq)
    lane = lax.broadcasted_iota(jnp.int32, (tq, LANES), 1)
    first = lane < HEAD
    q2 = q_ref[pl.ds(q0, tq), :] * (HEAD ** -0.5 * LOG2E)
    q_heads = (jnp.where(first, q2, 0.0).astype(BF16), jnp.where(first, 0.0, q2).astype(BF16))
    tri = tri_ref[...]
    trip = trip_ref[...]
    row = lax.broadcasted_iota(jnp.int32, (tq, tq), 0)
    colm = lax.broadcasted_iota(jnp.int32, (tq, tq), 1)
    causal = colm < row

    def suffix_sum(sp, tri_m):
        return _dot(sp.astype(BF16), tri_m)

    def block(qh, kb, vb, tri_m, carry, acc, mask):
        z = _dot_nt(qh, kb)
        sp = _softplus2(z)
        if mask is not None:
            sp = jnp.where(mask, sp, 0.0)
        suffix = suffix_sum(sp, tri_m)
        p = jnp.exp2((z + carry) - suffix)
        if mask is not None:
            p = jnp.where(mask, p, 0.0)
        acc = acc + _dot(p.astype(BF16), vb)
        return carry - suffix[:, 0:1], acc

    def piped(kr, vr, bs, tri_m, count, block_of, bufs, state, first_scored):
        z_s, d_s, tot_s = bufs
        start = 1 if first_scored else 0
        if not first_scored:
            z_s[1] = jnp.zeros_like(z_s[1])
        d_s[...] = jnp.zeros_like(d_s)
        tot_s[...] = jnp.zeros_like(tot_s)
        last = jnp.maximum(count - 1, 0)
        rows = lambda m: pl.ds(pl.multiple_of(block_of(jnp.clip(m, 0, last)) * bs, bs), bs)

        def stages(n, par, st):
            kb = kr[rows(n), :].astype(BF16)
            for h in range(2):
                z_s[par, h] = _dot_nt(q_heads[h], kb)
            live = jnp.logical_and(n >= 2, n - 2 < count)
            vb = vr[rows(n - 2), :].astype(BF16)
            for h in range(2):
                p = jnp.exp2(d_s[h] + jnp.where(live, st[2 * h], -1e30))
                st[2 * h + 1] = st[2 * h + 1] + _dot(p.astype(BF16), vb)
                st[2 * h] = st[2 * h] - jnp.where(live, tot_s[h], 0.0)
            for h in range(2):
                z = z_s[1 - par, h]
                suffix = suffix_sum(_softplus2(z), tri_m)
                d_s[h] = z - suffix
                tot_s[h] = suffix[:, 0:1]
            return st

        def body(j, st):
            st = list(st)
            for k in range(2):
                st = stages(start + 2 * j + k, (start + k) % 2, st)
            return tuple(st)

        return lax.fori_loop(0, (count + 3 - start) // 2, body, tuple(state))

    state = [jnp.zeros((tq, 1), F32), jnp.zeros((tq, LANES), F32)] * 2
    kd = k_ref[pl.ds(q0, tq), :].astype(BF16)
    if pipe_main:
        for h in range(2):
            scratch[0][0, h] = jnp.where(causal, _dot_nt(q_heads[h], kd), -1e30)
        state = piped(k_ref, v_ref, tq, tri, i + 1, lambda m: i - m, scratch[0:3], state, True)
    else:
        vd = v_ref[pl.ds(q0, tq), :].astype(BF16)
        state[0:2] = block(q_heads[0], kd, vd, tri, state[0], state[1], causal)
        state[2:4] = block(q_heads[1], kd, vd, tri, state[2], state[3], causal)

    if pipe_pre:
        state = piped(kp_ref, vp_ref, pb, trip, n_pre, lambda m: n_pre - 1 - m, scratch[-3:], state, False)
    else:
        for t in range(n_pre):
            rows = pl.ds((n_pre - 1 - t) * pb, pb)
            kb = kp_ref[rows, :].astype(BF16)
            vb = vp_ref[rows, :].astype(BF16)
            state = list(state)
            state[0:2] = block(q_heads[0], kb, vb, trip, state[0], state[1], None)
            state[2:4] = block(q_heads[1], kb, vb, trip, state[2], state[3], None)

    o = jnp.where(first, state[1], state[3])
    sq = o * o
    s_a = jnp.sum(jnp.where(first, sq, 0.0), -1, keepdims=True)
    s_b = jnp.sum(sq, -1, keepdims=True) - s_a
    inv = jnp.where(first, lax.rsqrt(s_a / HEAD + RMS_EPS), lax.rsqrt(s_b / HEAD + RMS_EPS))
    out_ref[...] = (o * inv * g_ref[...]).astype(out_ref.dtype)


def _attention(q, k, v, kp, vp, gain, tq, pb):
    bsz, length, c = q.shape
    b0, plen, _ = kp.shape
    assert length % tq == 0 and plen % pb == 0 and b0 in (1, bsz)
    sb = (lambda b: b) if b0 == bsz else (lambda b: 0)
    n_pre = plen // pb
    n_main = length // tq
    pipe_main = n_main > 1
    pipe_pre = n_pre >= 3
    tri1 = lambda n: (jnp.arange(n)[:, None] >= jnp.arange(n)[None, :]).astype(BF16)
    stage_bufs = lambda bs: [pltpu.VMEM((2, 2, tq, bs), F32), pltpu.VMEM((2, tq, bs), F32),
                             pltpu.VMEM((2, tq, 1), F32)]
    full = pl.BlockSpec((None, length, LANES), lambda b, h, i: (b, 0, h))
    pre = pl.BlockSpec((None, plen, LANES), lambda b, h, i: (sb(b), 0, h))
    kernel = functools.partial(_attn_kernel, tq=tq, pb=pb, n_pre=n_pre, pipe_main=pipe_main, pipe_pre=pipe_pre)
    return pl.pallas_call(
        kernel,
        grid=(bsz, c // LANES, n_main),
        in_specs=[
            full, full, full, pre, pre,
            pl.BlockSpec((tq, tq), lambda b, h, i: (0, 0)),
            pl.BlockSpec((pb, pb), lambda b, h, i: (0, 0)),
            pl.BlockSpec((1, LANES), lambda b, h, i: (0, h)),
        ],
        out_specs=pl.BlockSpec((None, tq, LANES), lambda b, h, i: (b, i, h)),
        out_shape=jax.ShapeDtypeStruct((bsz, length, c), BF16),
        scratch_shapes=(stage_bufs(tq) if pipe_main else []) + (stage_bufs(pb) if pipe_pre else []),
        compiler_params=_cparams(("parallel", "parallel", "arbitrary")),
        name="attn",
    )(q, k, v, kp, vp, tri1(tq), tri1(pb), gain)


def _route_columns(logits_t, bias):
    ne, n = logits_t.shape
    gsz = ne // N_GROUPS
    neg = -jnp.inf
    scores = _sigmoid(logits_t)
    sel = scores + bias
    ig = lax.broadcasted_iota(jnp.int32, (gsz, n), 0)
    gs = []
    for g in range(N_GROUPS):
        xg = sel[g * gsz:(g + 1) * gsz, :]
        m1 = jnp.max(xg, axis=0, keepdims=True)
        first = jnp.min(jnp.where(xg == m1, ig, gsz), axis=0, keepdims=True)
        m2 = jnp.max(jnp.where(ig == first, neg, xg), axis=0, keepdims=True)
        gs.append(m1 + m2)
    kept = []
    for g in range(N_GROUPS):
        rank = jnp.zeros((1, n), jnp.int32)
        for g2 in range(N_GROUPS):
            if g2 == g:
                continue
            ahead = (gs[g2] >= gs[g]) if g2 < g else (gs[g2] > gs[g])
            rank = rank + ahead.astype(jnp.int32)
        kept.append(jnp.where(rank < TOPK_GROUPS, sel[g * gsz:(g + 1) * gsz, :], neg))
    cand = jnp.concatenate(kept, axis=0)
    ie = lax.broadcasted_iota(jnp.int32, (ne, n), 0)
    ids, gates = [], []
    for _ in range(TOP_K):
        m = jnp.max(cand, axis=0, keepdims=True)
        idx = jnp.min(jnp.where(cand == m, ie, ne), axis=0, keepdims=True)
        hit = ie == idx
        gates.append(jnp.sum(jnp.where(hit, scores, 0.0), axis=0, keepdims=True))
        ids.append(idx)
        cand = jnp.where(hit, neg, cand)
    gate = jnp.concatenate(gates, axis=0)
    gate = gate / jnp.sum(gate, axis=0, keepdims=True) * ROUTED_SCALE
    return jnp.concatenate(ids, axis=0), gate


def _out_proj_kernel(x_ref, rw_ref, sb_ref, wa_ref, wb_ref, g0_ref, b0_ref, g1_ref, b1_ref,
                     wrh_ref, wrl_ref, rb_ref, h_ref, hb_ref, idx_ref, gate_ref):
    xn = _layer_norm(x_ref[...], g0_ref[...], b0_ref[...])
    mix = _dot(rw_ref[...], wa_ref[...]) + _dot(sb_ref[...], wb_ref[...])
    h = _layer_norm(DN_ALPHA * xn + mix, g1_ref[...], b1_ref[...])
    h_ref[...] = h
    hi, lo = _split2(h)
    hb_ref[...] = hi
    wrh = wrh_ref[...]
    logits_t = _dot_nt(wrh, hi) + _dot_nt(wrh, lo) + _dot_nt(wrl_ref[...], hi)
    idx, gate = _route_columns(logits_t, rb_ref[...])
    idx_ref[...] = idx
    gate_ref[...] = gate


def _out_proj(x, rw, sbo, wa, wb, g0, b0, g1, b1, wrh, wrl, rbias, tm):
    t, d = x.shape
    assert t % tm == 0
    row = lambda w: pl.BlockSpec((tm, w), lambda i: (i, 0))
    col = pl.BlockSpec((TOP_K, tm), lambda i: (0, i))
    const = lambda a: pl.BlockSpec(a.shape, lambda i: (0, 0))
    return pl.pallas_call(
        _out_proj_kernel,
        grid=(t // tm,),
        in_specs=[row(d), row(C_RW), row(C_SB), const(wa), const(wb), const(g0), const(b0), const(g1), const(b1),
                  const(wrh), const(wrl), const(rbias)],
        out_specs=[row(d), row(d), col, col],
        out_shape=[jax.ShapeDtypeStruct((t, d), F32), jax.ShapeDtypeStruct((t, d), BF16),
                   jax.ShapeDtypeStruct((TOP_K, t), jnp.int32), jax.ShapeDtypeStruct((TOP_K, t), F32)],
        compiler_params=_cparams(("parallel",)),
        name="out_proj",
    )(x, rw, sbo, wa, wb, g0, b0, g1, b1, wrh, wrl, rbias)


def _expert_kernel(be_ref, nv_ref, tok_ref, nxt_ref, rw_ref, h_hbm, wg_ref, wu_ref, wd_ref, out_ref,
                   xbuf, sems, wgb_ref, wub_ref, wdb_ref, prev_ref):
    i = pl.program_id(0)
    nv = nv_ref[0]
    e = be_ref[i]
    slot = lax.rem(i, 2)
    blk = xbuf.shape[1]

    def row_copy(src_row, r, s):
        return pltpu.make_async_copy(h_hbm.at[pl.ds(src_row, 1), :], xbuf.at[s, pl.ds(r, 1), :], sems.at[s])

    @pl.when(i == 0)
    def _():
        prev_ref[0] = -1
        for r in range(blk):
            row_copy(tok_ref[0, r], r, 0).start()

    @pl.when(i < nv)
    def _():
        @pl.when(e != prev_ref[0])
        def _():
            wgb_ref[...] = wg_ref[...].astype(BF16)
            wub_ref[...] = wu_ref[...].astype(BF16)
            wdb_ref[...] = wd_ref[...].astype(BF16)
            prev_ref[0] = e

        for r in range(blk):
            row_copy(0, r, slot).wait()
        for r in range(blk):
            row_copy(nxt_ref[0, r], r, 1 - slot).start()
        x = xbuf[slot].astype(BF16)
        hg = _dot(x, wgb_ref[...])
        hu = _dot(x, wub_ref[...])
        hid = (hg * _sigmoid(hg) * hu).astype(BF16)
        out_ref[...] = (_dot(hid, wdb_ref[...]) * rw_ref[...]).astype(out_ref.dtype)

        @pl.when(i == nv - 1)
        def _():
            for r in range(blk):
                row_copy(0, r, 1 - slot).wait()

    @pl.when(i >= nv)
    def _():
        out_ref[...] = jnp.zeros_like(out_ref)


def _experts(h, row_tok, row_w, blk_exp, n_valid, w_gate, w_up, w_down):
    _, d = h.shape
    _, _, de = w_gate.shape
    blk = EXPERT_BLOCK
    nb = row_tok.shape[0] // blk
    tok3 = row_tok.reshape(nb, 1, blk)
    live = lambda i, nv: jnp.minimum(i, nv[0] - 1)
    idx_spec = lambda step: pl.BlockSpec((None, 1, blk), lambda i, be, nv: (live(i + step, nv), 0, 0),
                                         memory_space=pltpu.SMEM)
    grid_spec = pltpu.PrefetchScalarGridSpec(
        num_scalar_prefetch=2,
        grid=(nb,),
        in_specs=[
            idx_spec(0), idx_spec(1),
            pl.BlockSpec((blk, 1), lambda i, be, nv: (live(i, nv), 0)),
            pl.BlockSpec(memory_space=pl.ANY),
            pl.BlockSpec((None, d, de), lambda i, be, nv: (be[i], 0, 0)),
            pl.BlockSpec((None, d, de), lambda i, be, nv: (be[i], 0, 0)),
            pl.BlockSpec((None, de, d), lambda i, be, nv: (be[i], 0, 0)),
        ],
        out_specs=pl.BlockSpec((blk, d), lambda i, be, nv: (i, 0)),
        scratch_shapes=[pltpu.VMEM((2, blk, d), F32), pltpu.SemaphoreType.DMA((2,)),
                        pltpu.VMEM((d, de), BF16), pltpu.VMEM((d, de), BF16), pltpu.VMEM((de, d), BF16),
                        pltpu.SMEM((1,), jnp.int32)],
    )
    return pl.pallas_call(
        _expert_kernel,
        grid_spec=grid_spec,
        out_shape=jax.ShapeDtypeStruct((nb * blk, d), BF16),
        compiler_params=_cparams(("arbitrary",)),
        name="experts",
    )(blk_exp, n_valid, tok3, tok3, row_w, h, w_gate, w_up, w_down)


def _final_kernel(h_ref, hb_ref, routed_ref, wg_ref, wu_ref, wd_ref, g_ref, b_ref, out_ref):
    x = hb_ref[...]
    hg = _dot(x, wg_ref[...])
    hu = _dot(x, wu_ref[...])
    shared = _dot((hg * _sigmoid(hg) * hu).astype(BF16), wd_ref[...])
    y = DN_ALPHA * h_ref[...] + routed_ref[...] + shared
    out_ref[...] = _layer_norm(y, g_ref[...], b_ref[...])


def _final(h, hb, routed, wg, wu, wd, g, b, tm):
    t, d = h.shape
    assert t % tm == 0
    row = pl.BlockSpec((tm, d), lambda i: (i, 0))
    const = lambda a: pl.BlockSpec(a.shape, lambda i: (0, 0))
    return pl.pallas_call(
        _final_kernel,
        grid=(t // tm,),
        in_specs=[row, row, row, const(wg), const(wu), const(wd), const(g), const(b)],
        out_specs=row,
        out_shape=jax.ShapeDtypeStruct((t, d), F32),
        compiler_params=_cparams(("parallel",)),
        name="final",
    )(h, hb, routed, wg, wu, wd, g, b)


def _dispatch_plan(idx_t, gate_t, ne):
    t = idx_t.shape[1]
    blk = EXPERT_BLOCK
    n_assign = t * TOP_K
    nb = n_assign // blk + ne
    n_rows = nb * blk
    n_fill = n_rows - n_assign
    e_flat = idx_t.reshape(n_assign)
    iota = jnp.arange(n_assign, dtype=jnp.int32)
    experts = jnp.arange(ne, dtype=jnp.int32)
    counts = jnp.sum((e_flat[:, None] == experts[None, :]).astype(jnp.int32), axis=0)
    padded = (counts + blk - 1) // blk * blk
    n_valid = (jnp.sum(padded) // blk).astype(jnp.int32).reshape(1)
    fill_end = jnp.cumsum(padded - counts)
    fill_e = jnp.sum((fill_end[None, :] <= jnp.arange(n_fill, dtype=jnp.int32)[:, None]).astype(jnp.int32), axis=1)
    zeros = jnp.zeros((n_fill,), jnp.int32)
    keys = jnp.concatenate([2 * e_flat, 2 * fill_e + 1])
    toks = jnp.concatenate([iota % t, zeros])
    wts = jnp.concatenate([gate_t.reshape(n_assign), zeros.astype(F32)])
    aid = jnp.concatenate([iota, zeros + n_assign])
    keys, row_tok, row_w, row_aid = lax.sort((keys, toks, wts, aid), num_keys=1)
    blk_exp = jnp.minimum(keys[::blk] // 2, ne - 1)
    _, pos = lax.sort((row_aid, jnp.arange(n_rows, dtype=jnp.int32)), num_keys=1)
    return row_tok, row_w.reshape(n_rows, 1), blk_exp, n_valid, pos[:n_assign].reshape(TOP_K, t)


def _pick_tile(n, cap):
    t = cap
    while n % t:
        t //= 2
    return t


def kernel(x_prompt, x_sample, cache_sb_k, cache_sb_v, state_rwkv_wkv, state_rwkv_shift, meta_tokens, ln0_g, ln0_b, w_in, rw_mu, rw_w0, rw_w2, rw_a0, rw_a2, rw_g2, rw_k_k, rw_k_a, rw_r_k, rw_gn_g, rw_gn_b, sb_norm_g, w_out, ln1_g, ln1_b, w_router, router_bias, w_exp_gate, w_exp_up, w_exp_down, w_sh_gate, w_sh_up, w_sh_down, ln2_g, ln2_b):
    bsz, seq, d = x_prompt.shape
    dbs, dseq, _ = x_sample.shape
    past = cache_sb_k.shape[2]
    n_h = C_RW // HEAD
    ne = w_router.shape[-1]
    rw_cols = 3 * C_RW + D_DECAY + D_AAA + D_GATE
    o_wd, o_ad, o_gd = 3 * C_RW, 3 * C_RW + D_DECAY, 3 * C_RW + D_DECAY + D_AAA

    def pad_low(a):
        z = lambda n: jnp.zeros(a.shape[:-1] + (n,), a.dtype)
        return jnp.concatenate([a[..., :o_wd], a[..., o_wd:o_ad], z(LANES - D_DECAY), a[..., o_ad:o_gd],
                                z(LANES - D_AAA), a[..., o_gd:rw_cols], z(2 * LANES - D_GATE)], -1)

    def unpad_low(rkv_row, low_row):
        return jnp.concatenate([rkv_row, low_row[..., 0:D_DECAY], low_row[..., LANES:LANES + D_AAA],
                                low_row[..., 2 * LANES:2 * LANES + D_GATE]], -1)

    wi = w_in[0]
    w_rw = pad_low(wi[:, :rw_cols])
    w6 = jnp.concatenate([w_rw[:, :3 * C_RW], wi[:, rw_cols:]], 1).astype(BF16)
    wlow = w_rw[:, 3 * C_RW:].astype(BF16)
    pad_rows = lambda a, n: jnp.concatenate([a, jnp.zeros((n - a.shape[0], a.shape[1]), a.dtype)], 0)
    row = lambda a: a.reshape(1, -1)
    prm = {
        "mu": pad_low(rw_mu[0]).reshape(1, -1),
        "w0": row(rw_w0[0]), "a0": row(rw_a0[0]), "k_k": row(rw_k_k[0]), "k_a": row(rw_k_a[0]),
        "r_k": row(rw_r_k[0]), "gn_g": row(rw_gn_g[0]), "gn_b": row(rw_gn_b[0]),
        "w2": pad_rows(rw_w2[0], LANES), "a2": pad_rows(rw_a2[0], LANES), "g2": pad_rows(rw_g2[0], 2 * LANES),
    }
    g0, b0 = row(ln0_g), row(ln0_b)
    g1, b1 = row(ln1_g[0]), row(ln1_b[0])
    g2, b2 = row(ln2_g[0]), row(ln2_b[0])
    sb_gain = row(sb_norm_g[0])
    wo = w_out[0].astype(BF16)
    wo_a, wo_b = wo[:C_RW], wo[C_RW:]
    wr_hi, wr_lo = _split2(w_router[0].T)
    rbias = router_bias[0].astype(F32).reshape(ne, 1)
    wsg, wsu, wsd = w_sh_gate[0].astype(BF16), w_sh_up[0].astype(BF16), w_sh_down[0].astype(BF16)

    xp = x_prompt.reshape(bsz * seq, d)
    xs = x_sample.reshape(dbs * dseq, d)
    xm = meta_tokens.astype(x_prompt.dtype)
    rkv_m, _, k_m, v_m, low_m = _in_proj(xm, g0, b0, w6, wlow, N_META)
    rkv_p, q_p, k_p, v_p, low_p = _in_proj(xp, g0, b0, w6, wlow, _pick_tile(bsz * seq, 512))
    rkv_s, q_s, k_s, v_s, low_s = _in_proj(xs, g0, b0, w6, wlow, _pick_tile(dbs * dseq, 512))

    zero_shift = jnp.zeros((1, 1, 3 * C_RW + LOW_PAD), F32)
    zero_wkv = jnp.zeros((1, n_h, HEAD, HEAD), F32)
    _, wkv_m = _rwkv(rkv_m[None], low_m[None], zero_shift, zero_wkv, prm, N_META)
    shift_m = jnp.concatenate([rkv_m[-1:], low_m[-1:]], -1)[None]
    rw_p, wkv_p = _rwkv(rkv_p.reshape(bsz, seq, -1), low_p.reshape(bsz, seq, -1), shift_m, wkv_m, prm,
                        _pick_tile(seq, 64))
    shift_s0 = pad_low(state_rwkv_shift[0])
    rw_s, wkv_s = _rwkv(rkv_s.reshape(dbs, dseq, -1), low_s.reshape(dbs, dseq, -1), shift_s0,
                        state_rwkv_wkv[0], prm, dseq)

    c3 = lambda a, b_, l_: a.reshape(b_, l_, C_SB)
    sb_p = _attention(c3(q_p, bsz, seq), c3(k_p, bsz, seq), c3(v_p, bsz, seq), k_m[None], v_m[None], sb_gain,
                      _pick_tile(seq, 256), N_META)
    sb_s = _attention(c3(q_s, dbs, dseq), c3(k_s, dbs, dseq), c3(v_s, dbs, dseq),
                      cache_sb_k[0].reshape(dbs, past, C_SB), cache_sb_v[0].reshape(dbs, past, C_SB), sb_gain,
                      dseq, _pick_tile(past, 256))

    x_all = jnp.concatenate([xp, xs], 0)
    rw_all = jnp.concatenate([rw_p.reshape(bsz * seq, C_RW), rw_s.reshape(dbs * dseq, C_RW)], 0)
    sb_all = jnp.concatenate([sb_p.reshape(bsz * seq, C_SB), sb_s.reshape(dbs * dseq, C_SB)], 0)
    t_all = x_all.shape[0]
    tm = _pick_tile(t_all, 256)
    h, hb, idx_t, gate_t = _out_proj(x_all, rw_all, sb_all, wo_a, wo_b, g0, b0, g1, b1, wr_hi, wr_lo, rbias, tm)

    row_tok, row_w, blk_exp, n_valid, pos = _dispatch_plan(idx_t, gate_t, ne)
    expert_out = _experts(h, row_tok, row_w, blk_exp, n_valid, w_exp_gate[0], w_exp_up[0], w_exp_down[0])
    routed = jnp.sum(jnp.take(expert_out, pos, axis=0).astype(F32), axis=0)
    y = _final(h, hb, routed, wsg, wsu, wsd, g2, b2, tm)

    y_prompt = y[:bsz * seq].reshape(bsz, seq, d)
    y_sample = y[bsz * seq:].reshape(dbs, dseq, d)
    heads = lambda a, b_, l_: a.reshape(b_, l_, C_SB // HEAD, HEAD)

    def with_meta(m, p):
        m4 = jnp.broadcast_to(heads(m, 1, N_META), (bsz, N_META, C_SB // HEAD, HEAD))
        return jnp.concatenate([m4, heads(p, bsz, seq)], 1)[None]

    k_prompt = with_meta(k_m, k_p)
    v_prompt = with_meta(v_m, v_p)
    last = lambda a, b_, l_: a.reshape(b_, l_, -1)[:, -1:]
    shift_prompt = unpad_low(last(rkv_p, bsz, seq), last(low_p, bsz, seq))[None]
    shift_sample = unpad_low(last(rkv_s, dbs, dseq), last(low_s, dbs, dseq))[None]
    return (y_prompt, y_sample, k_prompt, v_prompt, wkv_p[None], shift_prompt,
            heads(k_s, dbs, dseq)[None], heads(v_s, dbs, dseq)[None], wkv_s[None], shift_sample)
```

```python
import functools

import jax
import jax.numpy as jnp
from jax import lax
from jax.experimental import pallas as pl
from jax.experimental.pallas import tpu as pltpu

F32 = jnp.float32
BF16 = jnp.bfloat16
HIGHEST = lax.Precision.HIGHEST

N_META = 16
HEAD = 64
C_RW = 1024
C_SB = 1024
D_DECAY = 64
D_AAA = 64
D_GATE = 160
LOW_PAD = 512
TOP_K = 8
N_GROUPS = 8
TOPK_GROUPS = 4
ROUTED_SCALE = 2.5
LN_EPS = 1e-5
GN_EPS = 64e-5
RMS_EPS = 1e-6
DEPTH = 1
DN_ALPHA = (2 * DEPTH) ** 0.25
LOG2E = 1.4426950408889634
EXP2_UNDERFLOW = -170.0

LANES = 128
VMEM_LIMIT = 56 * 1024 * 1024
RW_HEADS_PER_STEP = 16
EXPERT_BLOCK = 256


def _cparams(sem):
    return pltpu.CompilerParams(dimension_semantics=sem, vmem_limit_bytes=VMEM_LIMIT)


def _layer_norm(x, g, b):
    mu = jnp.mean(x, -1, keepdims=True)
    xc = x - mu
    var = jnp.mean(xc * xc, -1, keepdims=True)
    return xc * lax.rsqrt(var + LN_EPS) * g + b


def _sigmoid(x):
    return 1.0 / (1.0 + jnp.exp(-x))


def _softplus(x):
    return jnp.maximum(x, 0.0) + jnp.log(1.0 + jnp.exp(-jnp.abs(x)))


def _softplus2(x2):
    neg_abs = lax.bitcast_convert_type(lax.bitcast_convert_type(x2, jnp.uint32) | jnp.uint32(0x80000000), F32)
    return jnp.maximum(x2, 0.0) + jnp.log(1.0 + jnp.exp2(neg_abs)) * LOG2E


def _dot(a, b, precision=None):
    return jnp.dot(a, b, preferred_element_type=F32, precision=precision)


def _dot_nt(a, b, precision=None):
    return lax.dot_general(a, b, (((1,), (1,)), ((), ())), preferred_element_type=F32, precision=precision)


def _dot_tn(a, b, precision=None):
    return lax.dot_general(a, b, (((0,), (0,)), ((), ())), preferred_element_type=F32, precision=precision)


def _split2(x):
    hi = x.astype(BF16)
    return hi, (x - hi.astype(F32)).astype(BF16)


def _split3(x):
    hi = x.astype(BF16)
    r1 = x - hi.astype(F32)
    mid = r1.astype(BF16)
    return hi, mid, (r1 - mid.astype(F32)).astype(BF16)


def _dot_x3(a, b, dot=_dot):
    ah, al = _split2(a)
    bh, bl = _split2(b)
    return dot(ah, bh) + dot(al, bh) + dot(ah, bl)


def _dot_bf(a, b, dot=_dot):
    return dot(a.astype(BF16), b.astype(BF16))


RW_PREC = {"tri": "bf", "dbl": "bf", "wu": "bf", "ro": "bf", "pq": "bf", "state": "x3"}


def _mm(site, a, b, dot=_dot):
    mode = RW_PREC[site]
    if mode == "bf":
        return _dot_bf(a, b, dot)
    if mode == "x3":
        return _dot_x3(a, b, dot)
    return dot(a, b, HIGHEST)


def _in_proj_kernel(x_ref, g_ref, b_ref, w_ref, wlow_ref, rkv_ref, q_ref, k_ref, v_ref, low_ref, xn_ref):
    j = pl.program_id(1)

    @pl.when(j == 0)
    def _():
        xn = _layer_norm(x_ref[...], g_ref[...], b_ref[...]).astype(BF16)
        xn_ref[...] = xn
        low_ref[...] = _dot(xn, wlow_ref[...])

    y = _dot(xn_ref[...], w_ref[...])

    @pl.when(j < 3)
    def _():
        rkv_ref[...] = y

    @pl.when(j == 3)
    def _():
        q_ref[...] = y

    @pl.when(j == 4)
    def _():
        k_ref[...] = y

    @pl.when(j == 5)
    def _():
        v_ref[...] = y


def _in_proj(x, ln_g, ln_b, w6, wlow, tm):
    t, d = x.shape
    assert t % tm == 0
    c = C_RW
    grid = (t // tm, 6)
    col = lambda i, j: (i, 0)
    return pl.pallas_call(
        _in_proj_kernel,
        grid=grid,
        in_specs=[
            pl.BlockSpec((tm, d), col),
            pl.BlockSpec((1, d), lambda i, j: (0, 0)),
            pl.BlockSpec((1, d), lambda i, j: (0, 0)),
            pl.BlockSpec((d, c), lambda i, j: (0, j)),
            pl.BlockSpec((d, LOW_PAD), lambda i, j: (0, 0)),
        ],
        out_specs=[
            pl.BlockSpec((tm, c), lambda i, j: (i, jnp.minimum(j, 2))),
            pl.BlockSpec((tm, c), col),
            pl.BlockSpec((tm, c), col),
            pl.BlockSpec((tm, c), col),
            pl.BlockSpec((tm, LOW_PAD), col),
        ],
        out_shape=[
            jax.ShapeDtypeStruct((t, 3 * c), F32),
            jax.ShapeDtypeStruct((t, c), F32),
            jax.ShapeDtypeStruct((t, c), F32),
            jax.ShapeDtypeStruct((t, c), F32),
            jax.ShapeDtypeStruct((t, LOW_PAD), F32),
        ],
        scratch_shapes=[pltpu.VMEM((tm, d), BF16)],
        compiler_params=_cparams(("parallel", "arbitrary")),
        name="in_proj",
    )(x, ln_g, ln_b, w6, wlow)


def _rwkv_kernel(r_ref, k_ref, v_ref, low_ref, sr_ref, sk_ref, sv_ref, slow_ref,
                 mur_ref, muk_ref, muv_ref, mulow_ref,
                 w0_ref, a0_ref, kk_ref, ka_ref, rk_ref, gng_ref, gnb_ref,
                 w2_ref, a2_ref, g2_ref, bd_ref, tri_ref, wkv0_ref,
                 out_ref, wkv_ref,
                 s_ref, pr_ref, pk_ref, pv_ref, plow_ref, *, tc, hps, n_dbl):
    c = pl.program_id(2)
    nc = pl.num_programs(2)

    @pl.when(c == 0)
    def _():
        s_ref[...] = wkv0_ref[...]
        pr_ref[...] = sr_ref[...]
        pk_ref[...] = sk_ref[...]
        pv_ref[...] = sv_ref[...]
        plow_ref[...] = slow_ref[...]

    def token_shift(cur_ref, prev_ref, mu_ref):
        cur = cur_ref[...]
        row = lax.broadcasted_iota(jnp.int32, cur.shape, 0)
        prev = jnp.where(row == 0, prev_ref[...], pltpu.roll(cur, 1, 0))
        prev_ref[...] = cur[tc - 1:tc, :]
        return cur + (prev - cur) * mu_ref[...]

    r = token_shift(r_ref, pr_ref, mur_ref)
    k = token_shift(k_ref, pk_ref, muk_ref)
    v = token_shift(v_ref, pv_ref, muv_ref)
    xl = token_shift(low_ref, plow_ref, mulow_ref)
    wd = xl[:, 0:LANES]
    ad = xl[:, LANES:2 * LANES]
    gd = xl[:, 2 * LANES:LOW_PAD]

    w_log = -_softplus(-(w0_ref[...] + _dot_x3(jnp.tanh(wd), w2_ref[...]))) - 0.5
    lw = -jnp.exp(w_log)
    a = _sigmoid(a0_ref[...] + _dot_x3(ad, a2_ref[...]))
    g = _dot_x3(_sigmoid(gd), g2_ref[...])

    bd = bd_ref[...]
    nb_lane = bd.shape[0]

    def head_sum(x):
        parts = []
        for j in range(x.shape[1] // nb_lane):
            hi, lo = _split2(x[:, j * nb_lane:(j + 1) * nb_lane])
            parts.append(_dot(hi, bd) + _dot(lo, bd))
        return parts[0] if len(parts) == 1 else jnp.concatenate(parts, axis=1)

    kk = k * kk_ref[...]
    kk = kk / jnp.maximum(jnp.sqrt(head_sum(kk * kk)), 1e-12)
    k2 = k * (1.0 + (a - 1.0) * ka_ref[...])
    bonus = head_sum(r * k2 * rk_ref[...]) * v

    ti = lax.broadcasted_iota(jnp.int32, (tc, tc), 0)
    si = lax.broadcasted_iota(jnp.int32, (tc, tc), 1)
    incl = si <= ti
    strict = si < ti
    tri = tri_ref[...]
    l1, l2, l3 = _split3(lw)
    cs = _dot(tri, l1) + _dot(tri, l2) + _dot(tri, l3)
    c_last = cs[tc - 1:tc, :]
    e_pos = jnp.exp(cs)
    e_neg = jnp.exp(-cs)
    e_end = jnp.exp(c_last - cs)
    b = kk * a
    r_t = r * e_pos
    al_t = -kk * jnp.exp(cs - lw)
    be_t = b * e_neg
    k_t = k2 * e_neg
    be_h = b * e_end
    k_h = k2 * e_end
    g_last = jnp.exp(c_last)

    eye_t = (si == ti).astype(F32)
    di = lax.broadcasted_iota(jnp.int32, (HEAD, HEAD), 0)
    dj = lax.broadcasted_iota(jnp.int32, (HEAD, HEAD), 1)
    eye_h = di == dj

    hs = range(hps)
    cut = lambda x: [x[:, h * HEAD:(h + 1) * HEAD] for h in hs]
    al_h, r_h, v_h, bet, kt, beh, kh, gl = (cut(x) for x in (al_t, r_t, v, be_t, k_t, be_h, k_h, g_last))
    a_mat = [jnp.where(strict, _mm("tri", al_h[h], bet[h], _dot_nt), 0.0) for h in hs]
    b_mat = [jnp.where(strict, _mm("tri", al_h[h], kt[h], _dot_nt), 0.0) for h in hs]
    ar_mat = [jnp.where(incl, _mm("tri", r_h[h], bet[h], _dot_nt), 0.0) for h in hs]
    br_mat = [jnp.where(incl, _mm("tri", r_h[h], kt[h], _dot_nt), 0.0) for h in hs]
    bv = [_mm("wu", b_mat[h], v_h[h]) for h in hs]
    m = [eye_t + a_mat[h] for h in hs]
    pw = a_mat
    for _ in range(n_dbl):
        pw = [_mm("dbl", pw[h], pw[h]) for h in hs]
        m = [m[h] + _mm("dbl", m[h], pw[h]) for h in hs]
    w_m = [_mm("wu", m[h], al_h[h]) for h in hs]
    u_loc = [_mm("wu", m[h], bv[h]) for h in hs]
    r_hat = [r_h[h] + _mm("ro", ar_mat[h], w_m[h]) for h in hs]
    o_loc = [_mm("ro", ar_mat[h], u_loc[h]) + _mm("ro", br_mat[h], v_h[h]) for h in hs]
    p_mat = [_mm("pq", w_m[h], beh[h], _dot_tn) + jnp.where(eye_h, gl[h], 0.0) for h in hs]
    q_mat = [_mm("pq", u_loc[h], beh[h], _dot_tn) + _mm("pq", v_h[h], kh[h], _dot_tn) for h in hs]
    s0 = [s_ref[h] for h in hs]
    o = [o_loc[h] + _mm("state", r_hat[h], s0[h], _dot_nt) for h in hs]
    for h in hs:
        s_ref[h] = _mm("state", s0[h], p_mat[h]) + q_mat[h]
    outs = []
    for h in hs:
        mu = jnp.mean(o[h], -1, keepdims=True)
        oc = o[h] - mu
        var = jnp.mean(oc * oc, -1, keepdims=True)
        outs.append(oc * lax.rsqrt(var + GN_EPS))
    on = jnp.concatenate(outs, axis=1)
    out_ref[...] = ((on * gng_ref[...] + gnb_ref[...] + bonus) * g).astype(out_ref.dtype)

    @pl.when(c == nc - 1)
    def _():
        wkv_ref[...] = s_ref[...]


def _rwkv(rkv, low, shift0, wkv0, prm, tc):
    bsz, length, _ = rkv.shape
    b0 = shift0.shape[0]
    assert length % tc == 0 and b0 in (1, bsz)
    hps = RW_HEADS_PER_STEP
    w = hps * HEAD
    n_h = C_RW // HEAD
    ng = n_h // hps
    nblk = C_RW // w
    n_dbl = max((tc - 1).bit_length() - 1, 0)
    sb = (lambda b: b) if b0 == bsz else (lambda b: 0)
    seq = lambda off: pl.BlockSpec((None, tc, w), lambda b, g, c: (b, c, off * nblk + g))
    row0 = lambda off: pl.BlockSpec((None, 1, w), lambda b, g, c: (sb(b), 0, off * nblk + g))
    vec = lambda off: pl.BlockSpec((1, w), lambda b, g, c: (0, off * nblk + g))
    low_blk = (3 * C_RW) // LOW_PAD
    nbd = 2 * LANES
    bd = (jnp.arange(nbd)[:, None] // HEAD == jnp.arange(nbd)[None, :] // HEAD).astype(BF16)
    tri = (jnp.arange(tc)[None, :] <= jnp.arange(tc)[:, None]).astype(BF16)
    kernel = functools.partial(_rwkv_kernel, tc=tc, hps=hps, n_dbl=n_dbl)
    return pl.pallas_call(
        kernel,
        grid=(bsz, ng, length // tc),
        in_specs=[
            seq(0), seq(1), seq(2),
            pl.BlockSpec((None, tc, LOW_PAD), lambda b, g, c: (b, c, 0)),
            row0(0), row0(1), row0(2),
            pl.BlockSpec((None, 1, LOW_PAD), lambda b, g, c: (sb(b), 0, low_blk)),
            vec(0), vec(1), vec(2),
            pl.BlockSpec((1, LOW_PAD), lambda b, g, c: (0, low_blk)),
            vec(0), vec(0), vec(0), vec(0), vec(0), vec(0), vec(0),
            pl.BlockSpec((LANES, w), lambda b, g, c: (0, g)),
            pl.BlockSpec((LANES, w), lambda b, g, c: (0, g)),
            pl.BlockSpec((2 * LANES, w), lambda b, g, c: (0, g)),
            pl.BlockSpec((nbd, nbd), lambda b, g, c: (0, 0)),
            pl.BlockSpec((tc, tc), lambda b, g, c: (0, 0)),
            pl.BlockSpec((None, hps, HEAD, HEAD), lambda b, g, c: (sb(b), g, 0, 0)),
        ],
        out_specs=[
            pl.BlockSpec((None, tc, w), lambda b, g, c: (b, c, g)),
            pl.BlockSpec((None, hps, HEAD, HEAD), lambda b, g, c: (b, g, 0, 0)),
        ],
        out_shape=[
            jax.ShapeDtypeStruct((bsz, length, C_RW), BF16),
            jax.ShapeDtypeStruct((bsz, n_h, HEAD, HEAD), F32),
        ],
        scratch_shapes=[
            pltpu.VMEM((hps, HEAD, HEAD), F32),
            pltpu.VMEM((1, w), F32), pltpu.VMEM((1, w), F32), pltpu.VMEM((1, w), F32),
            pltpu.VMEM((1, LOW_PAD), F32),
        ],
        compiler_params=_cparams(("parallel", "parallel", "arbitrary")),
        name="rwkv",
    )(rkv, rkv, rkv, low, shift0, shift0, shift0, shift0,
      prm["mu"], prm["mu"], prm["mu"], prm["mu"],
      prm["w0"], prm["a0"], prm["k_k"], prm["k_a"], prm["r_k"], prm["gn_g"], prm["gn_b"],
      prm["w2"], prm["a2"], prm["g2"], bd, tri, wkv0)


def _attn_kernel(q_ref, k_ref, v_ref, kp_ref, vp_ref, tri_ref, trip_ref, g_ref, out_ref, *scratch,
                 tq, pb, n_pre, pipe_main, pipe_pre):
    i = pl.program_id(2)
    q0 = pl.multiple_of(i * tq, tq)
    lane = lax.broadcasted_iota(jnp.int32, (tq, LANES), 1)
    first = lane < HEAD
    q2 = q_ref[pl.ds(q0, tq), :] * (HEAD ** -0.5 * LOG2E)
    q_heads = (jnp.where(first, q2, 0.0).astype(BF16), jnp.where(first, 0.0, q2).astype(BF16))
    tri = tri_ref[...]
    trip = trip_ref[...]
    bufs, kmax_ref = scratch[:-1], scratch[-1]

    @pl.when(i == 0)
    def _():
        kmax_ref[...] = jnp.maximum(jnp.max(jnp.abs(k_ref[...]), axis=0, keepdims=True),
                                    jnp.max(jnp.abs(kp_ref[...]), axis=0, keepdims=True))

    reach_w = jnp.abs(q2) * kmax_ref[...] * 1.02
    zb_a = jnp.sum(jnp.where(first, reach_w, 0.0), -1, keepdims=True)
    z_bound = (zb_a, jnp.sum(reach_w, -1, keepdims=True) - zb_a)
    row = lax.broadcasted_iota(jnp.int32, (tq, tq), 0)
    colm = lax.broadcasted_iota(jnp.int32, (tq, tq), 1)
    causal = colm < row

    def suffix_sum(sp, tri_m):
        return _dot(sp.astype(BF16), tri_m)

    def block(qh, kb, vb, tri_m, carry, acc, mask):
        z = _dot_nt(qh, kb)
        sp = _softplus2(z)
        if mask is not None:
            sp = jnp.where(mask, sp, 0.0)
        suffix = suffix_sum(sp, tri_m)
        p = jnp.exp2((z + carry) - suffix)
        if mask is not None:
            p = jnp.where(mask, p, 0.0)
        acc = acc + _dot(p.astype(BF16), vb)
        return carry - suffix[:, 0:1], acc

    def piped(kr, vr, bs, tri_m, count, block_of, bufs, state, first_scored):
        z_s, d_s, tot_s = bufs
        start = 1 if first_scored else 0
        if not first_scored:
            z_s[1] = jnp.zeros_like(z_s[1])
        d_s[...] = jnp.zeros_like(d_s)
        tot_s[...] = jnp.zeros_like(tot_s)
        last = jnp.maximum(count - 1, 0)
        rows = lambda m: pl.ds(pl.multiple_of(block_of(jnp.clip(m, 0, last)) * bs, bs), bs)

        def stages(n, par, st):
            kb = kr[rows(n), :].astype(BF16)
            for h in range(2):
                z_s[par, h] = _dot_nt(q_heads[h], kb)
            live = jnp.logical_and(n >= 2, n - 2 < count)
            vb = vr[rows(n - 2), :].astype(BF16)
            for h in range(2):
                p = jnp.exp2(d_s[h] + jnp.where(live, st[2 * h], -1e30))
                st[2 * h + 1] = st[2 * h + 1] + _dot(p.astype(BF16), vb)
                st[2 * h] = st[2 * h] - jnp.where(live, tot_s[h], 0.0)
            for h in range(2):
                z = z_s[1 - par, h]
                suffix = suffix_sum(_softplus2(z), tri_m)
                d_s[h] = z - suffix
                tot_s[h] = suffix[:, 0:1]
            return st

        trips = (count + 3 - start) // 2

        def body(c):
            j, st = c[0], list(c[2:])
            for k in range(2):
                st = stages(start + 2 * j + k, (start + k) % 2, st)
            reach = jnp.maximum(jnp.max(st[0] + z_bound[0]), jnp.max(st[2] + z_bound[1]))
            return (j + 1, reach < EXP2_UNDERFLOW, *st)

        cond = lambda c: jnp.logical_and(c[0] < trips, jnp.logical_not(c[1]))
        out = lax.while_loop(cond, body, (jnp.int32(0), jnp.bool_(False), *state))
        return out[2:]

    state = [jnp.zeros((tq, 1), F32), jnp.zeros((tq, LANES), F32)] * 2
    kd = k_ref[pl.ds(q0, tq), :].astype(BF16)
    if pipe_main:
        for h in range(2):
            bufs[0][0, h] = jnp.where(causal, _dot_nt(q_heads[h], kd), -1e30)
        state = piped(k_ref, v_ref, tq, tri, i + 1, lambda m: i - m, bufs[0:3], state, True)
    else:
        vd = v_ref[pl.ds(q0, tq), :].astype(BF16)
        state[0:2] = block(q_heads[0], kd, vd, tri, state[0], state[1], causal)
        state[2:4] = block(q_heads[1], kd, vd, tri, state[2], state[3], causal)

    if pipe_pre:
        state = piped(kp_ref, vp_ref, pb, trip, n_pre, lambda m: n_pre - 1 - m, bufs[-3:], state, False)
    else:
        for t in range(n_pre):
            rows = pl.ds((n_pre - 1 - t) * pb, pb)
            kb = kp_ref[rows, :].astype(BF16)
            vb = vp_ref[rows, :].astype(BF16)
            state = list(state)
            state[0:2] = block(q_heads[0], kb, vb, trip, state[0], state[1], None)
            state[2:4] = block(q_heads[1], kb, vb, trip, state[2], state[3], None)

    o = jnp.where(first, state[1], state[3])
    sq = o * o
    s_a = jnp.sum(jnp.where(first, sq, 0.0), -1, keepdims=True)
    s_b = jnp.sum(sq, -1, keepdims=True) - s_a
    inv = jnp.where(first, lax.rsqrt(s_a / HEAD + RMS_EPS), lax.rsqrt(s_b / HEAD + RMS_EPS))
    out_ref[...] = (o * inv * g_ref[...]).astype(out_ref.dtype)


def _attention(q, k, v, kp, vp, gain, tq, pb):
    bsz, length, c = q.shape
    b0, plen, _ = kp.shape
    assert length % tq == 0 and plen % pb == 0 and b0 in (1, bsz)
    sb = (lambda b: b) if b0 == bsz else (lambda b: 0)
    n_pre = plen // pb
    n_main = length // tq
    pipe_main = n_main > 1
    pipe_pre = n_pre >= 3
    tri1 = lambda n: (jnp.arange(n)[:, None] >= jnp.arange(n)[None, :]).astype(BF16)
    stage_bufs = lambda bs: [pltpu.VMEM((2, 2, tq, bs), F32), pltpu.VMEM((2, tq, bs), F32),
                             pltpu.VMEM((2, tq, 1), F32)]
    full = pl.BlockSpec((None, length, LANES), lambda b, h, i: (b, 0, h))
    pre = pl.BlockSpec((None, plen, LANES), lambda b, h, i: (sb(b), 0, h))
    kernel = functools.partial(_attn_kernel, tq=tq, pb=pb, n_pre=n_pre, pipe_main=pipe_main, pipe_pre=pipe_pre)
    return pl.pallas_call(
        kernel,
        grid=(bsz, c // LANES, n_main),
        in_specs=[
            full, full, full, pre, pre,
            pl.BlockSpec((tq, tq), lambda b, h, i: (0, 0)),
            pl.BlockSpec((pb, pb), lambda b, h, i: (0, 0)),
            pl.BlockSpec((1, LANES), lambda b, h, i: (0, h)),
        ],
        out_specs=pl.BlockSpec((None, tq, LANES), lambda b, h, i: (b, i, h)),
        out_shape=jax.ShapeDtypeStruct((bsz, length, c), BF16),
        scratch_shapes=((stage_bufs(tq) if pipe_main else []) + (stage_bufs(pb) if pipe_pre else [])
                        + [pltpu.VMEM((1, LANES), F32)]),
        compiler_params=_cparams(("parallel", "parallel", "arbitrary")),
        name="attn",
    )(q, k, v, kp, vp, tri1(tq), tri1(pb), gain)


def _route_columns(logits_t, bias):
    ne, n = logits_t.shape
    gsz = ne // N_GROUPS
    neg = -jnp.inf
    scores = _sigmoid(logits_t)
    sel = scores + bias
    ig = lax.broadcasted_iota(jnp.int32, (gsz, n), 0)
    gs = []
    for g in range(N_GROUPS):
        xg = sel[g * gsz:(g + 1) * gsz, :]
        m1 = jnp.max(xg, axis=0, keepdims=True)
        first = jnp.min(jnp.where(xg == m1, ig, gsz), axis=0, keepdims=True)
        m2 = jnp.max(jnp.where(ig == first, neg, xg), axis=0, keepdims=True)
        gs.append(m1 + m2)
    kept = []
    for g in range(N_GROUPS):
        rank = jnp.zeros((1, n), jnp.int32)
        for g2 in range(N_GROUPS):
            if g2 == g:
                continue
            ahead = (gs[g2] >= gs[g]) if g2 < g else (gs[g2] > gs[g])
            rank = rank + ahead.astype(jnp.int32)
        kept.append(jnp.where(rank < TOPK_GROUPS, sel[g * gsz:(g + 1) * gsz, :], neg))
    cand = jnp.concatenate(kept, axis=0)
    ie = lax.broadcasted_iota(jnp.int32, (ne, n), 0)
    ids, gates = [], []
    for _ in range(TOP_K):
        m = jnp.max(cand, axis=0, keepdims=True)
        idx = jnp.min(jnp.where(cand == m, ie, ne), axis=0, keepdims=True)
        hit = ie == idx
        gates.append(jnp.sum(jnp.where(hit, scores, 0.0), axis=0, keepdims=True))
        ids.append(idx)
        cand = jnp.where(hit, neg, cand)
    gate = jnp.concatenate(gates, axis=0)
    gate = gate / jnp.sum(gate, axis=0, keepdims=True) * ROUTED_SCALE
    return jnp.concatenate(ids, axis=0), gate


def _out_proj_kernel(x_ref, rw_ref, sb_ref, wa_ref, wb_ref, g0_ref, b0_ref, g1_ref, b1_ref,
                     wrh_ref, wrl_ref, rb_ref, h_ref, hb_ref, idx_ref, gate_ref):
    xn = _layer_norm(x_ref[...], g0_ref[...], b0_ref[...])
    mix = _dot(rw_ref[...], wa_ref[...]) + _dot(sb_ref[...], wb_ref[...])
    h = _layer_norm(DN_ALPHA * xn + mix, g1_ref[...], b1_ref[...])
    h_ref[...] = h
    hi, lo = _split2(h)
    hb_ref[...] = hi
    wrh = wrh_ref[...]
    logits_t = _dot_nt(wrh, hi) + _dot_nt(wrh, lo) + _dot_nt(wrl_ref[...], hi)
    idx, gate = _route_columns(logits_t, rb_ref[...])
    idx_ref[...] = idx
    gate_ref[...] = gate


def _out_proj(x, rw, sbo, wa, wb, g0, b0, g1, b1, wrh, wrl, rbias, tm):
    t, d = x.shape
    assert t % tm == 0
    row = lambda w: pl.BlockSpec((tm, w), lambda i: (i, 0))
    col = pl.BlockSpec((TOP_K, tm), lambda i: (0, i))
    const = lambda a: pl.BlockSpec(a.shape, lambda i: (0, 0))
    return pl.pallas_call(
        _out_proj_kernel,
        grid=(t // tm,),
        in_specs=[row(d), row(C_RW), row(C_SB), const(wa), const(wb), const(g0), const(b0), const(g1), const(b1),
                  const(wrh), const(wrl), const(rbias)],
        out_specs=[row(d), row(d), col, col],
        out_shape=[jax.ShapeDtypeStruct((t, d), F32), jax.ShapeDtypeStruct((t, d), BF16),
                   jax.ShapeDtypeStruct((TOP_K, t), jnp.int32), jax.ShapeDtypeStruct((TOP_K, t), F32)],
        compiler_params=_cparams(("parallel",)),
        name="out_proj",
    )(x, rw, sbo, wa, wb, g0, b0, g1, b1, wrh, wrl, rbias)


def _expert_kernel(be_ref, nv_ref, tok_ref, nxt_ref, rw_ref, h_hbm, wg_ref, wu_ref, wd_ref, out_ref,
                   xbuf, sems, xb_ref, wgb_ref, wub_ref, wdb_ref, prev_ref):
    i = pl.program_id(0)
    nv = nv_ref[0]
    e = be_ref[i]
    slot = lax.rem(i, 2)
    blk = xbuf.shape[1]

    def row_copy(src_row, r, s):
        return pltpu.make_async_copy(h_hbm.at[pl.ds(src_row, 1), :], xbuf.at[s, pl.ds(r, 1), :], sems.at[s])

    @pl.when(i == 0)
    def _():
        prev_ref[0] = -1
        for r in range(blk):
            row_copy(tok_ref[0, r], r, 0).start()

    @pl.when(i < nv)
    def _():
        @pl.when(e != prev_ref[0])
        def _():
            wgb_ref[...] = wg_ref[...].astype(BF16)
            wub_ref[...] = wu_ref[...].astype(BF16)
            wdb_ref[...] = wd_ref[...].astype(BF16)
            prev_ref[0] = e

        for r in range(blk):
            row_copy(0, r, slot).wait()
        xb_ref[...] = xbuf[slot].astype(BF16)
        for r in range(blk):
            row_copy(nxt_ref[0, r], r, 1 - slot).start()
        x = xb_ref[...]
        hg = _dot(x, wgb_ref[...])
        hu = _dot(x, wub_ref[...])
        hid = (hg * _sigmoid(hg) * hu).astype(BF16)
        out_ref[...] = (_dot(hid, wdb_ref[...]) * rw_ref[...]).astype(out_ref.dtype)

        @pl.when(i == nv - 1)
        def _():
            for r in range(blk):
                row_copy(0, r, 1 - slot).wait()

    @pl.when(i >= nv)
    def _():
        out_ref[...] = jnp.zeros_like(out_ref)


def _experts(h, row_tok, row_w, blk_exp, n_valid, w_gate, w_up, w_down):
    _, d = h.shape
    _, _, de = w_gate.shape
    blk = EXPERT_BLOCK
    nb = row_tok.shape[0] // blk
    tok3 = row_tok.reshape(nb, 1, blk)
    live = lambda i, nv: jnp.minimum(i, nv[0] - 1)
    idx_spec = lambda step: pl.BlockSpec((None, 1, blk), lambda i, be, nv: (live(i + step, nv), 0, 0),
                                         memory_space=pltpu.SMEM)
    grid_spec = pltpu.PrefetchScalarGridSpec(
        num_scalar_prefetch=2,
        grid=(nb,),
        in_specs=[
            idx_spec(0), idx_spec(1),
            pl.BlockSpec((blk, 1), lambda i, be, nv: (live(i, nv), 0)),
            pl.BlockSpec(memory_space=pl.ANY),
            pl.BlockSpec((None, d, de), lambda i, be, nv: (be[i], 0, 0)),
            pl.BlockSpec((None, d, de), lambda i, be, nv: (be[i], 0, 0)),
            pl.BlockSpec((None, de, d), lambda i, be, nv: (be[i], 0, 0)),
        ],
        out_specs=pl.BlockSpec((blk, d), lambda i, be, nv: (i, 0)),
        scratch_shapes=[pltpu.VMEM((2, blk, d), F32), pltpu.SemaphoreType.DMA((2,)), pltpu.VMEM((blk, d), BF16),
                        pltpu.VMEM((d, de), BF16), pltpu.VMEM((d, de), BF16), pltpu.VMEM((de, d), BF16),
                        pltpu.SMEM((1,), jnp.int32)],
    )
    return pl.pallas_call(
        _expert_kernel,
        grid_spec=grid_spec,
        out_shape=jax.ShapeDtypeStruct((nb * blk, d), BF16),
        compiler_params=_cparams(("arbitrary",)),
        name="experts",
    )(blk_exp, n_valid, tok3, tok3, row_w, h, w_gate, w_up, w_down)


def _final_kernel(h_ref, hb_ref, routed_ref, wg_ref, wu_ref, wd_ref, g_ref, b_ref, out_ref):
    x = hb_ref[...]
    hg = _dot(x, wg_ref[...])
    hu = _dot(x, wu_ref[...])
    shared = _dot((hg * _sigmoid(hg) * hu).astype(BF16), wd_ref[...])
    y = DN_ALPHA * h_ref[...] + routed_ref[...] + shared
    out_ref[...] = _layer_norm(y, g_ref[...], b_ref[...])


def _final(h, hb, routed, wg, wu, wd, g, b, tm):
    t, d = h.shape
    assert t % tm == 0
    row = pl.BlockSpec((tm, d), lambda i: (i, 0))
    const = lambda a: pl.BlockSpec(a.shape, lambda i: (0, 0))
    return pl.pallas_call(
        _final_kernel,
        grid=(t // tm,),
        in_specs=[row, row, row, const(wg), const(wu), const(wd), const(g), const(b)],
        out_specs=row,
        out_shape=jax.ShapeDtypeStruct((t, d), F32),
        compiler_params=_cparams(("parallel",)),
        name="final",
    )(h, hb, routed, wg, wu, wd, g, b)


def _dispatch_plan(idx_t, gate_t, ne):
    t = idx_t.shape[1]
    blk = EXPERT_BLOCK
    n_assign = t * TOP_K
    nb = n_assign // blk + ne
    n_rows = nb * blk
    n_fill = n_rows - n_assign
    e_flat = idx_t.reshape(n_assign)
    iota = jnp.arange(n_assign, dtype=jnp.int32)
    experts = jnp.arange(ne, dtype=jnp.int32)
    counts = jnp.sum((e_flat[:, None] == experts[None, :]).astype(jnp.int32), axis=0)
    padded = (counts + blk - 1) // blk * blk
    n_valid = (jnp.sum(padded) // blk).astype(jnp.int32).reshape(1)
    fill_end = jnp.cumsum(padded - counts)
    fill_e = jnp.sum((fill_end[None, :] <= jnp.arange(n_fill, dtype=jnp.int32)[:, None]).astype(jnp.int32), axis=1)
    zeros = jnp.zeros((n_fill,), jnp.int32)
    keys = jnp.concatenate([2 * e_flat, 2 * fill_e + 1])
    toks = jnp.concatenate([iota % t, zeros])
    wts = jnp.concatenate([gate_t.reshape(n_assign), zeros.astype(F32)])
    aid = jnp.concatenate([iota, zeros + n_assign])
    keys, row_tok, row_w, row_aid = lax.sort((keys, toks, wts, aid), num_keys=1)
    blk_exp = jnp.minimum(keys[::blk] // 2, ne - 1)
    _, pos = lax.sort((row_aid, jnp.arange(n_rows, dtype=jnp.int32)), num_keys=1)
    return row_tok, row_w.reshape(n_rows, 1), blk_exp, n_valid, pos[:n_assign].reshape(TOP_K, t)


def _pick_tile(n, cap):
    t = cap
    while n % t:
        t //= 2
    return t


def kernel(x_prompt, x_sample, cache_sb_k, cache_sb_v, state_rwkv_wkv, state_rwkv_shift, meta_tokens, ln0_g, ln0_b, w_in, rw_mu, rw_w0, rw_w2, rw_a0, rw_a2, rw_g2, rw_k_k, rw_k_a, rw_r_k, rw_gn_g, rw_gn_b, sb_norm_g, w_out, ln1_g, ln1_b, w_router, router_bias, w_exp_gate, w_exp_up, w_exp_down, w_sh_gate, w_sh_up, w_sh_down, ln2_g, ln2_b):
    bsz, seq, d = x_prompt.shape
    dbs, dseq, _ = x_sample.shape
    past = cache_sb_k.shape[2]
    n_h = C_RW // HEAD
    ne = w_router.shape[-1]
    rw_cols = 3 * C_RW + D_DECAY + D_AAA + D_GATE
    o_wd, o_ad, o_gd = 3 * C_RW, 3 * C_RW + D_DECAY, 3 * C_RW + D_DECAY + D_AAA

    def pad_low(a):
        z = lambda n: jnp.zeros(a.shape[:-1] + (n,), a.dtype)
        return jnp.concatenate([a[..., :o_wd], a[..., o_wd:o_ad], z(LANES - D_DECAY), a[..., o_ad:o_gd],
                                z(LANES - D_AAA), a[..., o_gd:rw_cols], z(2 * LANES - D_GATE)], -1)

    def unpad_low(rkv_row, low_row):
        return jnp.concatenate([rkv_row, low_row[..., 0:D_DECAY], low_row[..., LANES:LANES + D_AAA],
                                low_row[..., 2 * LANES:2 * LANES + D_GATE]], -1)

    wi = w_in[0]
    w_rw = pad_low(wi[:, :rw_cols])
    w6 = jnp.concatenate([w_rw[:, :3 * C_RW], wi[:, rw_cols:]], 1).astype(BF16)
    wlow = w_rw[:, 3 * C_RW:].astype(BF16)
    pad_rows = lambda a, n: jnp.concatenate([a, jnp.zeros((n - a.shape[0], a.shape[1]), a.dtype)], 0)
    row = lambda a: a.reshape(1, -1)
    prm = {
        "mu": pad_low(rw_mu[0]).reshape(1, -1),
        "w0": row(rw_w0[0]), "a0": row(rw_a0[0]), "k_k": row(rw_k_k[0]), "k_a": row(rw_k_a[0]),
        "r_k": row(rw_r_k[0]), "gn_g": row(rw_gn_g[0]), "gn_b": row(rw_gn_b[0]),
        "w2": pad_rows(rw_w2[0], LANES), "a2": pad_rows(rw_a2[0], LANES), "g2": pad_rows(rw_g2[0], 2 * LANES),
    }
    g0, b0 = row(ln0_g), row(ln0_b)
    g1, b1 = row(ln1_g[0]), row(ln1_b[0])
    g2, b2 = row(ln2_g[0]), row(ln2_b[0])
    sb_gain = row(sb_norm_g[0])
    wo = w_out[0].astype(BF16)
    wo_a, wo_b = wo[:C_RW], wo[C_RW:]
    wr_hi, wr_lo = _split2(w_router[0].T)
    rbias = router_bias[0].astype(F32).reshape(ne, 1)
    wsg, wsu, wsd = w_sh_gate[0].astype(BF16), w_sh_up[0].astype(BF16), w_sh_down[0].astype(BF16)

    xp = x_prompt.reshape(bsz * seq, d)
    xs = x_sample.reshape(dbs * dseq, d)
    xm = meta_tokens.astype(x_prompt.dtype)
    rkv_m, _, k_m, v_m, low_m = _in_proj(xm, g0, b0, w6, wlow, N_META)
    rkv_p, q_p, k_p, v_p, low_p = _in_proj(xp, g0, b0, w6, wlow, _pick_tile(bsz * seq, 512))
    rkv_s, q_s, k_s, v_s, low_s = _in_proj(xs, g0, b0, w6, wlow, _pick_tile(dbs * dseq, 512))

    zero_shift = jnp.zeros((1, 1, 3 * C_RW + LOW_PAD), F32)
    zero_wkv = jnp.zeros((1, n_h, HEAD, HEAD), F32)
    _, wkv_m = _rwkv(rkv_m[None], low_m[None], zero_shift, zero_wkv, prm, N_META)
    shift_m = jnp.concatenate([rkv_m[-1:], low_m[-1:]], -1)[None]
    rw_p, wkv_p = _rwkv(rkv_p.reshape(bsz, seq, -1), low_p.reshape(bsz, seq, -1), shift_m, wkv_m, prm,
                        _pick_tile(seq, 64))
    shift_s0 = pad_low(state_rwkv_shift[0])
    rw_s, wkv_s = _rwkv(rkv_s.reshape(dbs, dseq, -1), low_s.reshape(dbs, dseq, -1), shift_s0,
                        state_rwkv_wkv[0], prm, dseq)

    c3 = lambda a, b_, l_: a.reshape(b_, l_, C_SB)
    sb_p = _attention(c3(q_p, bsz, seq), c3(k_p, bsz, seq), c3(v_p, bsz, seq), k_m[None], v_m[None], sb_gain,
                      _pick_tile(seq, 256), N_META)
    sb_s = _attention(c3(q_s, dbs, dseq), c3(k_s, dbs, dseq), c3(v_s, dbs, dseq),
                      cache_sb_k[0].reshape(dbs, past, C_SB), cache_sb_v[0].reshape(dbs, past, C_SB), sb_gain,
                      dseq, _pick_tile(past, 256))

    x_all = jnp.concatenate([xp, xs], 0)
    rw_all = jnp.concatenate([rw_p.reshape(bsz * seq, C_RW), rw_s.reshape(dbs * dseq, C_RW)], 0)
    sb_all = jnp.concatenate([sb_p.reshape(bsz * seq, C_SB), sb_s.reshape(dbs * dseq, C_SB)], 0)
    t_all = x_all.shape[0]
    tm = _pick_tile(t_all, 256)
    h, hb, idx_t, gate_t = _out_proj(x_all, rw_all, sb_all, wo_a, wo_b, g0, b0, g1, b1, wr_hi, wr_lo, rbias, tm)

    row_tok, row_w, blk_exp, n_valid, pos = _dispatch_plan(idx_t, gate_t, ne)
    expert_out = _experts(h, row_tok, row_w, blk_exp, n_valid, w_exp_gate[0], w_exp_up[0], w_exp_down[0])
    routed = jnp.sum(jnp.take(expert_out, pos, axis=0).astype(F32), axis=0)
    y = _final(h, hb, routed, wsg, wsu, wsd, g2, b2, tm)

    y_prompt = y[:bsz * seq].reshape(bsz, seq, d)
    y_sample = y[bsz * seq:].reshape(dbs, dseq, d)
    heads = lambda a, b_, l_: a.reshape(b_, l_, C_SB // HEAD, HEAD)

    def with_meta(m, p):
        m4 = jnp.broadcast_to(heads(m, 1, N_META), (bsz, N_META, C_SB // HEAD, HEAD))
        return jnp.concatenate([m4, heads(p, bsz, seq)], 1)[None]

    k_prompt = with_meta(k_m, k_p)
    v_prompt = with_meta(v_m, v_p)
    last = lambda a, b_, l_: a.reshape(b_, l_, -1)[:, -1:]
    shift_prompt = unpad_low(last(rkv_p, bsz, seq), last(low_p, bsz, seq))[None]
    shift_sample = unpad_low(last(rkv_s, dbs, dseq), last(low_s, dbs, dseq))[None]
    return (y_prompt, y_sample, k_prompt, v_prompt, wkv_p[None], shift_prompt,
            heads(k_s, dbs, dseq)[None], heads(v_s, dbs, dseq)[None], wkv_s[None], shift_sample)
```

```python
import functools

import jax
import jax.numpy as jnp
from jax import lax
from jax.experimental import pallas as pl
from jax.experimental.pallas import tpu as pltpu

F32 = jnp.float32
BF16 = jnp.bfloat16
HIGHEST = lax.Precision.HIGHEST

N_META = 16
HEAD = 64
C_RW = 1024
C_SB = 1024
D_DECAY = 64
D_AAA = 64
D_GATE = 160
LOW_PAD = 512
TOP_K = 8
N_GROUPS = 8
TOPK_GROUPS = 4
ROUTED_SCALE = 2.5
LN_EPS = 1e-5
GN_EPS = 64e-5
RMS_EPS = 1e-6
DEPTH = 1
DN_ALPHA = (2 * DEPTH) ** 0.25
LOG2E = 1.4426950408889634
EXP2_UNDERFLOW = -170.0

LANES = 128
VMEM_LIMIT = 56 * 1024 * 1024
RW_HEADS_PER_STEP = 16
EXPERT_BLOCK = 256
GATHER_DMA_PRIORITY = 1


def _cparams(sem):
    return pltpu.CompilerParams(dimension_semantics=sem, vmem_limit_bytes=VMEM_LIMIT)


def _layer_norm(x, g, b):
    mu = jnp.mean(x, -1, keepdims=True)
    xc = x - mu
    var = jnp.mean(xc * xc, -1, keepdims=True)
    return xc * lax.rsqrt(var + LN_EPS) * g + b


def _sigmoid(x):
    return 1.0 / (1.0 + jnp.exp(-x))


def _softplus(x):
    return jnp.maximum(x, 0.0) + jnp.log(1.0 + jnp.exp(-jnp.abs(x)))


def _softplus2(x2):
    neg_abs = lax.bitcast_convert_type(lax.bitcast_convert_type(x2, jnp.uint32) | jnp.uint32(0x80000000), F32)
    return jnp.maximum(x2, 0.0) + jnp.log(1.0 + jnp.exp2(neg_abs)) * LOG2E


def _dot(a, b, precision=None):
    return jnp.dot(a, b, preferred_element_type=F32, precision=precision)


def _dot_nt(a, b, precision=None):
    return lax.dot_general(a, b, (((1,), (1,)), ((), ())), preferred_element_type=F32, precision=precision)


def _dot_tn(a, b, precision=None):
    return lax.dot_general(a, b, (((0,), (0,)), ((), ())), preferred_element_type=F32, precision=precision)


def _split2(x):
    hi = x.astype(BF16)
    return hi, (x - hi.astype(F32)).astype(BF16)


def _split3(x):
    hi = x.astype(BF16)
    r1 = x - hi.astype(F32)
    mid = r1.astype(BF16)
    return hi, mid, (r1 - mid.astype(F32)).astype(BF16)


def _dot_x3(a, b, dot=_dot):
    ah, al = _split2(a)
    bh, bl = _split2(b)
    return dot(ah, bh) + dot(al, bh) + dot(ah, bl)


def _dot_bf(a, b, dot=_dot):
    return dot(a.astype(BF16), b.astype(BF16))


RW_PREC = {"tri": "bf", "dbl": "bf", "wu": "bf", "ro": "bf", "pq": "bf", "state": "x3"}


def _mm(site, a, b, dot=_dot):
    mode = RW_PREC[site]
    if mode == "bf":
        return _dot_bf(a, b, dot)
    if mode == "x3":
        return _dot_x3(a, b, dot)
    return dot(a, b, HIGHEST)


def _in_proj_kernel(x_ref, g_ref, b_ref, w_ref, wlow_ref, rkv_ref, q_ref, k_ref, v_ref, low_ref, xn_ref):
    j = pl.program_id(1)

    @pl.when(j == 0)
    def _():
        xn = _layer_norm(x_ref[...], g_ref[...], b_ref[...]).astype(BF16)
        xn_ref[...] = xn
        low_ref[...] = _dot(xn, wlow_ref[...])

    y = _dot(xn_ref[...], w_ref[...])

    @pl.when(j < 3)
    def _():
        rkv_ref[...] = y

    @pl.when(j == 3)
    def _():
        q_ref[...] = y

    @pl.when(j == 4)
    def _():
        k_ref[...] = y

    @pl.when(j == 5)
    def _():
        v_ref[...] = y


def _in_proj(x, ln_g, ln_b, w6, wlow, tm):
    t, d = x.shape
    assert t % tm == 0
    c = C_RW
    grid = (t // tm, 6)
    col = lambda i, j: (i, 0)
    return pl.pallas_call(
        _in_proj_kernel,
        grid=grid,
        in_specs=[
            pl.BlockSpec((tm, d), col),
            pl.BlockSpec((1, d), lambda i, j: (0, 0)),
            pl.BlockSpec((1, d), lambda i, j: (0, 0)),
            pl.BlockSpec((d, c), lambda i, j: (0, j)),
            pl.BlockSpec((d, LOW_PAD), lambda i, j: (0, 0)),
        ],
        out_specs=[
            pl.BlockSpec((tm, c), lambda i, j: (i, jnp.minimum(j, 2))),
            pl.BlockSpec((tm, c), col),
            pl.BlockSpec((tm, c), col),
            pl.BlockSpec((tm, c), col),
            pl.BlockSpec((tm, LOW_PAD), col),
        ],
        out_shape=[
            jax.ShapeDtypeStruct((t, 3 * c), F32),
            jax.ShapeDtypeStruct((t, c), F32),
            jax.ShapeDtypeStruct((t, c), F32),
            jax.ShapeDtypeStruct((t, c), F32),
            jax.ShapeDtypeStruct((t, LOW_PAD), F32),
        ],
        scratch_shapes=[pltpu.VMEM((tm, d), BF16)],
        compiler_params=_cparams(("parallel", "arbitrary")),
        name="in_proj",
    )(x, ln_g, ln_b, w6, wlow)


def _rwkv_kernel(r_ref, k_ref, v_ref, low_ref, sr_ref, sk_ref, sv_ref, slow_ref,
                 mur_ref, muk_ref, muv_ref, mulow_ref,
                 w0_ref, a0_ref, kk_ref, ka_ref, rk_ref, gng_ref, gnb_ref,
                 w2_ref, a2_ref, g2_ref, bd_ref, tri_ref, wkv0_ref,
                 out_ref, wkv_ref,
                 s_ref, pr_ref, pk_ref, pv_ref, plow_ref, *, tc, hps, n_dbl):
    c = pl.program_id(2)
    nc = pl.num_programs(2)

    @pl.when(c == 0)
    def _():
        s_ref[...] = wkv0_ref[...]
        pr_ref[...] = sr_ref[...]
        pk_ref[...] = sk_ref[...]
        pv_ref[...] = sv_ref[...]
        plow_ref[...] = slow_ref[...]

    def token_shift(cur_ref, prev_ref, mu_ref):
        cur = cur_ref[...]
        row = lax.broadcasted_iota(jnp.int32, cur.shape, 0)
        prev = jnp.where(row == 0, prev_ref[...], pltpu.roll(cur, 1, 0))
        prev_ref[...] = cur[tc - 1:tc, :]
        return cur + (prev - cur) * mu_ref[...]

    r = token_shift(r_ref, pr_ref, mur_ref)
    k = token_shift(k_ref, pk_ref, muk_ref)
    v = token_shift(v_ref, pv_ref, muv_ref)
    xl = token_shift(low_ref, plow_ref, mulow_ref)
    wd = xl[:, 0:LANES]
    ad = xl[:, LANES:2 * LANES]
    gd = xl[:, 2 * LANES:LOW_PAD]

    w_log = -_softplus(-(w0_ref[...] + _dot_x3(jnp.tanh(wd), w2_ref[...]))) - 0.5
    lw = -jnp.exp(w_log)
    a = _sigmoid(a0_ref[...] + _dot_x3(ad, a2_ref[...]))
    g = _dot_x3(_sigmoid(gd), g2_ref[...])

    bd = bd_ref[...]
    nb_lane = bd.shape[0]

    def head_sum(x):
        parts = []
        for j in range(x.shape[1] // nb_lane):
            hi, lo = _split2(x[:, j * nb_lane:(j + 1) * nb_lane])
            parts.append(_dot(hi, bd) + _dot(lo, bd))
        return parts[0] if len(parts) == 1 else jnp.concatenate(parts, axis=1)

    kk = k * kk_ref[...]
    kk = kk / jnp.maximum(jnp.sqrt(head_sum(kk * kk)), 1e-12)
    k2 = k * (1.0 + (a - 1.0) * ka_ref[...])
    bonus = head_sum(r * k2 * rk_ref[...]) * v

    ti = lax.broadcasted_iota(jnp.int32, (tc, tc), 0)
    si = lax.broadcasted_iota(jnp.int32, (tc, tc), 1)
    incl = si <= ti
    strict = si < ti
    tri = tri_ref[...]
    l1, l2, l3 = _split3(lw)
    cs = _dot(tri, l1) + _dot(tri, l2) + _dot(tri, l3)
    c_last = cs[tc - 1:tc, :]
    e_pos = jnp.exp(cs)
    e_neg = jnp.exp(-cs)
    e_end = jnp.exp(c_last - cs)
    b = kk * a
    r_t = r * e_pos
    al_t = -kk * jnp.exp(cs - lw)
    be_t = b * e_neg
    k_t = k2 * e_neg
    be_h = b * e_end
    k_h = k2 * e_end
    g_last = jnp.exp(c_last)

    eye_t = (si == ti).astype(F32)
    di = lax.broadcasted_iota(jnp.int32, (HEAD, HEAD), 0)
    dj = lax.broadcasted_iota(jnp.int32, (HEAD, HEAD), 1)
    eye_h = di == dj

    hs = range(hps)
    cut = lambda x: [x[:, h * HEAD:(h + 1) * HEAD] for h in hs]
    al_h, r_h, v_h, bet, kt, beh, kh, gl = (cut(x) for x in (al_t, r_t, v, be_t, k_t, be_h, k_h, g_last))
    a_mat = [jnp.where(strict, _mm("tri", al_h[h], bet[h], _dot_nt), 0.0) for h in hs]
    b_mat = [jnp.where(strict, _mm("tri", al_h[h], kt[h], _dot_nt), 0.0) for h in hs]
    ar_mat = [jnp.where(incl, _mm("tri", r_h[h], bet[h], _dot_nt), 0.0) for h in hs]
    br_mat = [jnp.where(incl, _mm("tri", r_h[h], kt[h], _dot_nt), 0.0) for h in hs]
    bv = [_mm("wu", b_mat[h], v_h[h]) for h in hs]
    m = [eye_t + a_mat[h] for h in hs]
    pw = a_mat
    for _ in range(n_dbl):
        pw = [_mm("dbl", pw[h], pw[h]) for h in hs]
        m = [m[h] + _mm("dbl", m[h], pw[h]) for h in hs]
    w_m = [_mm("wu", m[h], al_h[h]) for h in hs]
    u_loc = [_mm("wu", m[h], bv[h]) for h in hs]
    r_hat = [r_h[h] + _mm("ro", ar_mat[h], w_m[h]) for h in hs]
    o_loc = [_mm("ro", ar_mat[h], u_loc[h]) + _mm("ro", br_mat[h], v_h[h]) for h in hs]
    p_mat = [_mm("pq", w_m[h], beh[h], _dot_tn) + jnp.where(eye_h, gl[h], 0.0) for h in hs]
    q_mat = [_mm("pq", u_loc[h], beh[h], _dot_tn) + _mm("pq", v_h[h], kh[h], _dot_tn) for h in hs]
    s0 = [s_ref[h] for h in hs]
    o = [o_loc[h] + _mm("state", r_hat[h], s0[h], _dot_nt) for h in hs]
    for h in hs:
        s_ref[h] = _mm("state", s0[h], p_mat[h]) + q_mat[h]
    outs = []
    for h in hs:
        mu = jnp.mean(o[h], -1, keepdims=True)
        oc = o[h] - mu
        var = jnp.mean(oc * oc, -1, keepdims=True)
        outs.append(oc * lax.rsqrt(var + GN_EPS))
    on = jnp.concatenate(outs, axis=1)
    out_ref[...] = ((on * gng_ref[...] + gnb_ref[...] + bonus) * g).astype(out_ref.dtype)

    @pl.when(c == nc - 1)
    def _():
        wkv_ref[...] = s_ref[...]


def _rwkv(rkv, low, shift0, wkv0, prm, tc):
    bsz, length, _ = rkv.shape
    b0 = shift0.shape[0]
    assert length % tc == 0 and b0 in (1, bsz)
    hps = RW_HEADS_PER_STEP
    w = hps * HEAD
    n_h = C_RW // HEAD
    ng = n_h // hps
    nblk = C_RW // w
    n_dbl = max((tc - 1).bit_length() - 1, 0)
    sb = (lambda b: b) if b0 == bsz else (lambda b: 0)
    seq = lambda off: pl.BlockSpec((None, tc, w), lambda b, g, c: (b, c, off * nblk + g))
    row0 = lambda off: pl.BlockSpec((None, 1, w), lambda b, g, c: (sb(b), 0, off * nblk + g))
    vec = lambda off: pl.BlockSpec((1, w), lambda b, g, c: (0, off * nblk + g))
    low_blk = (3 * C_RW) // LOW_PAD
    nbd = 2 * LANES
    bd = (jnp.arange(nbd)[:, None] // HEAD == jnp.arange(nbd)[None, :] // HEAD).astype(BF16)
    tri = (jnp.arange(tc)[None, :] <= jnp.arange(tc)[:, None]).astype(BF16)
    kernel = functools.partial(_rwkv_kernel, tc=tc, hps=hps, n_dbl=n_dbl)
    return pl.pallas_call(
        kernel,
        grid=(bsz, ng, length // tc),
        in_specs=[
            seq(0), seq(1), seq(2),
            pl.BlockSpec((None, tc, LOW_PAD), lambda b, g, c: (b, c, 0)),
            row0(0), row0(1), row0(2),
            pl.BlockSpec((None, 1, LOW_PAD), lambda b, g, c: (sb(b), 0, low_blk)),
            vec(0), vec(1), vec(2),
            pl.BlockSpec((1, LOW_PAD), lambda b, g, c: (0, low_blk)),
            vec(0), vec(0), vec(0), vec(0), vec(0), vec(0), vec(0),
            pl.BlockSpec((LANES, w), lambda b, g, c: (0, g)),
            pl.BlockSpec((LANES, w), lambda b, g, c: (0, g)),
            pl.BlockSpec((2 * LANES, w), lambda b, g, c: (0, g)),
            pl.BlockSpec((nbd, nbd), lambda b, g, c: (0, 0)),
            pl.BlockSpec((tc, tc), lambda b, g, c: (0, 0)),
            pl.BlockSpec((None, hps, HEAD, HEAD), lambda b, g, c: (sb(b), g, 0, 0)),
        ],
        out_specs=[
            pl.BlockSpec((None, tc, w), lambda b, g, c: (b, c, g)),
            pl.BlockSpec((None, hps, HEAD, HEAD), lambda b, g, c: (b, g, 0, 0)),
        ],
        out_shape=[
            jax.ShapeDtypeStruct((bsz, length, C_RW), BF16),
            jax.ShapeDtypeStruct((bsz, n_h, HEAD, HEAD), F32),
        ],
        scratch_shapes=[
            pltpu.VMEM((hps, HEAD, HEAD), F32),
            pltpu.VMEM((1, w), F32), pltpu.VMEM((1, w), F32), pltpu.VMEM((1, w), F32),
            pltpu.VMEM((1, LOW_PAD), F32),
        ],
        compiler_params=_cparams(("parallel", "parallel", "arbitrary")),
        name="rwkv",
    )(rkv, rkv, rkv, low, shift0, shift0, shift0, shift0,
      prm["mu"], prm["mu"], prm["mu"], prm["mu"],
      prm["w0"], prm["a0"], prm["k_k"], prm["k_a"], prm["r_k"], prm["gn_g"], prm["gn_b"],
      prm["w2"], prm["a2"], prm["g2"], bd, tri, wkv0)


def _attn_kernel(q_ref, k_ref, v_ref, kp_ref, vp_ref, tri_ref, trip_ref, g_ref, out_ref, *scratch,
                 tq, pb, n_pre, pipe_main, pipe_pre):
    i = pl.program_id(2)
    q0 = pl.multiple_of(i * tq, tq)
    lane = lax.broadcasted_iota(jnp.int32, (tq, LANES), 1)
    first = lane < HEAD
    q2 = q_ref[pl.ds(q0, tq), :] * (HEAD ** -0.5 * LOG2E)
    q_heads = (jnp.where(first, q2, 0.0).astype(BF16), jnp.where(first, 0.0, q2).astype(BF16))
    tri = tri_ref[...]
    trip = trip_ref[...]
    bufs, kmax_ref = scratch[:-1], scratch[-1]

    @pl.when(i == 0)
    def _():
        kmax_ref[...] = jnp.maximum(jnp.max(jnp.abs(k_ref[...]), axis=0, keepdims=True),
                                    jnp.max(jnp.abs(kp_ref[...]), axis=0, keepdims=True))

    reach_w = jnp.abs(q2) * kmax_ref[...] * 1.02
    zb_a = jnp.sum(jnp.where(first, reach_w, 0.0), -1, keepdims=True)
    z_bound = (zb_a, jnp.sum(reach_w, -1, keepdims=True) - zb_a)
    row = lax.broadcasted_iota(jnp.int32, (tq, tq), 0)
    colm = lax.broadcasted_iota(jnp.int32, (tq, tq), 1)
    causal = colm < row

    def suffix_sum(sp, tri_m):
        return _dot(sp.astype(BF16), tri_m)

    def block(qh, kb, vb, tri_m, carry, acc, mask):
        z = _dot_nt(qh, kb)
        sp = _softplus2(z)
        if mask is not None:
            sp = jnp.where(mask, sp, 0.0)
        suffix = suffix_sum(sp, tri_m)
        p = jnp.exp2((z + carry) - suffix)
        if mask is not None:
            p = jnp.where(mask, p, 0.0)
        acc = acc + _dot(p.astype(BF16), vb)
        return carry - suffix[:, 0:1], acc

    def piped(kr, vr, bs, tri_m, count, block_of, bufs, state, first_scored):
        z_s, d_s, tot_s = bufs
        start = 1 if first_scored else 0
        if not first_scored:
            z_s[1] = jnp.zeros_like(z_s[1])
        d_s[...] = jnp.zeros_like(d_s)
        tot_s[...] = jnp.zeros_like(tot_s)
        last = jnp.maximum(count - 1, 0)
        rows = lambda m: pl.ds(pl.multiple_of(block_of(jnp.clip(m, 0, last)) * bs, bs), bs)

        def stages(n, par, st):
            kb = kr[rows(n), :].astype(BF16)
            for h in range(2):
                z_s[par, h] = _dot_nt(q_heads[h], kb)
            live = jnp.logical_and(n >= 2, n - 2 < count)
            vb = vr[rows(n - 2), :].astype(BF16)
            for h in range(2):
                p = jnp.exp2(d_s[h] + jnp.where(live, st[2 * h], -1e30))
                st[2 * h + 1] = st[2 * h + 1] + _dot(p.astype(BF16), vb)
                st[2 * h] = st[2 * h] - jnp.where(live, tot_s[h], 0.0)
            for h in range(2):
                z = z_s[1 - par, h]
                suffix = suffix_sum(_softplus2(z), tri_m)
                d_s[h] = z - suffix
                tot_s[h] = suffix[:, 0:1]
            return st

        trips = (count + 3 - start) // 2

        def body(c):
            j, st = c[0], list(c[2:])
            for k in range(2):
                st = stages(start + 2 * j + k, (start + k) % 2, st)
            reach = jnp.maximum(jnp.max(st[0] + z_bound[0]), jnp.max(st[2] + z_bound[1]))
            return (j + 1, reach < EXP2_UNDERFLOW, *st)

        cond = lambda c: jnp.logical_and(c[0] < trips, jnp.logical_not(c[1]))
        out = lax.while_loop(cond, body, (jnp.int32(0), jnp.bool_(False), *state))
        return out[2:]

    state = [jnp.zeros((tq, 1), F32), jnp.zeros((tq, LANES), F32)] * 2
    kd = k_ref[pl.ds(q0, tq), :].astype(BF16)
    if pipe_main:
        for h in range(2):
            bufs[0][0, h] = jnp.where(causal, _dot_nt(q_heads[h], kd), -1e30)
        state = piped(k_ref, v_ref, tq, tri, i + 1, lambda m: i - m, bufs[0:3], state, True)
    else:
        vd = v_ref[pl.ds(q0, tq), :].astype(BF16)
        state[0:2] = block(q_heads[0], kd, vd, tri, state[0], state[1], causal)
        state[2:4] = block(q_heads[1], kd, vd, tri, state[2], state[3], causal)

    if pipe_pre:
        state = piped(kp_ref, vp_ref, pb, trip, n_pre, lambda m: n_pre - 1 - m, bufs[-3:], state, False)
    else:
        for t in range(n_pre):
            rows = pl.ds((n_pre - 1 - t) * pb, pb)
            kb = kp_ref[rows, :].astype(BF16)
            vb = vp_ref[rows, :].astype(BF16)
            state = list(state)
            state[0:2] = block(q_heads[0], kb, vb, trip, state[0], state[1], None)
            state[2:4] = block(q_heads[1], kb, vb, trip, state[2], state[3], None)

    o = jnp.where(first, state[1], state[3])
    sq = o * o
    s_a = jnp.sum(jnp.where(first, sq, 0.0), -1, keepdims=True)
    s_b = jnp.sum(sq, -1, keepdims=True) - s_a
    inv = jnp.where(first, lax.rsqrt(s_a / HEAD + RMS_EPS), lax.rsqrt(s_b / HEAD + RMS_EPS))
    out_ref[...] = (o * inv * g_ref[...]).astype(out_ref.dtype)


def _attention(q, k, v, kp, vp, gain, tq, pb):
    bsz, length, c = q.shape
    b0, plen, _ = kp.shape
    assert length % tq == 0 and plen % pb == 0 and b0 in (1, bsz)
    sb = (lambda b: b) if b0 == bsz else (lambda b: 0)
    n_pre = plen // pb
    n_main = length // tq
    pipe_main = n_main > 1
    pipe_pre = n_pre >= 3
    tri1 = lambda n: (jnp.arange(n)[:, None] >= jnp.arange(n)[None, :]).astype(BF16)
    stage_bufs = lambda bs: [pltpu.VMEM((2, 2, tq, bs), F32), pltpu.VMEM((2, tq, bs), F32),
                             pltpu.VMEM((2, tq, 1), F32)]
    full = pl.BlockSpec((None, length, LANES), lambda b, h, i: (b, 0, h))
    pre = pl.BlockSpec((None, plen, LANES), lambda b, h, i: (sb(b), 0, h))
    kernel = functools.partial(_attn_kernel, tq=tq, pb=pb, n_pre=n_pre, pipe_main=pipe_main, pipe_pre=pipe_pre)
    return pl.pallas_call(
        kernel,
        grid=(bsz, c // LANES, n_main),
        in_specs=[
            full, full, full, pre, pre,
            pl.BlockSpec((tq, tq), lambda b, h, i: (0, 0)),
            pl.BlockSpec((pb, pb), lambda b, h, i: (0, 0)),
            pl.BlockSpec((1, LANES), lambda b, h, i: (0, h)),
        ],
        out_specs=pl.BlockSpec((None, tq, LANES), lambda b, h, i: (b, i, h)),
        out_shape=jax.ShapeDtypeStruct((bsz, length, c), BF16),
        scratch_shapes=((stage_bufs(tq) if pipe_main else []) + (stage_bufs(pb) if pipe_pre else [])
                        + [pltpu.VMEM((1, LANES), F32)]),
        compiler_params=_cparams(("parallel", "parallel", "arbitrary")),
        name="attn",
    )(q, k, v, kp, vp, tri1(tq), tri1(pb), gain)


def _route_columns(logits_t, bias):
    ne, n = logits_t.shape
    gsz = ne // N_GROUPS
    neg = -jnp.inf
    scores = _sigmoid(logits_t)
    sel = scores + bias
    ig = lax.broadcasted_iota(jnp.int32, (gsz, n), 0)
    gs = []
    for g in range(N_GROUPS):
        xg = sel[g * gsz:(g + 1) * gsz, :]
        m1 = jnp.max(xg, axis=0, keepdims=True)
        first = jnp.min(jnp.where(xg == m1, ig, gsz), axis=0, keepdims=True)
        m2 = jnp.max(jnp.where(ig == first, neg, xg), axis=0, keepdims=True)
        gs.append(m1 + m2)
    kept = []
    for g in range(N_GROUPS):
        rank = jnp.zeros((1, n), jnp.int32)
        for g2 in range(N_GROUPS):
            if g2 == g:
                continue
            ahead = (gs[g2] >= gs[g]) if g2 < g else (gs[g2] > gs[g])
            rank = rank + ahead.astype(jnp.int32)
        kept.append(jnp.where(rank < TOPK_GROUPS, sel[g * gsz:(g + 1) * gsz, :], neg))
    cand = jnp.concatenate(kept, axis=0)
    ie = lax.broadcasted_iota(jnp.int32, (ne, n), 0)
    ids, gates = [], []
    chosen = jnp.zeros((ne, n), F32)
    for _ in range(TOP_K):
        m = jnp.max(cand, axis=0, keepdims=True)
        idx = jnp.min(jnp.where(cand == m, ie, ne), axis=0, keepdims=True)
        hit = ie == idx
        gates.append(jnp.sum(jnp.where(hit, scores, 0.0), axis=0, keepdims=True))
        ids.append(idx)
        cand = jnp.where(hit, neg, cand)
        chosen = chosen + jnp.where(hit, 1.0, 0.0)
    gate = jnp.concatenate(gates, axis=0)
    gate = gate / jnp.sum(gate, axis=0, keepdims=True) * ROUTED_SCALE
    return jnp.concatenate(ids, axis=0), gate, jnp.sum(chosen, axis=1, keepdims=True)


def _out_proj_kernel(x_ref, rw_ref, sb_ref, wa_ref, wb_ref, g0_ref, b0_ref, g1_ref, b1_ref,
                     wrh_ref, wrl_ref, rb_ref, h_ref, hb_ref, idx_ref, gate_ref, cnt_ref):
    xn = _layer_norm(x_ref[...], g0_ref[...], b0_ref[...])
    mix = _dot(rw_ref[...], wa_ref[...]) + _dot(sb_ref[...], wb_ref[...])
    h = _layer_norm(DN_ALPHA * xn + mix, g1_ref[...], b1_ref[...])
    h_ref[...] = h
    hi, lo = _split2(h)
    hb_ref[...] = hi
    wrh = wrh_ref[...]
    logits_t = _dot_nt(wrh, hi) + _dot_nt(wrh, lo) + _dot_nt(wrl_ref[...], hi)
    idx, gate, cnt = _route_columns(logits_t, rb_ref[...])
    idx_ref[...] = idx
    gate_ref[...] = gate
    cnt_ref[...] = cnt


def _out_proj(x, rw, sbo, wa, wb, g0, b0, g1, b1, wrh, wrl, rbias, tm):
    t, d = x.shape
    ne = wrh.shape[0]
    assert t % tm == 0
    row = lambda w: pl.BlockSpec((tm, w), lambda i: (i, 0))
    col = pl.BlockSpec((TOP_K, tm), lambda i: (0, i))
    const = lambda a: pl.BlockSpec(a.shape, lambda i: (0, 0))
    return pl.pallas_call(
        _out_proj_kernel,
        grid=(t // tm,),
        in_specs=[row(d), row(C_RW), row(C_SB), const(wa), const(wb), const(g0), const(b0), const(g1), const(b1),
                  const(wrh), const(wrl), const(rbias)],
        out_specs=[row(d), row(d), col, col, pl.BlockSpec((None, ne, 1), lambda i: (i, 0, 0))],
        out_shape=[jax.ShapeDtypeStruct((t, d), F32), jax.ShapeDtypeStruct((t, d), BF16),
                   jax.ShapeDtypeStruct((TOP_K, t), jnp.int32), jax.ShapeDtypeStruct((TOP_K, t), F32),
                   jax.ShapeDtypeStruct((t // tm, ne, 1), F32)],
        compiler_params=_cparams(("parallel",)),
        name="out_proj",
    )(x, rw, sbo, wa, wb, g0, b0, g1, b1, wrh, wrl, rbias)


def _expert_kernel(be_ref, nv_ref, tok_ref, nxt_ref, rw_ref, h_hbm, wg_ref, wu_ref, wd_ref, out_ref,
                   xbuf, sems, xb_ref, wgb_ref, wub_ref, wdb_ref, prev_ref):
    i = pl.program_id(0)
    nv = nv_ref[0]
    e = be_ref[i]
    slot = lax.rem(i, 2)
    blk = xbuf.shape[1]

    def row_copy(src_row, r, s):
        return pltpu.make_async_copy(h_hbm.at[pl.ds(src_row, 1), :], xbuf.at[s, pl.ds(r, 1), :], sems.at[s])

    @pl.when(i == 0)
    def _():
        prev_ref[0] = -1
        for r in range(blk):
            row_copy(tok_ref[0, r], r, 0).start(priority=GATHER_DMA_PRIORITY)

    @pl.when(i < nv)
    def _():
        @pl.when(e != prev_ref[0])
        def _():
            wgb_ref[...] = wg_ref[...].astype(BF16)
            wub_ref[...] = wu_ref[...].astype(BF16)
            wdb_ref[...] = wd_ref[...].astype(BF16)
            prev_ref[0] = e

        for r in range(blk):
            row_copy(0, r, slot).wait()
        xb_ref[...] = xbuf[slot].astype(BF16)
        for r in range(blk):
            row_copy(nxt_ref[0, r], r, 1 - slot).start(priority=GATHER_DMA_PRIORITY)
        x = xb_ref[...]
        hg = _dot(x, wgb_ref[...])
        hu = _dot(x, wub_ref[...])
        hid = (hg * _sigmoid(hg) * hu).astype(BF16)
        out_ref[...] = (_dot(hid, wdb_ref[...]) * rw_ref[...]).astype(out_ref.dtype)

        @pl.when(i == nv - 1)
        def _():
            for r in range(blk):
                row_copy(0, r, 1 - slot).wait()

    @pl.when(i >= nv)
    def _():
        out_ref[...] = jnp.zeros_like(out_ref)


def _experts(h, row_tok, row_w, blk_exp, n_valid, w_gate, w_up, w_down):
    _, d = h.shape
    _, _, de = w_gate.shape
    blk = EXPERT_BLOCK
    nb = row_tok.shape[0] // blk
    tok3 = row_tok.reshape(nb, 1, blk)
    live = lambda i, nv: jnp.minimum(i, nv[0] - 1)
    idx_spec = lambda step: pl.BlockSpec((None, 1, blk), lambda i, be, nv: (live(i + step, nv), 0, 0),
                                         memory_space=pltpu.SMEM)
    grid_spec = pltpu.PrefetchScalarGridSpec(
        num_scalar_prefetch=2,
        grid=(nb,),
        in_specs=[
            idx_spec(0), idx_spec(1),
            pl.BlockSpec((blk, 1), lambda i, be, nv: (live(i, nv), 0)),
            pl.BlockSpec(memory_space=pl.ANY),
            pl.BlockSpec((None, d, de), lambda i, be, nv: (be[i], 0, 0)),
            pl.BlockSpec((None, d, de), lambda i, be, nv: (be[i], 0, 0)),
            pl.BlockSpec((None, de, d), lambda i, be, nv: (be[i], 0, 0)),
        ],
        out_specs=pl.BlockSpec((blk, d), lambda i, be, nv: (i, 0)),
        scratch_shapes=[pltpu.VMEM((2, blk, d), F32), pltpu.SemaphoreType.DMA((2,)), pltpu.VMEM((blk, d), BF16),
                        pltpu.VMEM((d, de), BF16), pltpu.VMEM((d, de), BF16), pltpu.VMEM((de, d), BF16),
                        pltpu.SMEM((1,), jnp.int32)],
    )
    return pl.pallas_call(
        _expert_kernel,
        grid_spec=grid_spec,
        out_shape=jax.ShapeDtypeStruct((nb * blk, d), BF16),
        compiler_params=_cparams(("arbitrary",)),
        name="experts",
    )(blk_exp, n_valid, tok3, tok3, row_w, h, w_gate, w_up, w_down)


def _final_kernel(h_ref, hb_ref, routed_ref, wg_ref, wu_ref, wd_ref, g_ref, b_ref, out_ref):
    x = hb_ref[...]
    hg = _dot(x, wg_ref[...])
    hu = _dot(x, wu_ref[...])
    shared = _dot((hg * _sigmoid(hg) * hu).astype(BF16), wd_ref[...])
    y = DN_ALPHA * h_ref[...] + routed_ref[...] + shared
    out_ref[...] = _layer_norm(y, g_ref[...], b_ref[...])


def _final(h, hb, routed, wg, wu, wd, g, b, tm):
    t, d = h.shape
    assert t % tm == 0
    row = pl.BlockSpec((tm, d), lambda i: (i, 0))
    const = lambda a: pl.BlockSpec(a.shape, lambda i: (0, 0))
    return pl.pallas_call(
        _final_kernel,
        grid=(t // tm,),
        in_specs=[row, row, row, const(wg), const(wu), const(wd), const(g), const(b)],
        out_specs=row,
        out_shape=jax.ShapeDtypeStruct((t, d), F32),
        compiler_params=_cparams(("parallel",)),
        name="final",
    )(h, hb, routed, wg, wu, wd, g, b)


def _dispatch_plan(idx_t, gate_t, counts):
    ne = counts.shape[0]
    t = idx_t.shape[1]
    blk = EXPERT_BLOCK
    n_assign = t * TOP_K
    nb = n_assign // blk + ne
    n_rows = nb * blk
    n_fill = n_rows - n_assign
    e_flat = idx_t.reshape(n_assign)
    iota = jnp.arange(n_assign, dtype=jnp.int32)
    padded = (counts + blk - 1) // blk * blk
    n_valid = (jnp.sum(padded) // blk).astype(jnp.int32).reshape(1)
    fill_end = jnp.cumsum(padded - counts)
    fill_e = jnp.sum((fill_end[None, :] <= jnp.arange(n_fill, dtype=jnp.int32)[:, None]).astype(jnp.int32), axis=1)
    zeros = jnp.zeros((n_fill,), jnp.int32)
    keys = jnp.concatenate([2 * e_flat, 2 * fill_e + 1])
    toks = jnp.concatenate([iota % t, zeros])
    wts = jnp.concatenate([gate_t.reshape(n_assign), zeros.astype(F32)])
    aid = jnp.concatenate([iota, zeros + n_assign])
    keys, row_tok, row_w, row_aid = lax.sort((keys, toks, wts, aid), num_keys=1)
    blk_exp = jnp.minimum(keys[::blk] // 2, ne - 1)
    _, pos = lax.sort((row_aid, jnp.arange(n_rows, dtype=jnp.int32)), num_keys=1)
    return row_tok, row_w.reshape(n_rows, 1), blk_exp, n_valid, pos[:n_assign].reshape(TOP_K, t)


def _pick_tile(n, cap):
    t = cap
    while n % t:
        t //= 2
    return t


def kernel(x_prompt, x_sample, cache_sb_k, cache_sb_v, state_rwkv_wkv, state_rwkv_shift, meta_tokens, ln0_g, ln0_b, w_in, rw_mu, rw_w0, rw_w2, rw_a0, rw_a2, rw_g2, rw_k_k, rw_k_a, rw_r_k, rw_gn_g, rw_gn_b, sb_norm_g, w_out, ln1_g, ln1_b, w_router, router_bias, w_exp_gate, w_exp_up, w_exp_down, w_sh_gate, w_sh_up, w_sh_down, ln2_g, ln2_b):
    bsz, seq, d = x_prompt.shape
    dbs, dseq, _ = x_sample.shape
    past = cache_sb_k.shape[2]
    n_h = C_RW // HEAD
    ne = w_router.shape[-1]
    rw_cols = 3 * C_RW + D_DECAY + D_AAA + D_GATE
    o_wd, o_ad, o_gd = 3 * C_RW, 3 * C_RW + D_DECAY, 3 * C_RW + D_DECAY + D_AAA

    def pad_low(a):
        z = lambda n: jnp.zeros(a.shape[:-1] + (n,), a.dtype)
        return jnp.concatenate([a[..., :o_wd], a[..., o_wd:o_ad], z(LANES - D_DECAY), a[..., o_ad:o_gd],
                                z(LANES - D_AAA), a[..., o_gd:rw_cols], z(2 * LANES - D_GATE)], -1)

    def unpad_low(rkv_row, low_row):
        return jnp.concatenate([rkv_row, low_row[..., 0:D_DECAY], low_row[..., LANES:LANES + D_AAA],
                                low_row[..., 2 * LANES:2 * LANES + D_GATE]], -1)

    wi = w_in[0]
    w_rw = pad_low(wi[:, :rw_cols])
    w6 = jnp.concatenate([w_rw[:, :3 * C_RW], wi[:, rw_cols:]], 1).astype(BF16)
    wlow = w_rw[:, 3 * C_RW:].astype(BF16)
    pad_rows = lambda a, n: jnp.concatenate([a, jnp.zeros((n - a.shape[0], a.shape[1]), a.dtype)], 0)
    row = lambda a: a.reshape(1, -1)
    prm = {
        "mu": pad_low(rw_mu[0]).reshape(1, -1),
        "w0": row(rw_w0[0]), "a0": row(rw_a0[0]), "k_k": row(rw_k_k[0]), "k_a": row(rw_k_a[0]),
        "r_k": row(rw_r_k[0]), "gn_g": row(rw_gn_g[0]), "gn_b": row(rw_gn_b[0]),
        "w2": pad_rows(rw_w2[0], LANES), "a2": pad_rows(rw_a2[0], LANES), "g2": pad_rows(rw_g2[0], 2 * LANES),
    }
    g0, b0 = row(ln0_g), row(ln0_b)
    g1, b1 = row(ln1_g[0]), row(ln1_b[0])
    g2, b2 = row(ln2_g[0]), row(ln2_b[0])
    sb_gain = row(sb_norm_g[0])
    wo = w_out[0].astype(BF16)
    wo_a, wo_b = wo[:C_RW], wo[C_RW:]
    wr_hi, wr_lo = _split2(w_router[0].T)
    rbias = router_bias[0].astype(F32).reshape(ne, 1)
    wsg, wsu, wsd = w_sh_gate[0].astype(BF16), w_sh_up[0].astype(BF16), w_sh_down[0].astype(BF16)

    xp = x_prompt.reshape(bsz * seq, d)
    xs = x_sample.reshape(dbs * dseq, d)
    xm = meta_tokens.astype(x_prompt.dtype)
    rkv_m, _, k_m, v_m, low_m = _in_proj(xm, g0, b0, w6, wlow, N_META)
    rkv_p, q_p, k_p, v_p, low_p = _in_proj(xp, g0, b0, w6, wlow, _pick_tile(bsz * seq, 512))
    rkv_s, q_s, k_s, v_s, low_s = _in_proj(xs, g0, b0, w6, wlow, _pick_tile(dbs * dseq, 512))

    zero_shift = jnp.zeros((1, 1, 3 * C_RW + LOW_PAD), F32)
    zero_wkv = jnp.zeros((1, n_h, HEAD, HEAD), F32)
    _, wkv_m = _rwkv(rkv_m[None], low_m[None], zero_shift, zero_wkv, prm, N_META)
    shift_m = jnp.concatenate([rkv_m[-1:], low_m[-1:]], -1)[None]
    rw_p, wkv_p = _rwkv(rkv_p.reshape(bsz, seq, -1), low_p.reshape(bsz, seq, -1), shift_m, wkv_m, prm,
                        _pick_tile(seq, 64))
    shift_s0 = pad_low(state_rwkv_shift[0])
    rw_s, wkv_s = _rwkv(rkv_s.reshape(dbs, dseq, -1), low_s.reshape(dbs, dseq, -1), shift_s0,
                        state_rwkv_wkv[0], prm, dseq)

    c3 = lambda a, b_, l_: a.reshape(b_, l_, C_SB)
    sb_p = _attention(c3(q_p, bsz, seq), c3(k_p, bsz, seq), c3(v_p, bsz, seq), k_m[None], v_m[None], sb_gain,
                      _pick_tile(seq, 256), N_META)
    sb_s = _attention(c3(q_s, dbs, dseq), c3(k_s, dbs, dseq), c3(v_s, dbs, dseq),
                      cache_sb_k[0].reshape(dbs, past, C_SB), cache_sb_v[0].reshape(dbs, past, C_SB), sb_gain,
                      dseq, _pick_tile(past, 256))

    x_all = jnp.concatenate([xp, xs], 0)
    rw_all = jnp.concatenate([rw_p.reshape(bsz * seq, C_RW), rw_s.reshape(dbs * dseq, C_RW)], 0)
    sb_all = jnp.concatenate([sb_p.reshape(bsz * seq, C_SB), sb_s.reshape(dbs * dseq, C_SB)], 0)
    t_all = x_all.shape[0]
    tm = _pick_tile(t_all, 256)
    h, hb, idx_t, gate_t, cnt = _out_proj(x_all, rw_all, sb_all, wo_a, wo_b, g0, b0, g1, b1, wr_hi, wr_lo, rbias, tm)

    counts = jnp.sum(cnt[:, :, 0], axis=0).astype(jnp.int32)
    row_tok, row_w, blk_exp, n_valid, pos = _dispatch_plan(idx_t, gate_t, counts)
    expert_out = _experts(h, row_tok, row_w, blk_exp, n_valid, w_exp_gate[0], w_exp_up[0], w_exp_down[0])
    routed = jnp.sum(jnp.take(expert_out, pos, axis=0).astype(F32), axis=0)
    y = _final(h, hb, routed, wsg, wsu, wsd, g2, b2, tm)

    y_prompt = y[:bsz * seq].reshape(bsz, seq, d)
    y_sample = y[bsz * seq:].reshape(dbs, dseq, d)
    heads = lambda a, b_, l_: a.reshape(b_, l_, C_SB // HEAD, HEAD)

    def with_meta(m, p):
        m4 = jnp.broadcast_to(heads(m, 1, N_META), (bsz, N_META, C_SB // HEAD, HEAD))
        return jnp.concatenate([m4, heads(p, bsz, seq)], 1)[None]

    k_prompt = with_meta(k_m, k_p)
    v_prompt = with_meta(v_m, v_p)
    last = lambda a, b_, l_: a.reshape(b_, l_, -1)[:, -1:]
    shift_prompt = unpad_low(last(rkv_p, bsz, seq), last(low_p, bsz, seq))[None]
    shift_sample = unpad_low(last(rkv_s, dbs, dseq), last(low_s, dbs, dseq))[None]
    return (y_prompt, y_sample, k_prompt, v_prompt, wkv_p[None], shift_prompt,
            heads(k_s, dbs, dseq)[None], heads(v_s, dbs, dseq)[None], wkv_s[None], shift_sample)
```

```python
import functools

import jax
import jax.numpy as jnp
from jax import lax
from jax.experimental import pallas as pl
from jax.experimental.pallas import tpu as pltpu

F32 = jnp.float32
BF16 = jnp.bfloat16
HIGHEST = lax.Precision.HIGHEST

N_META = 16
HEAD = 64
C_RW = 1024
C_SB = 1024
D_DECAY = 64
D_AAA = 64
D_GATE = 160
LOW_PAD = 512
TOP_K = 8
N_GROUPS = 8
TOPK_GROUPS = 4
ROUTED_SCALE = 2.5
LN_EPS = 1e-5
GN_EPS = 64e-5
RMS_EPS = 1e-6
DEPTH = 1
DN_ALPHA = (2 * DEPTH) ** 0.25
LOG2E = 1.4426950408889634
EXP2_UNDERFLOW = -170.0

LANES = 128
VMEM_LIMIT = 56 * 1024 * 1024
RW_HEADS_PER_STEP = 16
EXPERT_BLOCK = 256
GATHER_DMA_PRIORITY = 1


def _cparams(sem):
    return pltpu.CompilerParams(dimension_semantics=sem, vmem_limit_bytes=VMEM_LIMIT)


def _layer_norm(x, g, b):
    mu = jnp.mean(x, -1, keepdims=True)
    xc = x - mu
    var = jnp.mean(xc * xc, -1, keepdims=True)
    return xc * lax.rsqrt(var + LN_EPS) * g + b


def _sigmoid(x):
    return 1.0 / (1.0 + jnp.exp(-x))


def _softplus(x):
    return jnp.maximum(x, 0.0) + jnp.log(1.0 + jnp.exp(-jnp.abs(x)))


def _softplus2(x2):
    neg_abs = lax.bitcast_convert_type(lax.bitcast_convert_type(x2, jnp.uint32) | jnp.uint32(0x80000000), F32)
    return jnp.maximum(x2, 0.0) + jnp.log(1.0 + jnp.exp2(neg_abs)) * LOG2E


def _dot(a, b, precision=None):
    return jnp.dot(a, b, preferred_element_type=F32, precision=precision)


def _dot_nt(a, b, precision=None):
    return lax.dot_general(a, b, (((1,), (1,)), ((), ())), preferred_element_type=F32, precision=precision)


def _dot_tn(a, b, precision=None):
    return lax.dot_general(a, b, (((0,), (0,)), ((), ())), preferred_element_type=F32, precision=precision)


def _split2(x):
    hi = x.astype(BF16)
    return hi, (x - hi.astype(F32)).astype(BF16)


def _split3(x):
    hi = x.astype(BF16)
    r1 = x - hi.astype(F32)
    mid = r1.astype(BF16)
    return hi, mid, (r1 - mid.astype(F32)).astype(BF16)


def _dot_x3(a, b, dot=_dot):
    ah, al = _split2(a)
    bh, bl = _split2(b)
    return dot(ah, bh) + dot(al, bh) + dot(ah, bl)


def _dot_bf(a, b, dot=_dot):
    return dot(a.astype(BF16), b.astype(BF16))


RW_PREC = {"tri": "bf", "dbl": "bf", "wu": "bf", "ro": "bf", "pq": "bf", "state": "x3"}


def _mm(site, a, b, dot=_dot):
    mode = RW_PREC[site]
    if mode == "bf":
        return _dot_bf(a, b, dot)
    if mode == "x3":
        return _dot_x3(a, b, dot)
    return dot(a, b, HIGHEST)


def _in_proj_kernel(x_ref, g_ref, b_ref, w_ref, wlow_ref, rkv_ref, q_ref, k_ref, v_ref, low_ref, xn_ref):
    j = pl.program_id(1)

    @pl.when(j == 0)
    def _():
        xn = _layer_norm(x_ref[...], g_ref[...], b_ref[...]).astype(BF16)
        xn_ref[...] = xn
        low_ref[...] = _dot(xn, wlow_ref[...])

    y = _dot(xn_ref[...], w_ref[...])

    @pl.when(j < 3)
    def _():
        rkv_ref[...] = y

    @pl.when(j == 3)
    def _():
        q_ref[...] = y

    @pl.when(j == 4)
    def _():
        k_ref[...] = y

    @pl.when(j == 5)
    def _():
        v_ref[...] = y


def _in_proj(x, ln_g, ln_b, w6, wlow, tm):
    t, d = x.shape
    assert t % tm == 0
    c = C_RW
    grid = (t // tm, 6)
    col = lambda i, j: (i, 0)
    return pl.pallas_call(
        _in_proj_kernel,
        grid=grid,
        in_specs=[
            pl.BlockSpec((tm, d), col),
            pl.BlockSpec((1, d), lambda i, j: (0, 0)),
            pl.BlockSpec((1, d), lambda i, j: (0, 0)),
            pl.BlockSpec((d, c), lambda i, j: (0, j)),
            pl.BlockSpec((d, LOW_PAD), lambda i, j: (0, 0)),
        ],
        out_specs=[
            pl.BlockSpec((tm, c), lambda i, j: (i, jnp.minimum(j, 2))),
            pl.BlockSpec((tm, c), col),
            pl.BlockSpec((tm, c), col),
            pl.BlockSpec((tm, c), col),
            pl.BlockSpec((tm, LOW_PAD), col),
        ],
        out_shape=[
            jax.ShapeDtypeStruct((t, 3 * c), F32),
            jax.ShapeDtypeStruct((t, c), F32),
            jax.ShapeDtypeStruct((t, c), F32),
            jax.ShapeDtypeStruct((t, c), F32),
            jax.ShapeDtypeStruct((t, LOW_PAD), F32),
        ],
        scratch_shapes=[pltpu.VMEM((tm, d), BF16)],
        compiler_params=_cparams(("parallel", "arbitrary")),
        name="in_proj",
    )(x, ln_g, ln_b, w6, wlow)


def _rwkv_kernel(r_ref, k_ref, v_ref, low_ref, sr_ref, sk_ref, sv_ref, slow_ref,
                 mur_ref, muk_ref, muv_ref, mulow_ref,
                 w0_ref, a0_ref, kk_ref, ka_ref, rk_ref, gng_ref, gnb_ref,
                 w2_ref, a2_ref, g2_ref, bd_ref, tri_ref, wkv0_ref,
                 out_ref, wkv_ref,
                 s_ref, pr_ref, pk_ref, pv_ref, plow_ref, *, tc, hps, n_dbl):
    c = pl.program_id(2)
    nc = pl.num_programs(2)

    @pl.when(c == 0)
    def _():
        s_ref[...] = wkv0_ref[...]
        pr_ref[...] = sr_ref[...]
        pk_ref[...] = sk_ref[...]
        pv_ref[...] = sv_ref[...]
        plow_ref[...] = slow_ref[...]

    def token_shift(cur_ref, prev_ref, mu_ref):
        cur = cur_ref[...]
        row = lax.broadcasted_iota(jnp.int32, cur.shape, 0)
        prev = jnp.where(row == 0, prev_ref[...], pltpu.roll(cur, 1, 0))
        prev_ref[...] = cur[tc - 1:tc, :]
        return cur + (prev - cur) * mu_ref[...]

    r = token_shift(r_ref, pr_ref, mur_ref)
    k = token_shift(k_ref, pk_ref, muk_ref)
    v = token_shift(v_ref, pv_ref, muv_ref)
    xl = token_shift(low_ref, plow_ref, mulow_ref)
    wd = xl[:, 0:LANES]
    ad = xl[:, LANES:2 * LANES]
    gd = xl[:, 2 * LANES:LOW_PAD]

    w_log = -_softplus(-(w0_ref[...] + _dot_x3(jnp.tanh(wd), w2_ref[...]))) - 0.5
    lw = -jnp.exp(w_log)
    a = _sigmoid(a0_ref[...] + _dot_x3(ad, a2_ref[...]))
    g = _dot_x3(_sigmoid(gd), g2_ref[...])

    bd = bd_ref[...]
    nb_lane = bd.shape[0]

    def head_sum(x):
        parts = []
        for j in range(x.shape[1] // nb_lane):
            hi, lo = _split2(x[:, j * nb_lane:(j + 1) * nb_lane])
            parts.append(_dot(hi, bd) + _dot(lo, bd))
        return parts[0] if len(parts) == 1 else jnp.concatenate(parts, axis=1)

    kk = k * kk_ref[...]
    kk = kk / jnp.maximum(jnp.sqrt(head_sum(kk * kk)), 1e-12)
    k2 = k * (1.0 + (a - 1.0) * ka_ref[...])
    bonus = head_sum(r * k2 * rk_ref[...]) * v

    ti = lax.broadcasted_iota(jnp.int32, (tc, tc), 0)
    si = lax.broadcasted_iota(jnp.int32, (tc, tc), 1)
    incl = si <= ti
    strict = si < ti
    tri = tri_ref[...]
    l1, l2, l3 = _split3(lw)
    cs = _dot(tri, l1) + _dot(tri, l2) + _dot(tri, l3)
    c_last = cs[tc - 1:tc, :]
    e_pos = jnp.exp(cs)
    e_neg = jnp.exp(-cs)
    e_end = jnp.exp(c_last - cs)
    b = kk * a
    r_t = r * e_pos
    al_t = -kk * jnp.exp(cs - lw)
    be_t = b * e_neg
    k_t = k2 * e_neg
    be_h = b * e_end
    k_h = k2 * e_end
    g_last = jnp.exp(c_last)

    eye_t = (si == ti).astype(F32)
    di = lax.broadcasted_iota(jnp.int32, (HEAD, HEAD), 0)
    dj = lax.broadcasted_iota(jnp.int32, (HEAD, HEAD), 1)
    eye_h = di == dj

    hs = range(hps)
    cut = lambda x: [x[:, h * HEAD:(h + 1) * HEAD] for h in hs]
    al_h, r_h, v_h, bet, kt, beh, kh, gl = (cut(x) for x in (al_t, r_t, v, be_t, k_t, be_h, k_h, g_last))
    a_mat = [jnp.where(strict, _mm("tri", al_h[h], bet[h], _dot_nt), 0.0) for h in hs]
    b_mat = [jnp.where(strict, _mm("tri", al_h[h], kt[h], _dot_nt), 0.0) for h in hs]
    ar_mat = [jnp.where(incl, _mm("tri", r_h[h], bet[h], _dot_nt), 0.0) for h in hs]
    br_mat = [jnp.where(incl, _mm("tri", r_h[h], kt[h], _dot_nt), 0.0) for h in hs]
    bv = [_mm("wu", b_mat[h], v_h[h]) for h in hs]
    m = [eye_t + a_mat[h] for h in hs]
    pw = a_mat
    for _ in range(n_dbl):
        pw = [_mm("dbl", pw[h], pw[h]) for h in hs]
        m = [m[h] + _mm("dbl", m[h], pw[h]) for h in hs]
    w_m = [_mm("wu", m[h], al_h[h]) for h in hs]
    u_loc = [_mm("wu", m[h], bv[h]) for h in hs]
    r_hat = [r_h[h] + _mm("ro", ar_mat[h], w_m[h]) for h in hs]
    o_loc = [_mm("ro", ar_mat[h], u_loc[h]) + _mm("ro", br_mat[h], v_h[h]) for h in hs]
    p_mat = [_mm("pq", w_m[h], beh[h], _dot_tn) + jnp.where(eye_h, gl[h], 0.0) for h in hs]
    q_mat = [_mm("pq", u_loc[h], beh[h], _dot_tn) + _mm("pq", v_h[h], kh[h], _dot_tn) for h in hs]
    s0 = [s_ref[h] for h in hs]
    o = [o_loc[h] + _mm("state", r_hat[h], s0[h], _dot_nt) for h in hs]
    for h in hs:
        s_ref[h] = _mm("state", s0[h], p_mat[h]) + q_mat[h]
    outs = []
    for h in hs:
        mu = jnp.mean(o[h], -1, keepdims=True)
        oc = o[h] - mu
        var = jnp.mean(oc * oc, -1, keepdims=True)
        outs.append(oc * lax.rsqrt(var + GN_EPS))
    on = jnp.concatenate(outs, axis=1)
    out_ref[...] = ((on * gng_ref[...] + gnb_ref[...] + bonus) * g).astype(out_ref.dtype)

    @pl.when(c == nc - 1)
    def _():
        wkv_ref[...] = s_ref[...]


def _rwkv(rkv, low, shift0, wkv0, prm, tc):
    bsz, length, _ = rkv.shape
    b0 = shift0.shape[0]
    assert length % tc == 0 and b0 in (1, bsz)
    hps = RW_HEADS_PER_STEP
    w = hps * HEAD
    n_h = C_RW // HEAD
    ng = n_h // hps
    nblk = C_RW // w
    n_dbl = max((tc - 1).bit_length() - 1, 0)
    sb = (lambda b: b) if b0 == bsz else (lambda b: 0)
    seq = lambda off: pl.BlockSpec((None, tc, w), lambda b, g, c: (b, c, off * nblk + g))
    row0 = lambda off: pl.BlockSpec((None, 1, w), lambda b, g, c: (sb(b), 0, off * nblk + g))
    vec = lambda off: pl.BlockSpec((1, w), lambda b, g, c: (0, off * nblk + g))
    low_blk = (3 * C_RW) // LOW_PAD
    nbd = 2 * LANES
    bd = (jnp.arange(nbd)[:, None] // HEAD == jnp.arange(nbd)[None, :] // HEAD).astype(BF16)
    tri = (jnp.arange(tc)[None, :] <= jnp.arange(tc)[:, None]).astype(BF16)
    kernel = functools.partial(_rwkv_kernel, tc=tc, hps=hps, n_dbl=n_dbl)
    return pl.pallas_call(
        kernel,
        grid=(bsz, ng, length // tc),
        in_specs=[
            seq(0), seq(1), seq(2),
            pl.BlockSpec((None, tc, LOW_PAD), lambda b, g, c: (b, c, 0)),
            row0(0), row0(1), row0(2),
            pl.BlockSpec((None, 1, LOW_PAD), lambda b, g, c: (sb(b), 0, low_blk)),
            vec(0), vec(1), vec(2),
            pl.BlockSpec((1, LOW_PAD), lambda b, g, c: (0, low_blk)),
            vec(0), vec(0), vec(0), vec(0), vec(0), vec(0), vec(0),
            pl.BlockSpec((LANES, w), lambda b, g, c: (0, g)),
            pl.BlockSpec((LANES, w), lambda b, g, c: (0, g)),
            pl.BlockSpec((2 * LANES, w), lambda b, g, c: (0, g)),
            pl.BlockSpec((nbd, nbd), lambda b, g, c: (0, 0)),
            pl.BlockSpec((tc, tc), lambda b, g, c: (0, 0)),
            pl.BlockSpec((None, hps, HEAD, HEAD), lambda b, g, c: (sb(b), g, 0, 0)),
        ],
        out_specs=[
            pl.BlockSpec((None, tc, w), lambda b, g, c: (b, c, g)),
            pl.BlockSpec((None, hps, HEAD, HEAD), lambda b, g, c: (b, g, 0, 0)),
        ],
        out_shape=[
            jax.ShapeDtypeStruct((bsz, length, C_RW), BF16),
            jax.ShapeDtypeStruct((bsz, n_h, HEAD, HEAD), F32),
        ],
        scratch_shapes=[
            pltpu.VMEM((hps, HEAD, HEAD), F32),
            pltpu.VMEM((1, w), F32), pltpu.VMEM((1, w), F32), pltpu.VMEM((1, w), F32),
            pltpu.VMEM((1, LOW_PAD), F32),
        ],
        compiler_params=_cparams(("parallel", "parallel", "arbitrary")),
        name="rwkv",
    )(rkv, rkv, rkv, low, shift0, shift0, shift0, shift0,
      prm["mu"], prm["mu"], prm["mu"], prm["mu"],
      prm["w0"], prm["a0"], prm["k_k"], prm["k_a"], prm["r_k"], prm["gn_g"], prm["gn_b"],
      prm["w2"], prm["a2"], prm["g2"], bd, tri, wkv0)


def _attn_kernel(q_ref, k_ref, v_ref, kp_ref, vp_ref, tri_ref, trip_ref, g_ref, out_ref, *scratch,
                 tq, pb, n_pre, pipe_main, pipe_pre):
    i = pl.program_id(2)
    q0 = pl.multiple_of(i * tq, tq)
    lane = lax.broadcasted_iota(jnp.int32, (tq, LANES), 1)
    first = lane < HEAD
    q2 = q_ref[pl.ds(q0, tq), :] * (HEAD ** -0.5 * LOG2E)
    q_heads = (jnp.where(first, q2, 0.0).astype(BF16), jnp.where(first, 0.0, q2).astype(BF16))
    tri = tri_ref[...]
    trip = trip_ref[...]
    bufs, kmax_ref = scratch[:-1], scratch[-1]

    @pl.when(i == 0)
    def _():
        kmax_ref[...] = jnp.maximum(jnp.max(jnp.abs(k_ref[...]), axis=0, keepdims=True),
                                    jnp.max(jnp.abs(kp_ref[...]), axis=0, keepdims=True))

    reach_w = jnp.abs(q2) * kmax_ref[...] * 1.02
    zb_a = jnp.sum(jnp.where(first, reach_w, 0.0), -1, keepdims=True)
    z_bound = (zb_a, jnp.sum(reach_w, -1, keepdims=True) - zb_a)
    row = lax.broadcasted_iota(jnp.int32, (tq, tq), 0)
    colm = lax.broadcasted_iota(jnp.int32, (tq, tq), 1)
    causal = colm < row

    def suffix_sum(sp, tri_m):
        return _dot(sp.astype(BF16), tri_m)

    def block(qh, kb, vb, tri_m, carry, acc, mask):
        z = _dot_nt(qh, kb)
        sp = _softplus2(z)
        if mask is not None:
            sp = jnp.where(mask, sp, 0.0)
        suffix = suffix_sum(sp, tri_m)
        p = jnp.exp2((z + carry) - suffix)
        if mask is not None:
            p = jnp.where(mask, p, 0.0)
        acc = acc + _dot(p.astype(BF16), vb)
        return carry - suffix[:, 0:1], acc

    def piped(kr, vr, bs, tri_m, count, block_of, bufs, state, first_scored):
        z_s, d_s, tot_s = bufs
        start = 1 if first_scored else 0
        if not first_scored:
            z_s[1] = jnp.zeros_like(z_s[1])
        d_s[...] = jnp.zeros_like(d_s)
        tot_s[...] = jnp.zeros_like(tot_s)
        last = jnp.maximum(count - 1, 0)
        rows = lambda m: pl.ds(pl.multiple_of(block_of(jnp.clip(m, 0, last)) * bs, bs), bs)

        def stages(n, par, st):
            kb = kr[rows(n), :].astype(BF16)
            for h in range(2):
                z_s[par, h] = _dot_nt(q_heads[h], kb)
            live = jnp.logical_and(n >= 2, n - 2 < count)
            vb = vr[rows(n - 2), :].astype(BF16)
            for h in range(2):
                p = jnp.exp2(d_s[h] + jnp.where(live, st[2 * h], -1e30))
                st[2 * h + 1] = st[2 * h + 1] + _dot(p.astype(BF16), vb)
                st[2 * h] = st[2 * h] - jnp.where(live, tot_s[h], 0.0)
            for h in range(2):
                z = z_s[1 - par, h]
                suffix = suffix_sum(_softplus2(z), tri_m)
                d_s[h] = z - suffix
                tot_s[h] = suffix[:, 0:1]
            return st

        trips = (count + 3 - start) // 2

        def body(c):
            j, st = c[0], list(c[2:])
            for k in range(2):
                st = stages(start + 2 * j + k, (start + k) % 2, st)
            reach = jnp.maximum(jnp.max(st[0] + z_bound[0]), jnp.max(st[2] + z_bound[1]))
            return (j + 1, reach < EXP2_UNDERFLOW, *st)

        cond = lambda c: jnp.logical_and(c[0] < trips, jnp.logical_not(c[1]))
        out = lax.while_loop(cond, body, (jnp.int32(0), jnp.bool_(False), *state))
        return out[2:]

    state = [jnp.zeros((tq, 1), F32), jnp.zeros((tq, LANES), F32)] * 2
    kd = k_ref[pl.ds(q0, tq), :].astype(BF16)
    if pipe_main:
        for h in range(2):
            bufs[0][0, h] = jnp.where(causal, _dot_nt(q_heads[h], kd), -1e30)
        state = piped(k_ref, v_ref, tq, tri, i + 1, lambda m: i - m, bufs[0:3], state, True)
    else:
        vd = v_ref[pl.ds(q0, tq), :].astype(BF16)
        state[0:2] = block(q_heads[0], kd, vd, tri, state[0], state[1], causal)
        state[2:4] = block(q_heads[1], kd, vd, tri, state[2], state[3], causal)

    if pipe_pre:
        state = piped(kp_ref, vp_ref, pb, trip, n_pre, lambda m: n_pre - 1 - m, bufs[-3:], state, False)
    else:
        for t in range(n_pre):
            rows = pl.ds((n_pre - 1 - t) * pb, pb)
            kb = kp_ref[rows, :].astype(BF16)
            vb = vp_ref[rows, :].astype(BF16)
            state = list(state)
            state[0:2] = block(q_heads[0], kb, vb, trip, state[0], state[1], None)
            state[2:4] = block(q_heads[1], kb, vb, trip, state[2], state[3], None)

    o = jnp.where(first, state[1], state[3])
    sq = o * o
    s_a = jnp.sum(jnp.where(first, sq, 0.0), -1, keepdims=True)
    s_b = jnp.sum(sq, -1, keepdims=True) - s_a
    inv = jnp.where(first, lax.rsqrt(s_a / HEAD + RMS_EPS), lax.rsqrt(s_b / HEAD + RMS_EPS))
    out_ref[...] = (o * inv * g_ref[...]).astype(out_ref.dtype)


def _attention(q, k, v, kp, vp, gain, tq, pb):
    bsz, length, c = q.shape
    b0, plen, _ = kp.shape
    assert length % tq == 0 and plen % pb == 0 and b0 in (1, bsz)
    sb = (lambda b: b) if b0 == bsz else (lambda b: 0)
    n_pre = plen // pb
    n_main = length // tq
    pipe_main = n_main > 1
    pipe_pre = n_pre >= 3
    tri1 = lambda n: (jnp.arange(n)[:, None] >= jnp.arange(n)[None, :]).astype(BF16)
    stage_bufs = lambda bs: [pltpu.VMEM((2, 2, tq, bs), F32), pltpu.VMEM((2, tq, bs), F32),
                             pltpu.VMEM((2, tq, 1), F32)]
    full = pl.BlockSpec((None, length, LANES), lambda b, h, i: (b, 0, h))
    pre = pl.BlockSpec((None, plen, LANES), lambda b, h, i: (sb(b), 0, h))
    kernel = functools.partial(_attn_kernel, tq=tq, pb=pb, n_pre=n_pre, pipe_main=pipe_main, pipe_pre=pipe_pre)
    return pl.pallas_call(
        kernel,
        grid=(bsz, c // LANES, n_main),
        in_specs=[
            full, full, full, pre, pre,
            pl.BlockSpec((tq, tq), lambda b, h, i: (0, 0)),
            pl.BlockSpec((pb, pb), lambda b, h, i: (0, 0)),
            pl.BlockSpec((1, LANES), lambda b, h, i: (0, h)),
        ],
        out_specs=pl.BlockSpec((None, tq, LANES), lambda b, h, i: (b, i, h)),
        out_shape=jax.ShapeDtypeStruct((bsz, length, c), BF16),
        scratch_shapes=((stage_bufs(tq) if pipe_main else []) + (stage_bufs(pb) if pipe_pre else [])
                        + [pltpu.VMEM((1, LANES), F32)]),
        compiler_params=_cparams(("parallel", "parallel", "arbitrary")),
        name="attn",
    )(q, k, v, kp, vp, tri1(tq), tri1(pb), gain)


def _route_columns(logits_t, bias):
    ne, n = logits_t.shape
    gsz = ne // N_GROUPS
    neg = -jnp.inf
    scores = _sigmoid(logits_t)
    sel = scores + bias
    ig = lax.broadcasted_iota(jnp.int32, (gsz, n), 0)
    gs = []
    for g in range(N_GROUPS):
        xg = sel[g * gsz:(g + 1) * gsz, :]
        m1 = jnp.max(xg, axis=0, keepdims=True)
        first = jnp.min(jnp.where(xg == m1, ig, gsz), axis=0, keepdims=True)
        m2 = jnp.max(jnp.where(ig == first, neg, xg), axis=0, keepdims=True)
        gs.append(m1 + m2)
    kept = []
    for g in range(N_GROUPS):
        rank = jnp.zeros((1, n), jnp.int32)
        for g2 in range(N_GROUPS):
            if g2 == g:
                continue
            ahead = (gs[g2] >= gs[g]) if g2 < g else (gs[g2] > gs[g])
            rank = rank + ahead.astype(jnp.int32)
        kept.append(jnp.where(rank < TOPK_GROUPS, sel[g * gsz:(g + 1) * gsz, :], neg))
    cand = jnp.concatenate(kept, axis=0)
    ie = lax.broadcasted_iota(jnp.int32, (ne, n), 0)
    ids, gates = [], []
    chosen = jnp.zeros((ne, n), F32)
    for _ in range(TOP_K):
        m = jnp.max(cand, axis=0, keepdims=True)
        idx = jnp.min(jnp.where(cand == m, ie, ne), axis=0, keepdims=True)
        hit = ie == idx
        gates.append(jnp.sum(jnp.where(hit, scores, 0.0), axis=0, keepdims=True))
        ids.append(idx)
        cand = jnp.where(hit, neg, cand)
        chosen = chosen + jnp.where(hit, 1.0, 0.0)
    gate = jnp.concatenate(gates, axis=0)
    gate = gate / jnp.sum(gate, axis=0, keepdims=True) * ROUTED_SCALE
    return jnp.concatenate(ids, axis=0), gate, jnp.sum(chosen, axis=1, keepdims=True)


def _out_proj_kernel(x_ref, rw_ref, sb_ref, wa_ref, wb_ref, g0_ref, b0_ref, g1_ref, b1_ref,
                     wrh_ref, wrl_ref, rb_ref, h_ref, hs_ref, hb_ref, idx_ref, gate_ref, cnt_ref):
    xn = _layer_norm(x_ref[...], g0_ref[...], b0_ref[...])
    mix = _dot(rw_ref[...], wa_ref[...]) + _dot(sb_ref[...], wb_ref[...])
    h = _layer_norm(DN_ALPHA * xn + mix, g1_ref[...], b1_ref[...])
    h_ref[...] = h
    tm, d = h.shape
    slab = d // LANES
    for c in range(slab):
        hs_ref[pl.ds(c, tm, stride=slab), :] = h[:, c * LANES:(c + 1) * LANES]
    hi, lo = _split2(h)
    hb_ref[...] = hi
    wrh = wrh_ref[...]
    logits_t = _dot_nt(wrh, hi) + _dot_nt(wrh, lo) + _dot_nt(wrl_ref[...], hi)
    idx, gate, cnt = _route_columns(logits_t, rb_ref[...])
    idx_ref[...] = idx
    gate_ref[...] = gate
    cnt_ref[...] = cnt


def _out_proj(x, rw, sbo, wa, wb, g0, b0, g1, b1, wrh, wrl, rbias, tm):
    t, d = x.shape
    ne = wrh.shape[0]
    assert t % tm == 0
    row = lambda w: pl.BlockSpec((tm, w), lambda i: (i, 0))
    col = pl.BlockSpec((TOP_K, tm), lambda i: (0, i))
    const = lambda a: pl.BlockSpec(a.shape, lambda i: (0, 0))
    return pl.pallas_call(
        _out_proj_kernel,
        grid=(t // tm,),
        in_specs=[row(d), row(C_RW), row(C_SB), const(wa), const(wb), const(g0), const(b0), const(g1), const(b1),
                  const(wrh), const(wrl), const(rbias)],
        out_specs=[row(d), pl.BlockSpec((tm * (d // LANES), LANES), lambda i: (i, 0)), row(d), col, col,
                   pl.BlockSpec((None, ne, 1), lambda i: (i, 0, 0))],
        out_shape=[jax.ShapeDtypeStruct((t, d), F32), jax.ShapeDtypeStruct((t * (d // LANES), LANES), F32),
                   jax.ShapeDtypeStruct((t, d), BF16),
                   jax.ShapeDtypeStruct((TOP_K, t), jnp.int32), jax.ShapeDtypeStruct((TOP_K, t), F32),
                   jax.ShapeDtypeStruct((t // tm, ne, 1), F32)],
        compiler_params=_cparams(("parallel",)),
        name="out_proj",
    )(x, rw, sbo, wa, wb, g0, b0, g1, b1, wrh, wrl, rbias)


def _expert_kernel(be_ref, nv_ref, tok_ref, nxt_ref, rw_ref, h_hbm, wg_ref, wu_ref, wd_ref, out_ref,
                   xbuf, sems, xb_ref, wgb_ref, wub_ref, wdb_ref, prev_ref):
    i = pl.program_id(0)
    nv = nv_ref[0]
    e = be_ref[i]
    slot = lax.rem(i, 2)
    blk = xb_ref.shape[0]
    slab = xbuf.shape[1] // blk

    def row_copy(tok, r, s):
        first_row = tok * slab if isinstance(tok, int) else pl.multiple_of(tok * slab, slab)
        src = h_hbm.at[pl.ds(first_row, slab), :]
        return pltpu.make_async_copy(src, xbuf.at[s, pl.ds(r * slab, slab), :], sems.at[s])

    @pl.when(i == 0)
    def _():
        prev_ref[0] = -1
        for r in range(blk):
            row_copy(tok_ref[0, r], r, 0).start(priority=GATHER_DMA_PRIORITY)

    @pl.when(i < nv)
    def _():
        @pl.when(e != prev_ref[0])
        def _():
            wgb_ref[...] = wg_ref[...].astype(BF16)
            wub_ref[...] = wu_ref[...].astype(BF16)
            wdb_ref[...] = wd_ref[...].astype(BF16)
            prev_ref[0] = e

        for r in range(blk):
            row_copy(0, r, slot).wait()
        for c in range(slab):
            xb_ref[:, c * LANES:(c + 1) * LANES] = xbuf[slot, pl.ds(c, blk, stride=slab), :].astype(BF16)
        for r in range(blk):
            row_copy(nxt_ref[0, r], r, 1 - slot).start(priority=GATHER_DMA_PRIORITY)
        x = xb_ref[...]
        hg = _dot(x, wgb_ref[...])
        hu = _dot(x, wub_ref[...])
        hid = (hg * _sigmoid(hg) * hu).astype(BF16)
        out_ref[...] = (_dot(hid, wdb_ref[...]) * rw_ref[...]).astype(out_ref.dtype)

        @pl.when(i == nv - 1)
        def _():
            for r in range(blk):
                row_copy(0, r, 1 - slot).wait()

    @pl.when(i >= nv)
    def _():
        out_ref[...] = jnp.zeros_like(out_ref)


def _experts(h_slabs, row_tok, row_w, blk_exp, n_valid, w_gate, w_up, w_down):
    _, d, de = w_gate.shape
    blk = EXPERT_BLOCK
    nb = row_tok.shape[0] // blk
    tok3 = row_tok.reshape(nb, 1, blk)
    live = lambda i, nv: jnp.minimum(i, nv[0] - 1)
    idx_spec = lambda step: pl.BlockSpec((None, 1, blk), lambda i, be, nv: (live(i + step, nv), 0, 0),
                                         memory_space=pltpu.SMEM)
    grid_spec = pltpu.PrefetchScalarGridSpec(
        num_scalar_prefetch=2,
        grid=(nb,),
        in_specs=[
            idx_spec(0), idx_spec(1),
            pl.BlockSpec((blk, 1), lambda i, be, nv: (live(i, nv), 0)),
            pl.BlockSpec(memory_space=pl.ANY),
            pl.BlockSpec((None, d, de), lambda i, be, nv: (be[i], 0, 0)),
            pl.BlockSpec((None, d, de), lambda i, be, nv: (be[i], 0, 0)),
            pl.BlockSpec((None, de, d), lambda i, be, nv: (be[i], 0, 0)),
        ],
        out_specs=pl.BlockSpec((blk, d), lambda i, be, nv: (i, 0)),
        scratch_shapes=[pltpu.VMEM((2, blk * (d // LANES), LANES), F32), pltpu.SemaphoreType.DMA((2,)),
                        pltpu.VMEM((blk, d), BF16),
                        pltpu.VMEM((d, de), BF16), pltpu.VMEM((d, de), BF16), pltpu.VMEM((de, d), BF16),
                        pltpu.SMEM((1,), jnp.int32)],
    )
    return pl.pallas_call(
        _expert_kernel,
        grid_spec=grid_spec,
        out_shape=jax.ShapeDtypeStruct((nb * blk, d), BF16),
        compiler_params=_cparams(("arbitrary",)),
        name="experts",
    )(blk_exp, n_valid, tok3, tok3, row_w, h_slabs, w_gate, w_up, w_down)


def _final_kernel(h_ref, hb_ref, routed_ref, wg_ref, wu_ref, wd_ref, g_ref, b_ref, out_ref):
    x = hb_ref[...]
    hg = _dot(x, wg_ref[...])
    hu = _dot(x, wu_ref[...])
    shared = _dot((hg * _sigmoid(hg) * hu).astype(BF16), wd_ref[...])
    y = DN_ALPHA * h_ref[...] + routed_ref[...] + shared
    out_ref[...] = _layer_norm(y, g_ref[...], b_ref[...])


def _final(h, hb, routed, wg, wu, wd, g, b, tm):
    t, d = h.shape
    assert t % tm == 0
    row = pl.BlockSpec((tm, d), lambda i: (i, 0))
    const = lambda a: pl.BlockSpec(a.shape, lambda i: (0, 0))
    return pl.pallas_call(
        _final_kernel,
        grid=(t // tm,),
        in_specs=[row, row, row, const(wg), const(wu), const(wd), const(g), const(b)],
        out_specs=row,
        out_shape=jax.ShapeDtypeStruct((t, d), F32),
        compiler_params=_cparams(("parallel",)),
        name="final",
    )(h, hb, routed, wg, wu, wd, g, b)


def _dispatch_plan(idx_t, gate_t, counts):
    ne = counts.shape[0]
    t = idx_t.shape[1]
    blk = EXPERT_BLOCK
    n_assign = t * TOP_K
    nb = n_assign // blk + ne
    n_rows = nb * blk
    n_fill = n_rows - n_assign
    e_flat = idx_t.reshape(n_assign)
    iota = jnp.arange(n_assign, dtype=jnp.int32)
    padded = (counts + blk - 1) // blk * blk
    n_valid = (jnp.sum(padded) // blk).astype(jnp.int32).reshape(1)
    fill_end = jnp.cumsum(padded - counts)
    fill_e = jnp.sum((fill_end[None, :] <= jnp.arange(n_fill, dtype=jnp.int32)[:, None]).astype(jnp.int32), axis=1)
    zeros = jnp.zeros((n_fill,), jnp.int32)
    keys = jnp.concatenate([2 * e_flat, 2 * fill_e + 1])
    toks = jnp.concatenate([iota % t, zeros])
    wts = jnp.concatenate([gate_t.reshape(n_assign), zeros.astype(F32)])
    aid = jnp.concatenate([iota, zeros + n_assign])
    keys, row_tok, row_w, row_aid = lax.sort((keys, toks, wts, aid), num_keys=1)
    blk_exp = jnp.minimum(keys[::blk] // 2, ne - 1)
    _, pos = lax.sort((row_aid, jnp.arange(n_rows, dtype=jnp.int32)), num_keys=1)
    return row_tok, row_w.reshape(n_rows, 1), blk_exp, n_valid, pos[:n_assign].reshape(TOP_K, t)


def _pick_tile(n, cap):
    t = cap
    while n % t:
        t //= 2
    return t


def kernel(x_prompt, x_sample, cache_sb_k, cache_sb_v, state_rwkv_wkv, state_rwkv_shift, meta_tokens, ln0_g, ln0_b, w_in, rw_mu, rw_w0, rw_w2, rw_a0, rw_a2, rw_g2, rw_k_k, rw_k_a, rw_r_k, rw_gn_g, rw_gn_b, sb_norm_g, w_out, ln1_g, ln1_b, w_router, router_bias, w_exp_gate, w_exp_up, w_exp_down, w_sh_gate, w_sh_up, w_sh_down, ln2_g, ln2_b):
    bsz, seq, d = x_prompt.shape
    dbs, dseq, _ = x_sample.shape
    past = cache_sb_k.shape[2]
    n_h = C_RW // HEAD
    ne = w_router.shape[-1]
    rw_cols = 3 * C_RW + D_DECAY + D_AAA + D_GATE
    o_wd, o_ad, o_gd = 3 * C_RW, 3 * C_RW + D_DECAY, 3 * C_RW + D_DECAY + D_AAA

    def pad_low(a):
        z = lambda n: jnp.zeros(a.shape[:-1] + (n,), a.dtype)
        return jnp.concatenate([a[..., :o_wd], a[..., o_wd:o_ad], z(LANES - D_DECAY), a[..., o_ad:o_gd],
                                z(LANES - D_AAA), a[..., o_gd:rw_cols], z(2 * LANES - D_GATE)], -1)

    def unpad_low(rkv_row, low_row):
        return jnp.concatenate([rkv_row, low_row[..., 0:D_DECAY], low_row[..., LANES:LANES + D_AAA],
                                low_row[..., 2 * LANES:2 * LANES + D_GATE]], -1)

    wi = w_in[0]
    w_rw = pad_low(wi[:, :rw_cols])
    w6 = jnp.concatenate([w_rw[:, :3 * C_RW], wi[:, rw_cols:]], 1).astype(BF16)
    wlow = w_rw[:, 3 * C_RW:].astype(BF16)
    pad_rows = lambda a, n: jnp.concatenate([a, jnp.zeros((n - a.shape[0], a.shape[1]), a.dtype)], 0)
    row = lambda a: a.reshape(1, -1)
    prm = {
        "mu": pad_low(rw_mu[0]).reshape(1, -1),
        "w0": row(rw_w0[0]), "a0": row(rw_a0[0]), "k_k": row(rw_k_k[0]), "k_a": row(rw_k_a[0]),
        "r_k": row(rw_r_k[0]), "gn_g": row(rw_gn_g[0]), "gn_b": row(rw_gn_b[0]),
        "w2": pad_rows(rw_w2[0], LANES), "a2": pad_rows(rw_a2[0], LANES), "g2": pad_rows(rw_g2[0], 2 * LANES),
    }
    g0, b0 = row(ln0_g), row(ln0_b)
    g1, b1 = row(ln1_g[0]), row(ln1_b[0])
    g2, b2 = row(ln2_g[0]), row(ln2_b[0])
    sb_gain = row(sb_norm_g[0])
    wo = w_out[0].astype(BF16)
    wo_a, wo_b = wo[:C_RW], wo[C_RW:]
    wr_hi, wr_lo = _split2(w_router[0].T)
    rbias = router_bias[0].astype(F32).reshape(ne, 1)
    wsg, wsu, wsd = w_sh_gate[0].astype(BF16), w_sh_up[0].astype(BF16), w_sh_down[0].astype(BF16)

    xp = x_prompt.reshape(bsz * seq, d)
    xs = x_sample.reshape(dbs * dseq, d)
    xm = meta_tokens.astype(x_prompt.dtype)
    rkv_m, _, k_m, v_m, low_m = _in_proj(xm, g0, b0, w6, wlow, N_META)
    rkv_p, q_p, k_p, v_p, low_p = _in_proj(xp, g0, b0, w6, wlow, _pick_tile(bsz * seq, 512))
    rkv_s, q_s, k_s, v_s, low_s = _in_proj(xs, g0, b0, w6, wlow, _pick_tile(dbs * dseq, 512))

    zero_shift = jnp.zeros((1, 1, 3 * C_RW + LOW_PAD), F32)
    zero_wkv = jnp.zeros((1, n_h, HEAD, HEAD), F32)
    _, wkv_m = _rwkv(rkv_m[None], low_m[None], zero_shift, zero_wkv, prm, N_META)
    shift_m = jnp.concatenate([rkv_m[-1:], low_m[-1:]], -1)[None]
    rw_p, wkv_p = _rwkv(rkv_p.reshape(bsz, seq, -1), low_p.reshape(bsz, seq, -1), shift_m, wkv_m, prm,
                        _pick_tile(seq, 64))
    shift_s0 = pad_low(state_rwkv_shift[0])
    rw_s, wkv_s = _rwkv(rkv_s.reshape(dbs, dseq, -1), low_s.reshape(dbs, dseq, -1), shift_s0,
                        state_rwkv_wkv[0], prm, dseq)

    c3 = lambda a, b_, l_: a.reshape(b_, l_, C_SB)
    sb_p = _attention(c3(q_p, bsz, seq), c3(k_p, bsz, seq), c3(v_p, bsz, seq), k_m[None], v_m[None], sb_gain,
                      _pick_tile(seq, 256), N_META)
    sb_s = _attention(c3(q_s, dbs, dseq), c3(k_s, dbs, dseq), c3(v_s, dbs, dseq),
                      cache_sb_k[0].reshape(dbs, past, C_SB), cache_sb_v[0].reshape(dbs, past, C_SB), sb_gain,
                      dseq, _pick_tile(past, 256))

    x_all = jnp.concatenate([xp, xs], 0)
    rw_all = jnp.concatenate([rw_p.reshape(bsz * seq, C_RW), rw_s.reshape(dbs * dseq, C_RW)], 0)
    sb_all = jnp.concatenate([sb_p.reshape(bsz * seq, C_SB), sb_s.reshape(dbs * dseq, C_SB)], 0)
    t_all = x_all.shape[0]
    tm = _pick_tile(t_all, 256)
    h, h_slabs, hb, idx_t, gate_t, cnt = _out_proj(x_all, rw_all, sb_all, wo_a, wo_b, g0, b0, g1, b1, wr_hi, wr_lo,
                                                   rbias, tm)

    counts = jnp.sum(cnt[:, :, 0], axis=0).astype(jnp.int32)
    row_tok, row_w, blk_exp, n_valid, pos = _dispatch_plan(idx_t, gate_t, counts)
    expert_out = _experts(h_slabs, row_tok, row_w, blk_exp, n_valid, w_exp_gate[0], w_exp_up[0], w_exp_down[0])
    routed = jnp.sum(jnp.take(expert_out, pos, axis=0).astype(F32), axis=0)
    y = _final(h, hb, routed, wsg, wsu, wsd, g2, b2, tm)

    y_prompt = y[:bsz * seq].reshape(bsz, seq, d)
    y_sample = y[bsz * seq:].reshape(dbs, dseq, d)
    heads = lambda a, b_, l_: a.reshape(b_, l_, C_SB // HEAD, HEAD)

    def with_meta(m, p):
        m4 = jnp.broadcast_to(heads(m, 1, N_META), (bsz, N_META, C_SB // HEAD, HEAD))
        return jnp.concatenate([m4, heads(p, bsz, seq)], 1)[None]

    k_prompt = with_meta(k_m, k_p)
    v_prompt = with_meta(v_m, v_p)
    last = lambda a, b_, l_: a.reshape(b_, l_, -1)[:, -1:]
    shift_prompt = unpad_low(last(rkv_p, bsz, seq), last(low_p, bsz, seq))[None]
    shift_sample = unpad_low(last(rkv_s, dbs, dseq), last(low_s, dbs, dseq))[None]
    return (y_prompt, y_sample, k_prompt, v_prompt, wkv_p[None], shift_prompt,
            heads(k_s, dbs, dseq)[None], heads(v_s, dbs, dseq)[None], wkv_s[None], shift_sample)
```

```python
import functools

import jax
import jax.numpy as jnp
from jax import lax
from jax.experimental import pallas as pl
from jax.experimental.pallas import tpu as pltpu

F32 = jnp.float32
BF16 = jnp.bfloat16
HIGHEST = lax.Precision.HIGHEST

N_META = 16
HEAD = 64
C_RW = 1024
C_SB = 1024
D_DECAY = 64
D_AAA = 64
D_GATE = 160
LOW_PAD = 512
TOP_K = 8
N_GROUPS = 8
TOPK_GROUPS = 4
ROUTED_SCALE = 2.5
LN_EPS = 1e-5
GN_EPS = 64e-5
RMS_EPS = 1e-6
DEPTH = 1
DN_ALPHA = (2 * DEPTH) ** 0.25
LOG2E = 1.4426950408889634
EXP2_UNDERFLOW = -170.0

LANES = 128
VMEM_LIMIT = 56 * 1024 * 1024
RW_HEADS_PER_STEP = 16
EXPERT_BLOCK = 256
GATHER_DMA_PRIORITY = 0
WEIGHT_DMA_PRIORITY = 1


def _cparams(sem):
    return pltpu.CompilerParams(dimension_semantics=sem, vmem_limit_bytes=VMEM_LIMIT)


def _layer_norm(x, g, b):
    mu = jnp.mean(x, -1, keepdims=True)
    xc = x - mu
    var = jnp.mean(xc * xc, -1, keepdims=True)
    return xc * lax.rsqrt(var + LN_EPS) * g + b


def _sigmoid(x):
    return 1.0 / (1.0 + jnp.exp(-x))


def _softplus(x):
    return jnp.maximum(x, 0.0) + jnp.log(1.0 + jnp.exp(-jnp.abs(x)))


def _softplus2(x2):
    neg_abs = lax.bitcast_convert_type(lax.bitcast_convert_type(x2, jnp.uint32) | jnp.uint32(0x80000000), F32)
    return jnp.maximum(x2, 0.0) + jnp.log(1.0 + jnp.exp2(neg_abs)) * LOG2E


def _dot(a, b, precision=None):
    return jnp.dot(a, b, preferred_element_type=F32, precision=precision)


def _dot_nt(a, b, precision=None):
    return lax.dot_general(a, b, (((1,), (1,)), ((), ())), preferred_element_type=F32, precision=precision)


def _dot_tn(a, b, precision=None):
    return lax.dot_general(a, b, (((0,), (0,)), ((), ())), preferred_element_type=F32, precision=precision)


def _split2(x):
    hi = x.astype(BF16)
    return hi, (x - hi.astype(F32)).astype(BF16)


def _split3(x):
    hi = x.astype(BF16)
    r1 = x - hi.astype(F32)
    mid = r1.astype(BF16)
    return hi, mid, (r1 - mid.astype(F32)).astype(BF16)


def _dot_x3(a, b, dot=_dot):
    ah, al = _split2(a)
    bh, bl = _split2(b)
    return dot(ah, bh) + dot(al, bh) + dot(ah, bl)


def _dot_bf(a, b, dot=_dot):
    return dot(a.astype(BF16), b.astype(BF16))


RW_PREC = {"tri": "bf", "dbl": "bf", "wu": "bf", "ro": "bf", "pq": "bf", "state": "x3"}


def _mm(site, a, b, dot=_dot):
    mode = RW_PREC[site]
    if mode == "bf":
        return _dot_bf(a, b, dot)
    if mode == "x3":
        return _dot_x3(a, b, dot)
    return dot(a, b, HIGHEST)


def _in_proj_kernel(x_ref, g_ref, b_ref, w_ref, wlow_ref, rkv_ref, q_ref, k_ref, v_ref, low_ref, xn_ref):
    j = pl.program_id(1)

    @pl.when(j == 0)
    def _():
        xn = _layer_norm(x_ref[...], g_ref[...], b_ref[...]).astype(BF16)
        xn_ref[...] = xn
        low_ref[...] = _dot(xn, wlow_ref[...])

    y = _dot(xn_ref[...], w_ref[...])

    @pl.when(j < 3)
    def _():
        rkv_ref[...] = y

    @pl.when(j == 3)
    def _():
        q_ref[...] = y

    @pl.when(j == 4)
    def _():
        k_ref[...] = y

    @pl.when(j == 5)
    def _():
        v_ref[...] = y


def _in_proj(x, ln_g, ln_b, w6, wlow, tm):
    t, d = x.shape
    assert t % tm == 0
    c = C_RW
    grid = (t // tm, 6)
    col = lambda i, j: (i, 0)
    return pl.pallas_call(
        _in_proj_kernel,
        grid=grid,
        in_specs=[
            pl.BlockSpec((tm, d), col),
            pl.BlockSpec((1, d), lambda i, j: (0, 0)),
            pl.BlockSpec((1, d), lambda i, j: (0, 0)),
            pl.BlockSpec((d, c), lambda i, j: (0, j)),
            pl.BlockSpec((d, LOW_PAD), lambda i, j: (0, 0)),
        ],
        out_specs=[
            pl.BlockSpec((tm, c), lambda i, j: (i, jnp.minimum(j, 2))),
            pl.BlockSpec((tm, c), col),
            pl.BlockSpec((tm, c), col),
            pl.BlockSpec((tm, c), col),
            pl.BlockSpec((tm, LOW_PAD), col),
        ],
        out_shape=[
            jax.ShapeDtypeStruct((t, 3 * c), F32),
            jax.ShapeDtypeStruct((t, c), F32),
            jax.ShapeDtypeStruct((t, c), F32),
            jax.ShapeDtypeStruct((t, c), F32),
            jax.ShapeDtypeStruct((t, LOW_PAD), F32),
        ],
        scratch_shapes=[pltpu.VMEM((tm, d), BF16)],
        compiler_params=_cparams(("parallel", "arbitrary")),
        name="in_proj",
    )(x, ln_g, ln_b, w6, wlow)


def _rwkv_kernel(r_ref, k_ref, v_ref, low_ref, sr_ref, sk_ref, sv_ref, slow_ref,
                 mur_ref, muk_ref, muv_ref, mulow_ref,
                 w0_ref, a0_ref, kk_ref, ka_ref, rk_ref, gng_ref, gnb_ref,
                 w2_ref, a2_ref, g2_ref, bd_ref, tri_ref, wkv0_ref,
                 out_ref, wkv_ref,
                 s_ref, pr_ref, pk_ref, pv_ref, plow_ref, *, tc, hps, n_dbl):
    c = pl.program_id(2)
    nc = pl.num_programs(2)

    @pl.when(c == 0)
    def _():
        s_ref[...] = wkv0_ref[...]
        pr_ref[...] = sr_ref[...]
        pk_ref[...] = sk_ref[...]
        pv_ref[...] = sv_ref[...]
        plow_ref[...] = slow_ref[...]

    def token_shift(cur_ref, prev_ref, mu_ref):
        cur = cur_ref[...]
        row = lax.broadcasted_iota(jnp.int32, cur.shape, 0)
        prev = jnp.where(row == 0, prev_ref[...], pltpu.roll(cur, 1, 0))
        prev_ref[...] = cur[tc - 1:tc, :]
        return cur + (prev - cur) * mu_ref[...]

    r = token_shift(r_ref, pr_ref, mur_ref)
    k = token_shift(k_ref, pk_ref, muk_ref)
    v = token_shift(v_ref, pv_ref, muv_ref)
    xl = token_shift(low_ref, plow_ref, mulow_ref)
    wd = xl[:, 0:LANES]
    ad = xl[:, LANES:2 * LANES]
    gd = xl[:, 2 * LANES:LOW_PAD]

    w_log = -_softplus(-(w0_ref[...] + _dot_x3(jnp.tanh(wd), w2_ref[...]))) - 0.5
    lw = -jnp.exp(w_log)
    a = _sigmoid(a0_ref[...] + _dot_x3(ad, a2_ref[...]))
    g = _dot_x3(_sigmoid(gd), g2_ref[...])

    bd = bd_ref[...]
    nb_lane = bd.shape[0]

    def head_sum(x):
        parts = []
        for j in range(x.shape[1] // nb_lane):
            hi, lo = _split2(x[:, j * nb_lane:(j + 1) * nb_lane])
            parts.append(_dot(hi, bd) + _dot(lo, bd))
        return parts[0] if len(parts) == 1 else jnp.concatenate(parts, axis=1)

    kk = k * kk_ref[...]
    kk = kk / jnp.maximum(jnp.sqrt(head_sum(kk * kk)), 1e-12)
    k2 = k * (1.0 + (a - 1.0) * ka_ref[...])
    bonus = head_sum(r * k2 * rk_ref[...]) * v

    ti = lax.broadcasted_iota(jnp.int32, (tc, tc), 0)
    si = lax.broadcasted_iota(jnp.int32, (tc, tc), 1)
    incl = si <= ti
    strict = si < ti
    tri = tri_ref[...]
    l1, l2, l3 = _split3(lw)
    cs = _dot(tri, l1) + _dot(tri, l2) + _dot(tri, l3)
    c_last = cs[tc - 1:tc, :]
    e_pos = jnp.exp(cs)
    e_neg = jnp.exp(-cs)
    e_end = jnp.exp(c_last - cs)
    b = kk * a
    r_t = r * e_pos
    al_t = -kk * jnp.exp(cs - lw)
    be_t = b * e_neg
    k_t = k2 * e_neg
    be_h = b * e_end
    k_h = k2 * e_end
    g_last = jnp.exp(c_last)

    eye_t = (si == ti).astype(F32)
    di = lax.broadcasted_iota(jnp.int32, (HEAD, HEAD), 0)
    dj = lax.broadcasted_iota(jnp.int32, (HEAD, HEAD), 1)
    eye_h = di == dj

    hs = range(hps)
    cut = lambda x: [x[:, h * HEAD:(h + 1) * HEAD] for h in hs]
    al_h, r_h, v_h, bet, kt, beh, kh, gl = (cut(x) for x in (al_t, r_t, v, be_t, k_t, be_h, k_h, g_last))
    a_mat = [jnp.where(strict, _mm("tri", al_h[h], bet[h], _dot_nt), 0.0) for h in hs]
    b_mat = [jnp.where(strict, _mm("tri", al_h[h], kt[h], _dot_nt), 0.0) for h in hs]
    ar_mat = [jnp.where(incl, _mm("tri", r_h[h], bet[h], _dot_nt), 0.0) for h in hs]
    br_mat = [jnp.where(incl, _mm("tri", r_h[h], kt[h], _dot_nt), 0.0) for h in hs]
    bv = [_mm("wu", b_mat[h], v_h[h]) for h in hs]
    m = [eye_t + a_mat[h] for h in hs]
    pw = a_mat
    for _ in range(n_dbl):
        pw = [_mm("dbl", pw[h], pw[h]) for h in hs]
        m = [m[h] + _mm("dbl", m[h], pw[h]) for h in hs]
    w_m = [_mm("wu", m[h], al_h[h]) for h in hs]
    u_loc = [_mm("wu", m[h], bv[h]) for h in hs]
    r_hat = [r_h[h] + _mm("ro", ar_mat[h], w_m[h]) for h in hs]
    o_loc = [_mm("ro", ar_mat[h], u_loc[h]) + _mm("ro", br_mat[h], v_h[h]) for h in hs]
    p_mat = [_mm("pq", w_m[h], beh[h], _dot_tn) + jnp.where(eye_h, gl[h], 0.0) for h in hs]
    q_mat = [_mm("pq", u_loc[h], beh[h], _dot_tn) + _mm("pq", v_h[h], kh[h], _dot_tn) for h in hs]
    s0 = [s_ref[h] for h in hs]
    o = [o_loc[h] + _mm("state", r_hat[h], s0[h], _dot_nt) for h in hs]
    for h in hs:
        s_ref[h] = _mm("state", s0[h], p_mat[h]) + q_mat[h]
    outs = []
    for h in hs:
        mu = jnp.mean(o[h], -1, keepdims=True)
        oc = o[h] - mu
        var = jnp.mean(oc * oc, -1, keepdims=True)
        outs.append(oc * lax.rsqrt(var + GN_EPS))
    on = jnp.concatenate(outs, axis=1)
    out_ref[...] = ((on * gng_ref[...] + gnb_ref[...] + bonus) * g).astype(out_ref.dtype)

    @pl.when(c == nc - 1)
    def _():
        wkv_ref[...] = s_ref[...]


def _rwkv(rkv, low, shift0, wkv0, prm, tc):
    bsz, length, _ = rkv.shape
    b0 = shift0.shape[0]
    assert length % tc == 0 and b0 in (1, bsz)
    hps = RW_HEADS_PER_STEP
    w = hps * HEAD
    n_h = C_RW // HEAD
    ng = n_h // hps
    nblk = C_RW // w
    n_dbl = max((tc - 1).bit_length() - 1, 0)
    sb = (lambda b: b) if b0 == bsz else (lambda b: 0)
    seq = lambda off: pl.BlockSpec((None, tc, w), lambda b, g, c: (b, c, off * nblk + g))
    row0 = lambda off: pl.BlockSpec((None, 1, w), lambda b, g, c: (sb(b), 0, off * nblk + g))
    vec = lambda off: pl.BlockSpec((1, w), lambda b, g, c: (0, off * nblk + g))
    low_blk = (3 * C_RW) // LOW_PAD
    nbd = 2 * LANES
    bd = (jnp.arange(nbd)[:, None] // HEAD == jnp.arange(nbd)[None, :] // HEAD).astype(BF16)
    tri = (jnp.arange(tc)[None, :] <= jnp.arange(tc)[:, None]).astype(BF16)
    kernel = functools.partial(_rwkv_kernel, tc=tc, hps=hps, n_dbl=n_dbl)
    return pl.pallas_call(
        kernel,
        grid=(bsz, ng, length // tc),
        in_specs=[
            seq(0), seq(1), seq(2),
            pl.BlockSpec((None, tc, LOW_PAD), lambda b, g, c: (b, c, 0)),
            row0(0), row0(1), row0(2),
            pl.BlockSpec((None, 1, LOW_PAD), lambda b, g, c: (sb(b), 0, low_blk)),
            vec(0), vec(1), vec(2),
            pl.BlockSpec((1, LOW_PAD), lambda b, g, c: (0, low_blk)),
            vec(0), vec(0), vec(0), vec(0), vec(0), vec(0), vec(0),
            pl.BlockSpec((LANES, w), lambda b, g, c: (0, g)),
            pl.BlockSpec((LANES, w), lambda b, g, c: (0, g)),
            pl.BlockSpec((2 * LANES, w), lambda b, g, c: (0, g)),
            pl.BlockSpec((nbd, nbd), lambda b, g, c: (0, 0)),
            pl.BlockSpec((tc, tc), lambda b, g, c: (0, 0)),
            pl.BlockSpec((None, hps, HEAD, HEAD), lambda b, g, c: (sb(b), g, 0, 0)),
        ],
        out_specs=[
            pl.BlockSpec((None, tc, w), lambda b, g, c: (b, c, g)),
            pl.BlockSpec((None, hps, HEAD, HEAD), lambda b, g, c: (b, g, 0, 0)),
        ],
        out_shape=[
            jax.ShapeDtypeStruct((bsz, length, C_RW), BF16),
            jax.ShapeDtypeStruct((bsz, n_h, HEAD, HEAD), F32),
        ],
        scratch_shapes=[
            pltpu.VMEM((hps, HEAD, HEAD), F32),
            pltpu.VMEM((1, w), F32), pltpu.VMEM((1, w), F32), pltpu.VMEM((1, w), F32),
            pltpu.VMEM((1, LOW_PAD), F32),
        ],
        compiler_params=_cparams(("parallel", "parallel", "arbitrary")),
        name="rwkv",
    )(rkv, rkv, rkv, low, shift0, shift0, shift0, shift0,
      prm["mu"], prm["mu"], prm["mu"], prm["mu"],
      prm["w0"], prm["a0"], prm["k_k"], prm["k_a"], prm["r_k"], prm["gn_g"], prm["gn_b"],
      prm["w2"], prm["a2"], prm["g2"], bd, tri, wkv0)


def _attn_kernel(q_ref, k_ref, v_ref, kp_ref, vp_ref, tri_ref, trip_ref, g_ref, out_ref, *scratch,
                 tq, pb, n_pre, pipe_main, pipe_pre):
    i = pl.program_id(2)
    q0 = pl.multiple_of(i * tq, tq)
    lane = lax.broadcasted_iota(jnp.int32, (tq, LANES), 1)
    first = lane < HEAD
    q2 = q_ref[pl.ds(q0, tq), :] * (HEAD ** -0.5 * LOG2E)
    q_heads = (jnp.where(first, q2, 0.0).astype(BF16), jnp.where(first, 0.0, q2).astype(BF16))
    tri = tri_ref[...]
    trip = trip_ref[...]
    bufs, kmax_ref = scratch[:-1], scratch[-1]

    @pl.when(i == 0)
    def _():
        kmax_ref[...] = jnp.maximum(jnp.max(jnp.abs(k_ref[...]), axis=0, keepdims=True),
                                    jnp.max(jnp.abs(kp_ref[...]), axis=0, keepdims=True))

    reach_w = jnp.abs(q2) * kmax_ref[...] * 1.02
    zb_a = jnp.sum(jnp.where(first, reach_w, 0.0), -1, keepdims=True)
    z_bound = (zb_a, jnp.sum(reach_w, -1, keepdims=True) - zb_a)
    row = lax.broadcasted_iota(jnp.int32, (tq, tq), 0)
    colm = lax.broadcasted_iota(jnp.int32, (tq, tq), 1)
    causal = colm < row

    def suffix_sum(sp, tri_m):
        return _dot(sp.astype(BF16), tri_m)

    def block(qh, kb, vb, tri_m, carry, acc, mask):
        z = _dot_nt(qh, kb)
        sp = _softplus2(z)
        if mask is not None:
            sp = jnp.where(mask, sp, 0.0)
        suffix = suffix_sum(sp, tri_m)
        p = jnp.exp2((z + carry) - suffix)
        if mask is not None:
            p = jnp.where(mask, p, 0.0)
        acc = acc + _dot(p.astype(BF16), vb)
        return carry - suffix[:, 0:1], acc

    def piped(kr, vr, bs, tri_m, count, block_of, bufs, state, first_scored):
        z_s, d_s, tot_s = bufs
        start = 1 if first_scored else 0
        if not first_scored:
            z_s[1] = jnp.zeros_like(z_s[1])
        d_s[...] = jnp.zeros_like(d_s)
        tot_s[...] = jnp.zeros_like(tot_s)
        last = jnp.maximum(count - 1, 0)
        rows = lambda m: pl.ds(pl.multiple_of(block_of(jnp.clip(m, 0, last)) * bs, bs), bs)

        def stages(n, par, st):
            kb = kr[rows(n), :].astype(BF16)
            for h in range(2):
                z_s[par, h] = _dot_nt(q_heads[h], kb)
            live = jnp.logical_and(n >= 2, n - 2 < count)
            vb = vr[rows(n - 2), :].astype(BF16)
            for h in range(2):
                p = jnp.exp2(d_s[h] + jnp.where(live, st[2 * h], -1e30))
                st[2 * h + 1] = st[2 * h + 1] + _dot(p.astype(BF16), vb)
                st[2 * h] = st[2 * h] - jnp.where(live, tot_s[h], 0.0)
            for h in range(2):
                z = z_s[1 - par, h]
                suffix = suffix_sum(_softplus2(z), tri_m)
                d_s[h] = z - suffix
                tot_s[h] = suffix[:, 0:1]
            return st

        trips = (count + 3 - start) // 2

        def body(c):
            j, st = c[0], list(c[2:])
            for k in range(2):
                st = stages(start + 2 * j + k, (start + k) % 2, st)
            reach = jnp.maximum(jnp.max(st[0] + z_bound[0]), jnp.max(st[2] + z_bound[1]))
            return (j + 1, reach < EXP2_UNDERFLOW, *st)

        cond = lambda c: jnp.logical_and(c[0] < trips, jnp.logical_not(c[1]))
        out = lax.while_loop(cond, body, (jnp.int32(0), jnp.bool_(False), *state))
        return out[2:]

    state = [jnp.zeros((tq, 1), F32), jnp.zeros((tq, LANES), F32)] * 2
    kd = k_ref[pl.ds(q0, tq), :].astype(BF16)
    if pipe_main:
        for h in range(2):
            bufs[0][0, h] = jnp.where(causal, _dot_nt(q_heads[h], kd), -1e30)
        state = piped(k_ref, v_ref, tq, tri, i + 1, lambda m: i - m, bufs[0:3], state, True)
    else:
        vd = v_ref[pl.ds(q0, tq), :].astype(BF16)
        state[0:2] = block(q_heads[0], kd, vd, tri, state[0], state[1], causal)
        state[2:4] = block(q_heads[1], kd, vd, tri, state[2], state[3], causal)

    if pipe_pre:
        state = piped(kp_ref, vp_ref, pb, trip, n_pre, lambda m: n_pre - 1 - m, bufs[-3:], state, False)
    else:
        for t in range(n_pre):
            rows = pl.ds((n_pre - 1 - t) * pb, pb)
            kb = kp_ref[rows, :].astype(BF16)
            vb = vp_ref[rows, :].astype(BF16)
            state = list(state)
            state[0:2] = block(q_heads[0], kb, vb, trip, state[0], state[1], None)
            state[2:4] = block(q_heads[1], kb, vb, trip, state[2], state[3], None)

    o = jnp.where(first, state[1], state[3])
    sq = o * o
    s_a = jnp.sum(jnp.where(first, sq, 0.0), -1, keepdims=True)
    s_b = jnp.sum(sq, -1, keepdims=True) - s_a
    inv = jnp.where(first, lax.rsqrt(s_a / HEAD + RMS_EPS), lax.rsqrt(s_b / HEAD + RMS_EPS))
    out_ref[...] = (o * inv * g_ref[...]).astype(out_ref.dtype)


def _attention(q, k, v, kp, vp, gain, tq, pb):
    bsz, length, c = q.shape
    b0, plen, _ = kp.shape
    assert length % tq == 0 and plen % pb == 0 and b0 in (1, bsz)
    sb = (lambda b: b) if b0 == bsz else (lambda b: 0)
    n_pre = plen // pb
    n_main = length // tq
    pipe_main = n_main > 1
    pipe_pre = n_pre >= 3
    tri1 = lambda n: (jnp.arange(n)[:, None] >= jnp.arange(n)[None, :]).astype(BF16)
    stage_bufs = lambda bs: [pltpu.VMEM((2, 2, tq, bs), F32), pltpu.VMEM((2, tq, bs), F32),
                             pltpu.VMEM((2, tq, 1), F32)]
    full = pl.BlockSpec((None, length, LANES), lambda b, h, i: (b, 0, h))
    pre = pl.BlockSpec((None, plen, LANES), lambda b, h, i: (sb(b), 0, h))
    kernel = functools.partial(_attn_kernel, tq=tq, pb=pb, n_pre=n_pre, pipe_main=pipe_main, pipe_pre=pipe_pre)
    return pl.pallas_call(
        kernel,
        grid=(bsz, c // LANES, n_main),
        in_specs=[
            full, full, full, pre, pre,
            pl.BlockSpec((tq, tq), lambda b, h, i: (0, 0)),
            pl.BlockSpec((pb, pb), lambda b, h, i: (0, 0)),
            pl.BlockSpec((1, LANES), lambda b, h, i: (0, h)),
        ],
        out_specs=pl.BlockSpec((None, tq, LANES), lambda b, h, i: (b, i, h)),
        out_shape=jax.ShapeDtypeStruct((bsz, length, c), BF16),
        scratch_shapes=((stage_bufs(tq) if pipe_main else []) + (stage_bufs(pb) if pipe_pre else [])
                        + [pltpu.VMEM((1, LANES), F32)]),
        compiler_params=_cparams(("parallel", "parallel", "arbitrary")),
        name="attn",
    )(q, k, v, kp, vp, tri1(tq), tri1(pb), gain)


def _route_columns(logits_t, bias):
    ne, n = logits_t.shape
    gsz = ne // N_GROUPS
    neg = -jnp.inf
    scores = _sigmoid(logits_t)
    sel = scores + bias
    ig = lax.broadcasted_iota(jnp.int32, (gsz, n), 0)
    gs = []
    for g in range(N_GROUPS):
        xg = sel[g * gsz:(g + 1) * gsz, :]
        m1 = jnp.max(xg, axis=0, keepdims=True)
        first = jnp.min(jnp.where(xg == m1, ig, gsz), axis=0, keepdims=True)
        m2 = jnp.max(jnp.where(ig == first, neg, xg), axis=0, keepdims=True)
        gs.append(m1 + m2)
    kept = []
    for g in range(N_GROUPS):
        rank = jnp.zeros((1, n), jnp.int32)
        for g2 in range(N_GROUPS):
            if g2 == g:
                continue
            ahead = (gs[g2] >= gs[g]) if g2 < g else (gs[g2] > gs[g])
            rank = rank + ahead.astype(jnp.int32)
        kept.append(jnp.where(rank < TOPK_GROUPS, sel[g * gsz:(g + 1) * gsz, :], neg))
    cand = jnp.concatenate(kept, axis=0)
    ie = lax.broadcasted_iota(jnp.int32, (ne, n), 0)
    ids, gates = [], []
    chosen = jnp.zeros((ne, n), F32)
    for _ in range(TOP_K):
        m = jnp.max(cand, axis=0, keepdims=True)
        idx = jnp.min(jnp.where(cand == m, ie, ne), axis=0, keepdims=True)
        hit = ie == idx
        gates.append(jnp.sum(jnp.where(hit, scores, 0.0), axis=0, keepdims=True))
        ids.append(idx)
        cand = jnp.where(hit, neg, cand)
        chosen = chosen + jnp.where(hit, 1.0, 0.0)
    gate = jnp.concatenate(gates, axis=0)
    gate = gate / jnp.sum(gate, axis=0, keepdims=True) * ROUTED_SCALE
    return jnp.concatenate(ids, axis=0), gate, jnp.sum(chosen, axis=1, keepdims=True)


def _out_proj_kernel(x_ref, rw_ref, sb_ref, wa_ref, wb_ref, g0_ref, b0_ref, g1_ref, b1_ref,
                     wrh_ref, wrl_ref, rb_ref, h_ref, hb_ref, idx_ref, gate_ref, cnt_ref):
    xn = _layer_norm(x_ref[...], g0_ref[...], b0_ref[...])
    mix = _dot(rw_ref[...], wa_ref[...]) + _dot(sb_ref[...], wb_ref[...])
    h = _layer_norm(DN_ALPHA * xn + mix, g1_ref[...], b1_ref[...])
    h_ref[...] = h
    hi, lo = _split2(h)
    hb_ref[...] = hi
    wrh = wrh_ref[...]
    logits_t = _dot_nt(wrh, hi) + _dot_nt(wrh, lo) + _dot_nt(wrl_ref[...], hi)
    idx, gate, cnt = _route_columns(logits_t, rb_ref[...])
    idx_ref[...] = idx
    gate_ref[...] = gate
    cnt_ref[...] = cnt


def _out_proj(x, rw, sbo, wa, wb, g0, b0, g1, b1, wrh, wrl, rbias, tm):
    t, d = x.shape
    ne = wrh.shape[0]
    assert t % tm == 0
    row = lambda w: pl.BlockSpec((tm, w), lambda i: (i, 0))
    col = pl.BlockSpec((TOP_K, tm), lambda i: (0, i))
    const = lambda a: pl.BlockSpec(a.shape, lambda i: (0, 0))
    return pl.pallas_call(
        _out_proj_kernel,
        grid=(t // tm,),
        in_specs=[row(d), row(C_RW), row(C_SB), const(wa), const(wb), const(g0), const(b0), const(g1), const(b1),
                  const(wrh), const(wrl), const(rbias)],
        out_specs=[row(d), row(d), col, col, pl.BlockSpec((None, ne, 1), lambda i: (i, 0, 0))],
        out_shape=[jax.ShapeDtypeStruct((t, d), F32), jax.ShapeDtypeStruct((t, d), BF16),
                   jax.ShapeDtypeStruct((TOP_K, t), jnp.int32), jax.ShapeDtypeStruct((TOP_K, t), F32),
                   jax.ShapeDtypeStruct((t // tm, ne, 1), F32)],
        compiler_params=_cparams(("parallel",)),
        name="out_proj",
    )(x, rw, sbo, wa, wb, g0, b0, g1, b1, wrh, wrl, rbias)


def _expert_kernel(be_ref, nx_ref, nv_ref, tok_ref, nxt_ref, rw_ref, h_hbm, wg_hbm, wu_hbm, wd_hbm, out_ref,
                   xbuf, sems, xb_ref, wg_buf, wu_buf, wd_buf, wsems, wgb_ref, wub_ref, wdb_ref, state_ref):
    i = pl.program_id(0)
    nv = nv_ref[0]
    e = be_ref[i]
    slot = lax.rem(i, 2)
    blk = xbuf.shape[1]

    def row_copy(src_row, r, s):
        return pltpu.make_async_copy(h_hbm.at[pl.ds(src_row, 1), :], xbuf.at[s, pl.ds(r, 1), :], sems.at[s])

    def weight_copies(expert, p):
        return [pltpu.make_async_copy(src.at[expert], dst.at[p], wsems.at[p])
                for src, dst in ((wg_hbm, wg_buf), (wu_hbm, wu_buf), (wd_hbm, wd_buf))]

    @pl.when(i == 0)
    def _():
        state_ref[0] = -1
        state_ref[1] = 0
        for cp in weight_copies(e, 0):
            cp.start(priority=WEIGHT_DMA_PRIORITY)
        for r in range(blk):
            row_copy(tok_ref[0, r], r, 0).start(priority=GATHER_DMA_PRIORITY)

    @pl.when(i < nv)
    def _():
        @pl.when(e != state_ref[0])
        def _():
            p = state_ref[1]
            for cp in weight_copies(e, p):
                cp.wait()

            @pl.when(nx_ref[i] != e)
            def _():
                for cp in weight_copies(nx_ref[i], 1 - p):
                    cp.start(priority=WEIGHT_DMA_PRIORITY)

            wgb_ref[...] = wg_buf[p].astype(BF16)
            wub_ref[...] = wu_buf[p].astype(BF16)
            wdb_ref[...] = wd_buf[p].astype(BF16)
            state_ref[0] = e
            state_ref[1] = 1 - p

        for r in range(blk):
            row_copy(0, r, slot).wait()
        xb_ref[...] = xbuf[slot].astype(BF16)
        for r in range(blk):
            row_copy(nxt_ref[0, r], r, 1 - slot).start(priority=GATHER_DMA_PRIORITY)
        x = xb_ref[...]
        hg = _dot(x, wgb_ref[...])
        hu = _dot(x, wub_ref[...])
        hid = (hg * _sigmoid(hg) * hu).astype(BF16)
        out_ref[...] = (_dot(hid, wdb_ref[...]) * rw_ref[...]).astype(out_ref.dtype)

        @pl.when(i == nv - 1)
        def _():
            for r in range(blk):
                row_copy(0, r, 1 - slot).wait()

    @pl.when(i >= nv)
    def _():
        out_ref[...] = jnp.zeros_like(out_ref)


def _experts(h, row_tok, row_w, blk_exp, n_valid, w_gate, w_up, w_down):
    _, d = h.shape
    _, _, de = w_gate.shape
    blk = EXPERT_BLOCK
    nb = row_tok.shape[0] // blk
    tok3 = row_tok.reshape(nb, 1, blk)
    after = jnp.sum((blk_exp[None, :] <= blk_exp[:, None]).astype(jnp.int32), axis=1)
    nxt_exp = jnp.where(after < n_valid[0], blk_exp[jnp.minimum(after, nb - 1)], blk_exp)
    live = lambda i, nv: jnp.minimum(i, nv[0] - 1)
    idx_spec = lambda step: pl.BlockSpec((None, 1, blk), lambda i, be, nx, nv: (live(i + step, nv), 0, 0),
                                         memory_space=pltpu.SMEM)
    hbm = pl.BlockSpec(memory_space=pl.ANY)
    grid_spec = pltpu.PrefetchScalarGridSpec(
        num_scalar_prefetch=3,
        grid=(nb,),
        in_specs=[
            idx_spec(0), idx_spec(1),
            pl.BlockSpec((blk, 1), lambda i, be, nx, nv: (live(i, nv), 0)),
            hbm, hbm, hbm, hbm,
        ],
        out_specs=pl.BlockSpec((blk, d), lambda i, be, nx, nv: (i, 0)),
        scratch_shapes=[pltpu.VMEM((2, blk, d), F32), pltpu.SemaphoreType.DMA((2,)), pltpu.VMEM((blk, d), BF16),
                        pltpu.VMEM((2, d, de), F32), pltpu.VMEM((2, d, de), F32), pltpu.VMEM((2, de, d), F32),
                        pltpu.SemaphoreType.DMA((2,)),
                        pltpu.VMEM((d, de), BF16), pltpu.VMEM((d, de), BF16), pltpu.VMEM((de, d), BF16),
                        pltpu.SMEM((2,), jnp.int32)],
    )
    return pl.pallas_call(
        _expert_kernel,
        grid_spec=grid_spec,
        out_shape=jax.ShapeDtypeStruct((nb * blk, d), BF16),
        compiler_params=_cparams(("arbitrary",)),
        name="experts",
    )(blk_exp, nxt_exp, n_valid, tok3, tok3, row_w, h, w_gate, w_up, w_down)


def _final_kernel(h_ref, hb_ref, routed_ref, wg_ref, wu_ref, wd_ref, g_ref, b_ref, out_ref):
    x = hb_ref[...]
    hg = _dot(x, wg_ref[...])
    hu = _dot(x, wu_ref[...])
    shared = _dot((hg * _sigmoid(hg) * hu).astype(BF16), wd_ref[...])
    y = DN_ALPHA * h_ref[...] + routed_ref[...] + shared
    out_ref[...] = _layer_norm(y, g_ref[...], b_ref[...])


def _final(h, hb, routed, wg, wu, wd, g, b, tm):
    t, d = h.shape
    assert t % tm == 0
    row = pl.BlockSpec((tm, d), lambda i: (i, 0))
    const = lambda a: pl.BlockSpec(a.shape, lambda i: (0, 0))
    return pl.pallas_call(
        _final_kernel,
        grid=(t // tm,),
        in_specs=[row, row, row, const(wg), const(wu), const(wd), const(g), const(b)],
        out_specs=row,
        out_shape=jax.ShapeDtypeStruct((t, d), F32),
        compiler_params=_cparams(("parallel",)),
        name="final",
    )(h, hb, routed, wg, wu, wd, g, b)


def _dispatch_plan(idx_t, gate_t, counts):
    ne = counts.shape[0]
    t = idx_t.shape[1]
    blk = EXPERT_BLOCK
    n_assign = t * TOP_K
    nb = n_assign // blk + ne
    n_rows = nb * blk
    n_fill = n_rows - n_assign
    e_flat = idx_t.reshape(n_assign)
    iota = jnp.arange(n_assign, dtype=jnp.int32)
    padded = (counts + blk - 1) // blk * blk
    n_valid = (jnp.sum(padded) // blk).astype(jnp.int32).reshape(1)
    fill_end = jnp.cumsum(padded - counts)
    fill_e = jnp.sum((fill_end[None, :] <= jnp.arange(n_fill, dtype=jnp.int32)[:, None]).astype(jnp.int32), axis=1)
    zeros = jnp.zeros((n_fill,), jnp.int32)
    keys = jnp.concatenate([2 * e_flat, 2 * fill_e + 1])
    toks = jnp.concatenate([iota % t, zeros])
    wts = jnp.concatenate([gate_t.reshape(n_assign), zeros.astype(F32)])
    aid = jnp.concatenate([iota, zeros + n_assign])
    keys, row_tok, row_w, row_aid = lax.sort((keys, toks, wts, aid), num_keys=1)
    blk_exp = jnp.minimum(keys[::blk] // 2, ne - 1)
    _, pos = lax.sort((row_aid, jnp.arange(n_rows, dtype=jnp.int32)), num_keys=1)
    return row_tok, row_w.reshape(n_rows, 1), blk_exp, n_valid, pos[:n_assign].reshape(TOP_K, t)


def _pick_tile(n, cap):
    t = cap
    while n % t:
        t //= 2
    return t


def kernel(x_prompt, x_sample, cache_sb_k, cache_sb_v, state_rwkv_wkv, state_rwkv_shift, meta_tokens, ln0_g, ln0_b, w_in, rw_mu, rw_w0, rw_w2, rw_a0, rw_a2, rw_g2, rw_k_k, rw_k_a, rw_r_k, rw_gn_g, rw_gn_b, sb_norm_g, w_out, ln1_g, ln1_b, w_router, router_bias, w_exp_gate, w_exp_up, w_exp_down, w_sh_gate, w_sh_up, w_sh_down, ln2_g, ln2_b):
    bsz, seq, d = x_prompt.shape
    dbs, dseq, _ = x_sample.shape
    past = cache_sb_k.shape[2]
    n_h = C_RW // HEAD
    ne = w_router.shape[-1]
    rw_cols = 3 * C_RW + D_DECAY + D_AAA + D_GATE
    o_wd, o_ad, o_gd = 3 * C_RW, 3 * C_RW + D_DECAY, 3 * C_RW + D_DECAY + D_AAA

    def pad_low(a):
        z = lambda n: jnp.zeros(a.shape[:-1] + (n,), a.dtype)
        return jnp.concatenate([a[..., :o_wd], a[..., o_wd:o_ad], z(LANES - D_DECAY), a[..., o_ad:o_gd],
                                z(LANES - D_AAA), a[..., o_gd:rw_cols], z(2 * LANES - D_GATE)], -1)

    def unpad_low(rkv_row, low_row):
        return jnp.concatenate([rkv_row, low_row[..., 0:D_DECAY], low_row[..., LANES:LANES + D_AAA],
                                low_row[..., 2 * LANES:2 * LANES + D_GATE]], -1)

    wi = w_in[0]
    w_rw = pad_low(wi[:, :rw_cols])
    w6 = jnp.concatenate([w_rw[:, :3 * C_RW], wi[:, rw_cols:]], 1).astype(BF16)
    wlow = w_rw[:, 3 * C_RW:].astype(BF16)
    pad_rows = lambda a, n: jnp.concatenate([a, jnp.zeros((n - a.shape[0], a.shape[1]), a.dtype)], 0)
    row = lambda a: a.reshape(1, -1)
    prm = {
        "mu": pad_low(rw_mu[0]).reshape(1, -1),
        "w0": row(rw_w0[0]), "a0": row(rw_a0[0]), "k_k": row(rw_k_k[0]), "k_a": row(rw_k_a[0]),
        "r_k": row(rw_r_k[0]), "gn_g": row(rw_gn_g[0]), "gn_b": row(rw_gn_b[0]),
        "w2": pad_rows(rw_w2[0], LANES), "a2": pad_rows(rw_a2[0], LANES), "g2": pad_rows(rw_g2[0], 2 * LANES),
    }
    g0, b0 = row(ln0_g), row(ln0_b)
    g1, b1 = row(ln1_g[0]), row(ln1_b[0])
    g2, b2 = row(ln2_g[0]), row(ln2_b[0])
    sb_gain = row(sb_norm_g[0])
    wo = w_out[0].astype(BF16)
    wo_a, wo_b = wo[:C_RW], wo[C_RW:]
    wr_hi, wr_lo = _split2(w_router[0].T)
    rbias = router_bias[0].astype(F32).reshape(ne, 1)
    wsg, wsu, wsd = w_sh_gate[0].astype(BF16), w_sh_up[0].astype(BF16), w_sh_down[0].astype(BF16)

    xp = x_prompt.reshape(bsz * seq, d)
    xs = x_sample.reshape(dbs * dseq, d)
    xm = meta_tokens.astype(x_prompt.dtype)
    rkv_m, _, k_m, v_m, low_m = _in_proj(xm, g0, b0, w6, wlow, N_META)
    rkv_p, q_p, k_p, v_p, low_p = _in_proj(xp, g0, b0, w6, wlow, _pick_tile(bsz * seq, 512))
    rkv_s, q_s, k_s, v_s, low_s = _in_proj(xs, g0, b0, w6, wlow, _pick_tile(dbs * dseq, 512))

    zero_shift = jnp.zeros((1, 1, 3 * C_RW + LOW_PAD), F32)
    zero_wkv = jnp.zeros((1, n_h, HEAD, HEAD), F32)
    _, wkv_m = _rwkv(rkv_m[None], low_m[None], zero_shift, zero_wkv, prm, N_META)
    shift_m = jnp.concatenate([rkv_m[-1:], low_m[-1:]], -1)[None]
    rw_p, wkv_p = _rwkv(rkv_p.reshape(bsz, seq, -1), low_p.reshape(bsz, seq, -1), shift_m, wkv_m, prm,
                        _pick_tile(seq, 64))
    shift_s0 = pad_low(state_rwkv_shift[0])
    rw_s, wkv_s = _rwkv(rkv_s.reshape(dbs, dseq, -1), low_s.reshape(dbs, dseq, -1), shift_s0,
                        state_rwkv_wkv[0], prm, dseq)

    c3 = lambda a, b_, l_: a.reshape(b_, l_, C_SB)
    sb_p = _attention(c3(q_p, bsz, seq), c3(k_p, bsz, seq), c3(v_p, bsz, seq), k_m[None], v_m[None], sb_gain,
                      _pick_tile(seq, 256), N_META)
    sb_s = _attention(c3(q_s, dbs, dseq), c3(k_s, dbs, dseq), c3(v_s, dbs, dseq),
                      cache_sb_k[0].reshape(dbs, past, C_SB), cache_sb_v[0].reshape(dbs, past, C_SB), sb_gain,
                      dseq, _pick_tile(past, 256))

    x_all = jnp.concatenate([xp, xs], 0)
    rw_all = jnp.concatenate([rw_p.reshape(bsz * seq, C_RW), rw_s.reshape(dbs * dseq, C_RW)], 0)
    sb_all = jnp.concatenate([sb_p.reshape(bsz * seq, C_SB), sb_s.reshape(dbs * dseq, C_SB)], 0)
    t_all = x_all.shape[0]
    tm = _pick_tile(t_all, 256)
    h, hb, idx_t, gate_t, cnt = _out_proj(x_all, rw_all, sb_all, wo_a, wo_b, g0, b0, g1, b1, wr_hi, wr_lo, rbias, tm)

    counts = jnp.sum(cnt[:, :, 0], axis=0).astype(jnp.int32)
    row_tok, row_w, blk_exp, n_valid, pos = _dispatch_plan(idx_t, gate_t, counts)
    expert_out = _experts(h, row_tok, row_w, blk_exp, n_valid, w_exp_gate[0], w_exp_up[0], w_exp_down[0])
    routed = jnp.sum(jnp.take(expert_out, pos, axis=0).astype(F32), axis=0)
    y = _final(h, hb, routed, wsg, wsu, wsd, g2, b2, tm)

    y_prompt = y[:bsz * seq].reshape(bsz, seq, d)
    y_sample = y[bsz * seq:].reshape(dbs, dseq, d)
    heads = lambda a, b_, l_: a.reshape(b_, l_, C_SB // HEAD, HEAD)

    def with_meta(m, p):
        m4 = jnp.broadcast_to(heads(m, 1, N_META), (bsz, N_META, C_SB // HEAD, HEAD))
        return jnp.concatenate([m4, heads(p, bsz, seq)], 1)[None]

    k_prompt = with_meta(k_m, k_p)
    v_prompt = with_meta(v_m, v_p)
    last = lambda a, b_, l_: a.reshape(b_, l_, -1)[:, -1:]
    shift_prompt = unpad_low(last(rkv_p, bsz, seq), last(low_p, bsz, seq))[None]
    shift_sample = unpad_low(last(rkv_s, dbs, dseq), last(low_s, dbs, dseq))[None]
    return (y_prompt, y_sample, k_prompt, v_prompt, wkv_p[None], shift_prompt,
            heads(k_s, dbs, dseq)[None], heads(v_s, dbs, dseq)[None], wkv_s[None], shift_sample)
```

```python
import functools

import jax
import jax.numpy as jnp
from jax import lax
from jax.experimental import pallas as pl
from jax.experimental.pallas import tpu as pltpu

F32 = jnp.float32
BF16 = jnp.bfloat16
HIGHEST = lax.Precision.HIGHEST

N_META = 16
HEAD = 64
C_RW = 1024
C_SB = 1024
D_DECAY = 64
D_AAA = 64
D_GATE = 160
LOW_PAD = 512
TOP_K = 8
N_GROUPS = 8
TOPK_GROUPS = 4
ROUTED_SCALE = 2.5
LN_EPS = 1e-5
GN_EPS = 64e-5
RMS_EPS = 1e-6
DEPTH = 1
DN_ALPHA = (2 * DEPTH) ** 0.25
LOG2E = 1.4426950408889634
EXP2_UNDERFLOW = -170.0

LANES = 128
VMEM_LIMIT = 56 * 1024 * 1024
RW_HEADS_PER_STEP = 16
EXPERT_BLOCK = 256
GATHER_DMA_PRIORITY = 0
WEIGHT_DMA_PRIORITY = 1


def _cparams(sem):
    return pltpu.CompilerParams(dimension_semantics=sem, vmem_limit_bytes=VMEM_LIMIT)


def _layer_norm(x, g, b):
    mu = jnp.mean(x, -1, keepdims=True)
    xc = x - mu
    var = jnp.mean(xc * xc, -1, keepdims=True)
    return xc * lax.rsqrt(var + LN_EPS) * g + b


def _sigmoid(x):
    return 1.0 / (1.0 + jnp.exp(-x))


def _softplus(x):
    return jnp.maximum(x, 0.0) + jnp.log(1.0 + jnp.exp(-jnp.abs(x)))


def _softplus2(x2):
    neg_abs = lax.bitcast_convert_type(lax.bitcast_convert_type(x2, jnp.uint32) | jnp.uint32(0x80000000), F32)
    return jnp.maximum(x2, 0.0) + jnp.log(1.0 + jnp.exp2(neg_abs)) * LOG2E


def _dot(a, b, precision=None):
    return jnp.dot(a, b, preferred_element_type=F32, precision=precision)


def _dot_nt(a, b, precision=None):
    return lax.dot_general(a, b, (((1,), (1,)), ((), ())), preferred_element_type=F32, precision=precision)


def _dot_tn(a, b, precision=None):
    return lax.dot_general(a, b, (((0,), (0,)), ((), ())), preferred_element_type=F32, precision=precision)


def _split2(x):
    hi = x.astype(BF16)
    return hi, (x - hi.astype(F32)).astype(BF16)


def _split3(x):
    hi = x.astype(BF16)
    r1 = x - hi.astype(F32)
    mid = r1.astype(BF16)
    return hi, mid, (r1 - mid.astype(F32)).astype(BF16)


def _dot_x3(a, b, dot=_dot):
    ah, al = _split2(a)
    bh, bl = _split2(b)
    return dot(ah, bh) + dot(al, bh) + dot(ah, bl)


def _dot_bf(a, b, dot=_dot):
    return dot(a.astype(BF16), b.astype(BF16))


RW_PREC = {"tri": "bf", "dbl": "bf", "wu": "bf", "ro": "bf", "pq": "bf", "state": "x3"}


def _mm(site, a, b, dot=_dot):
    mode = RW_PREC[site]
    if mode == "bf":
        return _dot_bf(a, b, dot)
    if mode == "x3":
        return _dot_x3(a, b, dot)
    return dot(a, b, HIGHEST)


def _in_proj_kernel(x_ref, g_ref, b_ref, w_ref, wlow_ref, rkv_ref, q_ref, k_ref, v_ref, low_ref, xn_ref):
    j = pl.program_id(1)

    @pl.when(j == 0)
    def _():
        xn = _layer_norm(x_ref[...], g_ref[...], b_ref[...]).astype(BF16)
        xn_ref[...] = xn
        low_ref[...] = _dot(xn, wlow_ref[...])

    y = _dot(xn_ref[...], w_ref[...])

    @pl.when(j < 3)
    def _():
        rkv_ref[...] = y

    @pl.when(j == 3)
    def _():
        q_ref[...] = y

    @pl.when(j == 4)
    def _():
        k_ref[...] = y

    @pl.when(j == 5)
    def _():
        v_ref[...] = y


def _in_proj(x, ln_g, ln_b, w6, wlow, tm):
    t, d = x.shape
    assert t % tm == 0
    c = C_RW
    grid = (t // tm, 6)
    col = lambda i, j: (i, 0)
    return pl.pallas_call(
        _in_proj_kernel,
        grid=grid,
        in_specs=[
            pl.BlockSpec((tm, d), col),
            pl.BlockSpec((1, d), lambda i, j: (0, 0)),
            pl.BlockSpec((1, d), lambda i, j: (0, 0)),
            pl.BlockSpec((d, c), lambda i, j: (0, j)),
            pl.BlockSpec((d, LOW_PAD), lambda i, j: (0, 0)),
        ],
        out_specs=[
            pl.BlockSpec((tm, c), lambda i, j: (i, jnp.minimum(j, 2))),
            pl.BlockSpec((tm, c), col),
            pl.BlockSpec((tm, c), col),
            pl.BlockSpec((tm, c), col),
            pl.BlockSpec((tm, LOW_PAD), col),
        ],
        out_shape=[
            jax.ShapeDtypeStruct((t, 3 * c), F32),
            jax.ShapeDtypeStruct((t, c), F32),
            jax.ShapeDtypeStruct((t, c), F32),
            jax.ShapeDtypeStruct((t, c), F32),
            jax.ShapeDtypeStruct((t, LOW_PAD), F32),
        ],
        scratch_shapes=[pltpu.VMEM((tm, d), BF16)],
        compiler_params=_cparams(("parallel", "arbitrary")),
        name="in_proj",
    )(x, ln_g, ln_b, w6, wlow)


def _rwkv_kernel(r_ref, k_ref, v_ref, low_ref, sr_ref, sk_ref, sv_ref, slow_ref,
                 mur_ref, muk_ref, muv_ref, mulow_ref,
                 w0_ref, a0_ref, kk_ref, ka_ref, rk_ref, gng_ref, gnb_ref,
                 w2_ref, a2_ref, g2_ref, bd_ref, tri_ref, wkv0_ref,
                 out_ref, wkv_ref,
                 s_ref, pr_ref, pk_ref, pv_ref, plow_ref, *, tc, hps, n_dbl):
    c = pl.program_id(2)
    nc = pl.num_programs(2)

    @pl.when(c == 0)
    def _():
        s_ref[...] = wkv0_ref[...]
        pr_ref[...] = sr_ref[...]
        pk_ref[...] = sk_ref[...]
        pv_ref[...] = sv_ref[...]
        plow_ref[...] = slow_ref[...]

    def token_shift(cur_ref, prev_ref, mu_ref):
        cur = cur_ref[...]
        row = lax.broadcasted_iota(jnp.int32, cur.shape, 0)
        prev = jnp.where(row == 0, prev_ref[...], pltpu.roll(cur, 1, 0))
        prev_ref[...] = cur[tc - 1:tc, :]
        return cur + (prev - cur) * mu_ref[...]

    r = token_shift(r_ref, pr_ref, mur_ref)
    k = token_shift(k_ref, pk_ref, muk_ref)
    v = token_shift(v_ref, pv_ref, muv_ref)
    xl = token_shift(low_ref, plow_ref, mulow_ref)
    wd = xl[:, 0:LANES]
    ad = xl[:, LANES:2 * LANES]
    gd = xl[:, 2 * LANES:LOW_PAD]

    w_log = -_softplus(-(w0_ref[...] + _dot_x3(jnp.tanh(wd), w2_ref[...]))) - 0.5
    lw = -jnp.exp(w_log)
    a = _sigmoid(a0_ref[...] + _dot_x3(ad, a2_ref[...]))
    g = _dot_x3(_sigmoid(gd), g2_ref[...])

    bd = bd_ref[...]
    nb_lane = bd.shape[0]

    def head_sum(x):
        parts = []
        for j in range(x.shape[1] // nb_lane):
            hi, lo = _split2(x[:, j * nb_lane:(j + 1) * nb_lane])
            parts.append(_dot(hi, bd) + _dot(lo, bd))
        return parts[0] if len(parts) == 1 else jnp.concatenate(parts, axis=1)

    kk = k * kk_ref[...]
    kk = kk / jnp.maximum(jnp.sqrt(head_sum(kk * kk)), 1e-12)
    k2 = k * (1.0 + (a - 1.0) * ka_ref[...])
    bonus = head_sum(r * k2 * rk_ref[...]) * v

    ti = lax.broadcasted_iota(jnp.int32, (tc, tc), 0)
    si = lax.broadcasted_iota(jnp.int32, (tc, tc), 1)
    incl = si <= ti
    strict = si < ti
    tri = tri_ref[...]
    l1, l2, l3 = _split3(lw)
    cs = _dot(tri, l1) + _dot(tri, l2) + _dot(tri, l3)
    c_last = cs[tc - 1:tc, :]
    e_pos = jnp.exp(cs)
    e_neg = jnp.exp(-cs)
    e_end = jnp.exp(c_last - cs)
    b = kk * a
    r_t = r * e_pos
    al_t = -kk * jnp.exp(cs - lw)
    be_t = b * e_neg
    k_t = k2 * e_neg
    be_h = b * e_end
    k_h = k2 * e_end
    g_last = jnp.exp(c_last)

    eye_t = (si == ti).astype(F32)
    di = lax.broadcasted_iota(jnp.int32, (HEAD, HEAD), 0)
    dj = lax.broadcasted_iota(jnp.int32, (HEAD, HEAD), 1)
    eye_h = di == dj

    hs = range(hps)
    cut = lambda x: [x[:, h * HEAD:(h + 1) * HEAD] for h in hs]
    al_h, r_h, v_h, bet, kt, beh, kh, gl = (cut(x) for x in (al_t, r_t, v, be_t, k_t, be_h, k_h, g_last))
    a_mat = [jnp.where(strict, _mm("tri", al_h[h], bet[h], _dot_nt), 0.0) for h in hs]
    b_mat = [jnp.where(strict, _mm("tri", al_h[h], kt[h], _dot_nt), 0.0) for h in hs]
    ar_mat = [jnp.where(incl, _mm("tri", r_h[h], bet[h], _dot_nt), 0.0) for h in hs]
    br_mat = [jnp.where(incl, _mm("tri", r_h[h], kt[h], _dot_nt), 0.0) for h in hs]
    bv = [_mm("wu", b_mat[h], v_h[h]) for h in hs]
    m = [eye_t + a_mat[h] for h in hs]
    pw = a_mat
    for _ in range(n_dbl):
        pw = [_mm("dbl", pw[h], pw[h]) for h in hs]
        m = [m[h] + _mm("dbl", m[h], pw[h]) for h in hs]
    w_m = [_mm("wu", m[h], al_h[h]) for h in hs]
    u_loc = [_mm("wu", m[h], bv[h]) for h in hs]
    r_hat = [r_h[h] + _mm("ro", ar_mat[h], w_m[h]) for h in hs]
    o_loc = [_mm("ro", ar_mat[h], u_loc[h]) + _mm("ro", br_mat[h], v_h[h]) for h in hs]
    p_mat = [_mm("pq", w_m[h], beh[h], _dot_tn) + jnp.where(eye_h, gl[h], 0.0) for h in hs]
    q_mat = [_mm("pq", u_loc[h], beh[h], _dot_tn) + _mm("pq", v_h[h], kh[h], _dot_tn) for h in hs]
    s0 = [s_ref[h] for h in hs]
    o = [o_loc[h] + _mm("state", r_hat[h], s0[h], _dot_nt) for h in hs]
    for h in hs:
        s_ref[h] = _mm("state", s0[h], p_mat[h]) + q_mat[h]
    outs = []
    for h in hs:
        mu = jnp.mean(o[h], -1, keepdims=True)
        oc = o[h] - mu
        var = jnp.mean(oc * oc, -1, keepdims=True)
        outs.append(oc * lax.rsqrt(var + GN_EPS))
    on = jnp.concatenate(outs, axis=1)
    out_ref[...] = ((on * gng_ref[...] + gnb_ref[...] + bonus) * g).astype(out_ref.dtype)

    @pl.when(c == nc - 1)
    def _():
        wkv_ref[...] = s_ref[...]


def _rwkv(rkv, low, shift0, wkv0, prm, tc):
    bsz, length, _ = rkv.shape
    b0 = shift0.shape[0]
    assert length % tc == 0 and b0 in (1, bsz)
    hps = RW_HEADS_PER_STEP
    w = hps * HEAD
    n_h = C_RW // HEAD
    ng = n_h // hps
    nblk = C_RW // w
    n_dbl = max((tc - 1).bit_length() - 1, 0)
    sb = (lambda b: b) if b0 == bsz else (lambda b: 0)
    seq = lambda off: pl.BlockSpec((None, tc, w), lambda b, g, c: (b, c, off * nblk + g))
    row0 = lambda off: pl.BlockSpec((None, 1, w), lambda b, g, c: (sb(b), 0, off * nblk + g))
    vec = lambda off: pl.BlockSpec((1, w), lambda b, g, c: (0, off * nblk + g))
    low_blk = (3 * C_RW) // LOW_PAD
    nbd = 2 * LANES
    bd = (jnp.arange(nbd)[:, None] // HEAD == jnp.arange(nbd)[None, :] // HEAD).astype(BF16)
    tri = (jnp.arange(tc)[None, :] <= jnp.arange(tc)[:, None]).astype(BF16)
    kernel = functools.partial(_rwkv_kernel, tc=tc, hps=hps, n_dbl=n_dbl)
    return pl.pallas_call(
        kernel,
        grid=(bsz, ng, length // tc),
        in_specs=[
            seq(0), seq(1), seq(2),
            pl.BlockSpec((None, tc, LOW_PAD), lambda b, g, c: (b, c, 0)),
            row0(0), row0(1), row0(2),
            pl.BlockSpec((None, 1, LOW_PAD), lambda b, g, c: (sb(b), 0, low_blk)),
            vec(0), vec(1), vec(2),
            pl.BlockSpec((1, LOW_PAD), lambda b, g, c: (0, low_blk)),
            vec(0), vec(0), vec(0), vec(0), vec(0), vec(0), vec(0),
            pl.BlockSpec((LANES, w), lambda b, g, c: (0, g)),
            pl.BlockSpec((LANES, w), lambda b, g, c: (0, g)),
            pl.BlockSpec((2 * LANES, w), lambda b, g, c: (0, g)),
            pl.BlockSpec((nbd, nbd), lambda b, g, c: (0, 0)),
            pl.BlockSpec((tc, tc), lambda b, g, c: (0, 0)),
            pl.BlockSpec((None, hps, HEAD, HEAD), lambda b, g, c: (sb(b), g, 0, 0)),
        ],
        out_specs=[
            pl.BlockSpec((None, tc, w), lambda b, g, c: (b, c, g)),
            pl.BlockSpec((None, hps, HEAD, HEAD), lambda b, g, c: (b, g, 0, 0)),
        ],
        out_shape=[
            jax.ShapeDtypeStruct((bsz, length, C_RW), BF16),
            jax.ShapeDtypeStruct((bsz, n_h, HEAD, HEAD), F32),
        ],
        scratch_shapes=[
            pltpu.VMEM((hps, HEAD, HEAD), F32),
            pltpu.VMEM((1, w), F32), pltpu.VMEM((1, w), F32), pltpu.VMEM((1, w), F32),
            pltpu.VMEM((1, LOW_PAD), F32),
        ],
        compiler_params=_cparams(("parallel", "parallel", "arbitrary")),
        name="rwkv",
    )(rkv, rkv, rkv, low, shift0, shift0, shift0, shift0,
      prm["mu"], prm["mu"], prm["mu"], prm["mu"],
      prm["w0"], prm["a0"], prm["k_k"], prm["k_a"], prm["r_k"], prm["gn_g"], prm["gn_b"],
      prm["w2"], prm["a2"], prm["g2"], bd, tri, wkv0)


def _attn_kernel(q_ref, k_ref, v_ref, kp_ref, vp_ref, tri_ref, trip_ref, g_ref, out_ref, *scratch,
                 tq, pb, n_pre, pipe_main, pipe_pre):
    i = pl.program_id(2)
    q0 = pl.multiple_of(i * tq, tq)
    lane = lax.broadcasted_iota(jnp.int32, (tq, LANES), 1)
    first = lane < HEAD
    q2 = q_ref[pl.ds(q0, tq), :] * (HEAD ** -0.5 * LOG2E)
    q_heads = (jnp.where(first, q2, 0.0).astype(BF16), jnp.where(first, 0.0, q2).astype(BF16))
    tri = tri_ref[...]
    trip = trip_ref[...]
    bufs, kmax_ref = scratch[:-1], scratch[-1]

    @pl.when(i == 0)
    def _():
        kmax_ref[...] = jnp.maximum(jnp.max(jnp.abs(k_ref[...]), axis=0, keepdims=True),
                                    jnp.max(jnp.abs(kp_ref[...]), axis=0, keepdims=True))

    reach_w = jnp.abs(q2) * kmax_ref[...] * 1.02
    zb_a = jnp.sum(jnp.where(first, reach_w, 0.0), -1, keepdims=True)
    z_bound = (zb_a, jnp.sum(reach_w, -1, keepdims=True) - zb_a)
    row = lax.broadcasted_iota(jnp.int32, (tq, tq), 0)
    colm = lax.broadcasted_iota(jnp.int32, (tq, tq), 1)
    causal = colm < row

    def suffix_sum(sp, tri_m):
        return _dot(sp.astype(BF16), tri_m)

    def block(qh, kb, vb, tri_m, carry, acc, mask):
        z = _dot_nt(qh, kb)
        sp = _softplus2(z)
        if mask is not None:
            sp = jnp.where(mask, sp, 0.0)
        suffix = suffix_sum(sp, tri_m)
        p = jnp.exp2((z + carry) - suffix)
        if mask is not None:
            p = jnp.where(mask, p, 0.0)
        acc = acc + _dot(p.astype(BF16), vb)
        return carry - suffix[:, 0:1], acc

    def piped(kr, vr, bs, tri_m, count, block_of, bufs, state, first_scored):
        z_s, d_s, tot_s = bufs
        last = jnp.maximum(count - 1, 0)
        rows = lambda m: pl.ds(pl.multiple_of(block_of(jnp.clip(m, 0, last)) * bs, bs), bs)

        def stages(n, par, st, accumulate=True):
            kb = kr[rows(n), :].astype(BF16)
            for h in range(2):
                z_s[par, h] = _dot_nt(q_heads[h], kb)
            live = jnp.logical_and(n >= 2, n - 2 < count)
            vb = vr[rows(n - 2), :].astype(BF16)
            for h in range(2 if accumulate else 0):
                p = jnp.exp2(d_s[h] + jnp.where(live, st[2 * h], -1e30))
                st[2 * h + 1] = st[2 * h + 1] + _dot(p.astype(BF16), vb)
                st[2 * h] = st[2 * h] - jnp.where(live, tot_s[h], 0.0)
            for h in range(2):
                z = z_s[1 - par, h]
                suffix = suffix_sum(_softplus2(z), tri_m)
                d_s[h] = z - suffix
                tot_s[h] = suffix[:, 0:1]
            return st

        state = list(state)
        if first_scored:
            start = 2
            state = stages(1, 1, state, accumulate=False)
        else:
            start = 0
            z_s[1] = jnp.zeros_like(z_s[1])
            d_s[...] = jnp.zeros_like(d_s)
            tot_s[...] = jnp.zeros_like(tot_s)
        trips = (count + 3 - start) // 2

        def body(c):
            j, st = c[0], list(c[2:])
            for k in range(2):
                st = stages(start + 2 * j + k, (start + k) % 2, st)
            reach = jnp.maximum(jnp.max(st[0] + z_bound[0]), jnp.max(st[2] + z_bound[1]))
            return (j + 1, reach < EXP2_UNDERFLOW, *st)

        cond = lambda c: jnp.logical_and(c[0] < trips, jnp.logical_not(c[1]))
        out = lax.while_loop(cond, body, (jnp.int32(0), jnp.bool_(False), *state))
        return out[2:]

    state = [jnp.zeros((tq, 1), F32), jnp.zeros((tq, LANES), F32)] * 2
    kd = k_ref[pl.ds(q0, tq), :].astype(BF16)
    if pipe_main:
        for h in range(2):
            bufs[0][0, h] = jnp.where(causal, _dot_nt(q_heads[h], kd), -1e30)
        state = piped(k_ref, v_ref, tq, tri, i + 1, lambda m: i - m, bufs[0:3], state, True)
    else:
        vd = v_ref[pl.ds(q0, tq), :].astype(BF16)
        state[0:2] = block(q_heads[0], kd, vd, tri, state[0], state[1], causal)
        state[2:4] = block(q_heads[1], kd, vd, tri, state[2], state[3], causal)

    if pipe_pre:
        state = piped(kp_ref, vp_ref, pb, trip, n_pre, lambda m: n_pre - 1 - m, bufs[-3:], state, False)
    else:
        for t in range(n_pre):
            rows = pl.ds((n_pre - 1 - t) * pb, pb)
            kb = kp_ref[rows, :].astype(BF16)
            vb = vp_ref[rows, :].astype(BF16)
            state = list(state)
            state[0:2] = block(q_heads[0], kb, vb, trip, state[0], state[1], None)
            state[2:4] = block(q_heads[1], kb, vb, trip, state[2], state[3], None)

    o = jnp.where(first, state[1], state[3])
    sq = o * o
    s_a = jnp.sum(jnp.where(first, sq, 0.0), -1, keepdims=True)
    s_b = jnp.sum(sq, -1, keepdims=True) - s_a
    inv = jnp.where(first, lax.rsqrt(s_a / HEAD + RMS_EPS), lax.rsqrt(s_b / HEAD + RMS_EPS))
    out_ref[...] = (o * inv * g_ref[...]).astype(out_ref.dtype)


def _attention(q, k, v, kp, vp, gain, tq, pb):
    bsz, length, c = q.shape
    b0, plen, _ = kp.shape
    assert length % tq == 0 and plen % pb == 0 and b0 in (1, bsz)
    sb = (lambda b: b) if b0 == bsz else (lambda b: 0)
    n_pre = plen // pb
    n_main = length // tq
    pipe_main = n_main > 1
    pipe_pre = n_pre >= 3
    tri1 = lambda n: (jnp.arange(n)[:, None] >= jnp.arange(n)[None, :]).astype(BF16)
    stage_bufs = lambda bs: [pltpu.VMEM((2, 2, tq, bs), F32), pltpu.VMEM((2, tq, bs), F32),
                             pltpu.VMEM((2, tq, 1), F32)]
    full = pl.BlockSpec((None, length, LANES), lambda b, h, i: (b, 0, h))
    pre = pl.BlockSpec((None, plen, LANES), lambda b, h, i: (sb(b), 0, h))
    kernel = functools.partial(_attn_kernel, tq=tq, pb=pb, n_pre=n_pre, pipe_main=pipe_main, pipe_pre=pipe_pre)
    return pl.pallas_call(
        kernel,
        grid=(bsz, c // LANES, n_main),
        in_specs=[
            full, full, full, pre, pre,
            pl.BlockSpec((tq, tq), lambda b, h, i: (0, 0)),
            pl.BlockSpec((pb, pb), lambda b, h, i: (0, 0)),
            pl.BlockSpec((1, LANES), lambda b, h, i: (0, h)),
        ],
        out_specs=pl.BlockSpec((None, tq, LANES), lambda b, h, i: (b, i, h)),
        out_shape=jax.ShapeDtypeStruct((bsz, length, c), BF16),
        scratch_shapes=((stage_bufs(tq) if pipe_main else []) + (stage_bufs(pb) if pipe_pre else [])
                        + [pltpu.VMEM((1, LANES), F32)]),
        compiler_params=_cparams(("parallel", "parallel", "arbitrary")),
        name="attn",
    )(q, k, v, kp, vp, tri1(tq), tri1(pb), gain)


def _route_columns(logits_t, bias):
    ne, n = logits_t.shape
    gsz = ne // N_GROUPS
    neg = -jnp.inf
    scores = _sigmoid(logits_t)
    sel = scores + bias
    ig = lax.broadcasted_iota(jnp.int32, (gsz, n), 0)
    gs = []
    for g in range(N_GROUPS):
        xg = sel[g * gsz:(g + 1) * gsz, :]
        m1 = jnp.max(xg, axis=0, keepdims=True)
        first = jnp.min(jnp.where(xg == m1, ig, gsz), axis=0, keepdims=True)
        m2 = jnp.max(jnp.where(ig == first, neg, xg), axis=0, keepdims=True)
        gs.append(m1 + m2)
    kept = []
    for g in range(N_GROUPS):
        rank = jnp.zeros((1, n), jnp.int32)
        for g2 in range(N_GROUPS):
            if g2 == g:
                continue
            ahead = (gs[g2] >= gs[g]) if g2 < g else (gs[g2] > gs[g])
            rank = rank + ahead.astype(jnp.int32)
        kept.append(jnp.where(rank < TOPK_GROUPS, sel[g * gsz:(g + 1) * gsz, :], neg))
    cand = jnp.concatenate(kept, axis=0)
    ie = lax.broadcasted_iota(jnp.int32, (ne, n), 0)
    ids, gates = [], []
    chosen = jnp.zeros((ne, n), F32)
    for _ in range(TOP_K):
        m = jnp.max(cand, axis=0, keepdims=True)
        idx = jnp.min(jnp.where(cand == m, ie, ne), axis=0, keepdims=True)
        hit = ie == idx
        gates.append(jnp.sum(jnp.where(hit, scores, 0.0), axis=0, keepdims=True))
        ids.append(idx)
        cand = jnp.where(hit, neg, cand)
        chosen = chosen + jnp.where(hit, 1.0, 0.0)
    gate = jnp.concatenate(gates, axis=0)
    gate = gate / jnp.sum(gate, axis=0, keepdims=True) * ROUTED_SCALE
    return jnp.concatenate(ids, axis=0), gate, jnp.sum(chosen, axis=1, keepdims=True)


def _out_proj_kernel(x_ref, rw_ref, sb_ref, wa_ref, wb_ref, g0_ref, b0_ref, g1_ref, b1_ref,
                     wrh_ref, wrl_ref, rb_ref, h_ref, hb_ref, idx_ref, gate_ref, cnt_ref):
    xn = _layer_norm(x_ref[...], g0_ref[...], b0_ref[...])
    mix = _dot(rw_ref[...], wa_ref[...]) + _dot(sb_ref[...], wb_ref[...])
    h = _layer_norm(DN_ALPHA * xn + mix, g1_ref[...], b1_ref[...])
    h_ref[...] = h
    hi, lo = _split2(h)
    hb_ref[...] = hi
    wrh = wrh_ref[...]
    logits_t = _dot_nt(wrh, hi) + _dot_nt(wrh, lo) + _dot_nt(wrl_ref[...], hi)
    idx, gate, cnt = _route_columns(logits_t, rb_ref[...])
    idx_ref[...] = idx
    gate_ref[...] = gate
    cnt_ref[...] = cnt


def _out_proj(x, rw, sbo, wa, wb, g0, b0, g1, b1, wrh, wrl, rbias, tm):
    t, d = x.shape
    ne = wrh.shape[0]
    assert t % tm == 0
    row = lambda w: pl.BlockSpec((tm, w), lambda i: (i, 0))
    col = pl.BlockSpec((TOP_K, tm), lambda i: (0, i))
    const = lambda a: pl.BlockSpec(a.shape, lambda i: (0, 0))
    return pl.pallas_call(
        _out_proj_kernel,
        grid=(t // tm,),
        in_specs=[row(d), row(C_RW), row(C_SB), const(wa), const(wb), const(g0), const(b0), const(g1), const(b1),
                  const(wrh), const(wrl), const(rbias)],
        out_specs=[row(d), row(d), col, col, pl.BlockSpec((None, ne, 1), lambda i: (i, 0, 0))],
        out_shape=[jax.ShapeDtypeStruct((t, d), F32), jax.ShapeDtypeStruct((t, d), BF16),
                   jax.ShapeDtypeStruct((TOP_K, t), jnp.int32), jax.ShapeDtypeStruct((TOP_K, t), F32),
                   jax.ShapeDtypeStruct((t // tm, ne, 1), F32)],
        compiler_params=_cparams(("parallel",)),
        name="out_proj",
    )(x, rw, sbo, wa, wb, g0, b0, g1, b1, wrh, wrl, rbias)


def _expert_kernel(be_ref, nx_ref, nv_ref, tok_ref, nxt_ref, rw_ref, h_hbm, wg_hbm, wu_hbm, wd_hbm, out_ref,
                   xbuf, sems, xb_ref, wg_buf, wu_buf, wd_buf, wsems, wgb_ref, wub_ref, wdb_ref, state_ref):
    i = pl.program_id(0)
    nv = nv_ref[0]
    e = be_ref[i]
    slot = lax.rem(i, 2)
    blk = xbuf.shape[1]

    def row_copy(src_row, r, s):
        return pltpu.make_async_copy(h_hbm.at[pl.ds(src_row, 1), :], xbuf.at[s, pl.ds(r, 1), :], sems.at[s])

    def weight_copies(expert, p):
        return [pltpu.make_async_copy(src.at[expert], dst.at[p], wsems.at[p])
                for src, dst in ((wg_hbm, wg_buf), (wu_hbm, wu_buf), (wd_hbm, wd_buf))]

    @pl.when(i == 0)
    def _():
        state_ref[0] = -1
        state_ref[1] = 0
        for cp in weight_copies(e, 0):
            cp.start(priority=WEIGHT_DMA_PRIORITY)
        for r in range(blk):
            row_copy(tok_ref[0, r], r, 0).start(priority=GATHER_DMA_PRIORITY)

    @pl.when(i < nv)
    def _():
        @pl.when(e != state_ref[0])
        def _():
            p = state_ref[1]
            for cp in weight_copies(e, p):
                cp.wait()

            @pl.when(nx_ref[i] != e)
            def _():
                for cp in weight_copies(nx_ref[i], 1 - p):
                    cp.start(priority=WEIGHT_DMA_PRIORITY)

            wgb_ref[...] = wg_buf[p].astype(BF16)
            wub_ref[...] = wu_buf[p].astype(BF16)
            wdb_ref[...] = wd_buf[p].astype(BF16)
            state_ref[0] = e
            state_ref[1] = 1 - p

        for r in range(blk):
            row_copy(0, r, slot).wait()
        xb_ref[...] = xbuf[slot].astype(BF16)
        for r in range(blk):
            row_copy(nxt_ref[0, r], r, 1 - slot).start(priority=GATHER_DMA_PRIORITY)
        x = xb_ref[...]
        hg = _dot(x, wgb_ref[...])
        hu = _dot(x, wub_ref[...])
        hid = (hg * _sigmoid(hg) * hu).astype(BF16)
        out_ref[...] = (_dot(hid, wdb_ref[...]) * rw_ref[...]).astype(out_ref.dtype)

        @pl.when(i == nv - 1)
        def _():
            for r in range(blk):
                row_copy(0, r, 1 - slot).wait()

    @pl.when(i >= nv)
    def _():
        out_ref[...] = jnp.zeros_like(out_ref)


def _experts(h, row_tok, row_w, blk_exp, n_valid, w_gate, w_up, w_down):
    _, d = h.shape
    _, _, de = w_gate.shape
    blk = EXPERT_BLOCK
    nb = row_tok.shape[0] // blk
    tok3 = row_tok.reshape(nb, 1, blk)
    after = jnp.sum((blk_exp[None, :] <= blk_exp[:, None]).astype(jnp.int32), axis=1)
    nxt_exp = jnp.where(after < n_valid[0], blk_exp[jnp.minimum(after, nb - 1)], blk_exp)
    live = lambda i, nv: jnp.minimum(i, nv[0] - 1)
    idx_spec = lambda step: pl.BlockSpec((None, 1, blk), lambda i, be, nx, nv: (live(i + step, nv), 0, 0),
                                         memory_space=pltpu.SMEM)
    hbm = pl.BlockSpec(memory_space=pl.ANY)
    grid_spec = pltpu.PrefetchScalarGridSpec(
        num_scalar_prefetch=3,
        grid=(nb,),
        in_specs=[
            idx_spec(0), idx_spec(1),
            pl.BlockSpec((blk, 1), lambda i, be, nx, nv: (live(i, nv), 0)),
            hbm, hbm, hbm, hbm,
        ],
        out_specs=pl.BlockSpec((blk, d), lambda i, be, nx, nv: (i, 0)),
        scratch_shapes=[pltpu.VMEM((2, blk, d), F32), pltpu.SemaphoreType.DMA((2,)), pltpu.VMEM((blk, d), BF16),
                        pltpu.VMEM((2, d, de), F32), pltpu.VMEM((2, d, de), F32), pltpu.VMEM((2, de, d), F32),
                        pltpu.SemaphoreType.DMA((2,)),
                        pltpu.VMEM((d, de), BF16), pltpu.VMEM((d, de), BF16), pltpu.VMEM((de, d), BF16),
                        pltpu.SMEM((2,), jnp.int32)],
    )
    return pl.pallas_call(
        _expert_kernel,
        grid_spec=grid_spec,
        out_shape=jax.ShapeDtypeStruct((nb * blk, d), BF16),
        compiler_params=_cparams(("arbitrary",)),
        name="experts",
    )(blk_exp, nxt_exp, n_valid, tok3, tok3, row_w, h, w_gate, w_up, w_down)


def _final_kernel(h_ref, hb_ref, routed_ref, wg_ref, wu_ref, wd_ref, g_ref, b_ref, out_ref):
    x = hb_ref[...]
    hg = _dot(x, wg_ref[...])
    hu = _dot(x, wu_ref[...])
    shared = _dot((hg * _sigmoid(hg) * hu).astype(BF16), wd_ref[...])
    y = DN_ALPHA * h_ref[...] + routed_ref[...] + shared
    out_ref[...] = _layer_norm(y, g_ref[...], b_ref[...])


def _final(h, hb, routed, wg, wu, wd, g, b, tm):
    t, d = h.shape
    assert t % tm == 0
    row = pl.BlockSpec((tm, d), lambda i: (i, 0))
    const = lambda a: pl.BlockSpec(a.shape, lambda i: (0, 0))
    return pl.pallas_call(
        _final_kernel,
        grid=(t // tm,),
        in_specs=[row, row, row, const(wg), const(wu), const(wd), const(g), const(b)],
        out_specs=row,
        out_shape=jax.ShapeDtypeStruct((t, d), F32),
        compiler_params=_cparams(("parallel",)),
        name="final",
    )(h, hb, routed, wg, wu, wd, g, b)


def _dispatch_plan(idx_t, gate_t, counts):
    ne = counts.shape[0]
    t = idx_t.shape[1]
    blk = EXPERT_BLOCK
    n_assign = t * TOP_K
    nb = n_assign // blk + ne
    n_rows = nb * blk
    n_fill = n_rows - n_assign
    e_flat = idx_t.reshape(n_assign)
    iota = jnp.arange(n_assign, dtype=jnp.int32)
    padded = (counts + blk - 1) // blk * blk
    n_valid = (jnp.sum(padded) // blk).astype(jnp.int32).reshape(1)
    fill_end = jnp.cumsum(padded - counts)
    fill_e = jnp.sum((fill_end[None, :] <= jnp.arange(n_fill, dtype=jnp.int32)[:, None]).astype(jnp.int32), axis=1)
    zeros = jnp.zeros((n_fill,), jnp.int32)
    keys = jnp.concatenate([2 * e_flat, 2 * fill_e + 1])
    toks = jnp.concatenate([iota % t, zeros])
    wts = jnp.concatenate([gate_t.reshape(n_assign), zeros.astype(F32)])
    aid = jnp.concatenate([iota, zeros + n_assign])
    keys, row_tok, row_w, row_aid = lax.sort((keys, toks, wts, aid), num_keys=1)
    blk_exp = jnp.minimum(keys[::blk] // 2, ne - 1)
    _, pos = lax.sort((row_aid, jnp.arange(n_rows, dtype=jnp.int32)), num_keys=1)
    return row_tok, row_w.reshape(n_rows, 1), blk_exp, n_valid, pos[:n_assign].reshape(TOP_K, t)


def _pick_tile(n, cap):
    t = cap
    while n % t:
        t //= 2
    return t


def kernel(x_prompt, x_sample, cache_sb_k, cache_sb_v, state_rwkv_wkv, state_rwkv_shift, meta_tokens, ln0_g, ln0_b, w_in, rw_mu, rw_w0, rw_w2, rw_a0, rw_a2, rw_g2, rw_k_k, rw_k_a, rw_r_k, rw_gn_g, rw_gn_b, sb_norm_g, w_out, ln1_g, ln1_b, w_router, router_bias, w_exp_gate, w_exp_up, w_exp_down, w_sh_gate, w_sh_up, w_sh_down, ln2_g, ln2_b):
    bsz, seq, d = x_prompt.shape
    dbs, dseq, _ = x_sample.shape
    past = cache_sb_k.shape[2]
    n_h = C_RW // HEAD
    ne = w_router.shape[-1]
    rw_cols = 3 * C_RW + D_DECAY + D_AAA + D_GATE
    o_wd, o_ad, o_gd = 3 * C_RW, 3 * C_RW + D_DECAY, 3 * C_RW + D_DECAY + D_AAA

    def pad_low(a):
        z = lambda n: jnp.zeros(a.shape[:-1] + (n,), a.dtype)
        return jnp.concatenate([a[..., :o_wd], a[..., o_wd:o_ad], z(LANES - D_DECAY), a[..., o_ad:o_gd],
                                z(LANES - D_AAA), a[..., o_gd:rw_cols], z(2 * LANES - D_GATE)], -1)

    def unpad_low(rkv_row, low_row):
        return jnp.concatenate([rkv_row, low_row[..., 0:D_DECAY], low_row[..., LANES:LANES + D_AAA],
                                low_row[..., 2 * LANES:2 * LANES + D_GATE]], -1)

    wi = w_in[0]
    w_rw = pad_low(wi[:, :rw_cols])
    w6 = jnp.concatenate([w_rw[:, :3 * C_RW], wi[:, rw_cols:]], 1).astype(BF16)
    wlow = w_rw[:, 3 * C_RW:].astype(BF16)
    pad_rows = lambda a, n: jnp.concatenate([a, jnp.zeros((n - a.shape[0], a.shape[1]), a.dtype)], 0)
    row = lambda a: a.reshape(1, -1)
    prm = {
        "mu": pad_low(rw_mu[0]).reshape(1, -1),
        "w0": row(rw_w0[0]), "a0": row(rw_a0[0]), "k_k": row(rw_k_k[0]), "k_a": row(rw_k_a[0]),
        "r_k": row(rw_r_k[0]), "gn_g": row(rw_gn_g[0]), "gn_b": row(rw_gn_b[0]),
        "w2": pad_rows(rw_w2[0], LANES), "a2": pad_rows(rw_a2[0], LANES), "g2": pad_rows(rw_g2[0], 2 * LANES),
    }
    g0, b0 = row(ln0_g), row(ln0_b)
    g1, b1 = row(ln1_g[0]), row(ln1_b[0])
    g2, b2 = row(ln2_g[0]), row(ln2_b[0])
    sb_gain = row(sb_norm_g[0])
    wo = w_out[0].astype(BF16)
    wo_a, wo_b = wo[:C_RW], wo[C_RW:]
    wr_hi, wr_lo = _split2(w_router[0].T)
    rbias = router_bias[0].astype(F32).reshape(ne, 1)
    wsg, wsu, wsd = w_sh_gate[0].astype(BF16), w_sh_up[0].astype(BF16), w_sh_down[0].astype(BF16)

    xp = x_prompt.reshape(bsz * seq, d)
    xs = x_sample.reshape(dbs * dseq, d)
    xm = meta_tokens.astype(x_prompt.dtype)
    rkv_m, _, k_m, v_m, low_m = _in_proj(xm, g0, b0, w6, wlow, N_META)
    rkv_p, q_p, k_p, v_p, low_p = _in_proj(xp, g0, b0, w6, wlow, _pick_tile(bsz * seq, 512))
    rkv_s, q_s, k_s, v_s, low_s = _in_proj(xs, g0, b0, w6, wlow, _pick_tile(dbs * dseq, 512))

    zero_shift = jnp.zeros((1, 1, 3 * C_RW + LOW_PAD), F32)
    zero_wkv = jnp.zeros((1, n_h, HEAD, HEAD), F32)
    _, wkv_m = _rwkv(rkv_m[None], low_m[None], zero_shift, zero_wkv, prm, N_META)
    shift_m = jnp.concatenate([rkv_m[-1:], low_m[-1:]], -1)[None]
    rw_p, wkv_p = _rwkv(rkv_p.reshape(bsz, seq, -1), low_p.reshape(bsz, seq, -1), shift_m, wkv_m, prm,
                        _pick_tile(seq, 64))
    shift_s0 = pad_low(state_rwkv_shift[0])
    rw_s, wkv_s = _rwkv(rkv_s.reshape(dbs, dseq, -1), low_s.reshape(dbs, dseq, -1), shift_s0,
                        state_rwkv_wkv[0], prm, dseq)

    c3 = lambda a, b_, l_: a.reshape(b_, l_, C_SB)
    sb_p = _attention(c3(q_p, bsz, seq), c3(k_p, bsz, seq), c3(v_p, bsz, seq), k_m[None], v_m[None], sb_gain,
                      _pick_tile(seq, 256), N_META)
    sb_s = _attention(c3(q_s, dbs, dseq), c3(k_s, dbs, dseq), c3(v_s, dbs, dseq),
                      cache_sb_k[0].reshape(dbs, past, C_SB), cache_sb_v[0].reshape(dbs, past, C_SB), sb_gain,
                      dseq, _pick_tile(past, 256))

    x_all = jnp.concatenate([xp, xs], 0)
    rw_all = jnp.concatenate([rw_p.reshape(bsz * seq, C_RW), rw_s.reshape(dbs * dseq, C_RW)], 0)
    sb_all = jnp.concatenate([sb_p.reshape(bsz * seq, C_SB), sb_s.reshape(dbs * dseq, C_SB)], 0)
    t_all = x_all.shape[0]
    tm = _pick_tile(t_all, 256)
    h, hb, idx_t, gate_t, cnt = _out_proj(x_all, rw_all, sb_all, wo_a, wo_b, g0, b0, g1, b1, wr_hi, wr_lo, rbias, tm)

    counts = jnp.sum(cnt[:, :, 0], axis=0).astype(jnp.int32)
    row_tok, row_w, blk_exp, n_valid, pos = _dispatch_plan(idx_t, gate_t, counts)
    expert_out = _experts(h, row_tok, row_w, blk_exp, n_valid, w_exp_gate[0], w_exp_up[0], w_exp_down[0])
    routed = jnp.sum(jnp.take(expert_out, pos, axis=0).astype(F32), axis=0)
    y = _final(h, hb, routed, wsg, wsu, wsd, g2, b2, tm)

    y_prompt = y[:bsz * seq].reshape(bsz, seq, d)
    y_sample = y[bsz * seq:].reshape(dbs, dseq, d)
    heads = lambda a, b_, l_: a.reshape(b_, l_, C_SB // HEAD, HEAD)

    def with_meta(m, p):
        m4 = jnp.broadcast_to(heads(m, 1, N_META), (bsz, N_META, C_SB // HEAD, HEAD))
        return jnp.concatenate([m4, heads(p, bsz, seq)], 1)[None]

    k_prompt = with_meta(k_m, k_p)
    v_prompt = with_meta(v_m, v_p)
    last = lambda a, b_, l_: a.reshape(b_, l_, -1)[:, -1:]
    shift_prompt = unpad_low(last(rkv_p, bsz, seq), last(low_p, bsz, seq))[None]
    shift_sample = unpad_low(last(rkv_s, dbs, dseq), last(low_s, dbs, dseq))[None]
    return (y_prompt, y_sample, k_prompt, v_prompt, wkv_p[None], shift_prompt,
            heads(k_s, dbs, dseq)[None], heads(v_s, dbs, dseq)[None], wkv_s[None], shift_sample)
```

```python
import functools

import jax
import jax.numpy as jnp
from jax import lax
from jax.experimental import pallas as pl
from jax.experimental.pallas import tpu as pltpu

F32 = jnp.float32
BF16 = jnp.bfloat16
HIGHEST = lax.Precision.HIGHEST

N_META = 16
HEAD = 64
C_RW = 1024
C_SB = 1024
D_DECAY = 64
D_AAA = 64
D_GATE = 160
LOW_PAD = 512
TOP_K = 8
N_GROUPS = 8
TOPK_GROUPS = 4
ROUTED_SCALE = 2.5
LN_EPS = 1e-5
GN_EPS = 64e-5
RMS_EPS = 1e-6
DEPTH = 1
DN_ALPHA = (2 * DEPTH) ** 0.25
LOG2E = 1.4426950408889634
EXP2_UNDERFLOW = -170.0

LANES = 128
VMEM_LIMIT = 56 * 1024 * 1024
RW_HEADS_PER_STEP = 16
EXPERT_BLOCK = 256
GATHER_DMA_PRIORITY = 0
WEIGHT_DMA_PRIORITY = 1


def _cparams(sem):
    return pltpu.CompilerParams(dimension_semantics=sem, vmem_limit_bytes=VMEM_LIMIT)


def _layer_norm(x, g, b):
    mu = jnp.mean(x, -1, keepdims=True)
    xc = x - mu
    var = jnp.mean(xc * xc, -1, keepdims=True)
    return xc * lax.rsqrt(var + LN_EPS) * g + b


def _sigmoid(x):
    return 1.0 / (1.0 + jnp.exp(-x))


def _softplus(x):
    return jnp.maximum(x, 0.0) + jnp.log(1.0 + jnp.exp(-jnp.abs(x)))


def _softplus2(x2):
    neg_abs = lax.bitcast_convert_type(lax.bitcast_convert_type(x2, jnp.uint32) | jnp.uint32(0x80000000), F32)
    return jnp.maximum(x2, 0.0) + jnp.log(1.0 + jnp.exp2(neg_abs)) * LOG2E


def _dot(a, b, precision=None):
    return jnp.dot(a, b, preferred_element_type=F32, precision=precision)


def _dot_nt(a, b, precision=None):
    return lax.dot_general(a, b, (((1,), (1,)), ((), ())), preferred_element_type=F32, precision=precision)


def _dot_tn(a, b, precision=None):
    return lax.dot_general(a, b, (((0,), (0,)), ((), ())), preferred_element_type=F32, precision=precision)


def _split2(x):
    hi = x.astype(BF16)
    return hi, (x - hi.astype(F32)).astype(BF16)


def _split3(x):
    hi = x.astype(BF16)
    r1 = x - hi.astype(F32)
    mid = r1.astype(BF16)
    return hi, mid, (r1 - mid.astype(F32)).astype(BF16)


def _dot_x3(a, b, dot=_dot):
    ah, al = _split2(a)
    bh, bl = _split2(b)
    return dot(ah, bh) + dot(al, bh) + dot(ah, bl)


def _dot_bf(a, b, dot=_dot):
    return dot(a.astype(BF16), b.astype(BF16))


RW_PREC = {"tri": "bf", "dbl": "bf", "wu": "bf", "ro": "bf", "pq": "bf", "state": "x3"}


def _mm(site, a, b, dot=_dot):
    mode = RW_PREC[site]
    if mode == "bf":
        return _dot_bf(a, b, dot)
    if mode == "x3":
        return _dot_x3(a, b, dot)
    return dot(a, b, HIGHEST)


def _in_proj_kernel(x_ref, g_ref, b_ref, w_ref, wlow_ref, rkv_ref, q_ref, k_ref, v_ref, low_ref, xn_ref):
    j = pl.program_id(1)

    @pl.when(j == 0)
    def _():
        xn = _layer_norm(x_ref[...], g_ref[...], b_ref[...]).astype(BF16)
        xn_ref[...] = xn
        low_ref[...] = _dot(xn, wlow_ref[...])

    y = _dot(xn_ref[...], w_ref[...])

    @pl.when(j < 3)
    def _():
        rkv_ref[...] = y

    @pl.when(j == 3)
    def _():
        q_ref[...] = y

    @pl.when(j == 4)
    def _():
        k_ref[...] = y

    @pl.when(j == 5)
    def _():
        v_ref[...] = y


def _in_proj(x, ln_g, ln_b, w6, wlow, tm):
    t, d = x.shape
    assert t % tm == 0
    c = C_RW
    grid = (t // tm, 6)
    col = lambda i, j: (i, 0)
    return pl.pallas_call(
        _in_proj_kernel,
        grid=grid,
        in_specs=[
            pl.BlockSpec((tm, d), col),
            pl.BlockSpec((1, d), lambda i, j: (0, 0)),
            pl.BlockSpec((1, d), lambda i, j: (0, 0)),
            pl.BlockSpec((d, c), lambda i, j: (0, j)),
            pl.BlockSpec((d, LOW_PAD), lambda i, j: (0, 0)),
        ],
        out_specs=[
            pl.BlockSpec((tm, c), lambda i, j: (i, jnp.minimum(j, 2))),
            pl.BlockSpec((tm, c), col),
            pl.BlockSpec((tm, c), col),
            pl.BlockSpec((tm, c), col),
            pl.BlockSpec((tm, LOW_PAD), col),
        ],
        out_shape=[
            jax.ShapeDtypeStruct((t, 3 * c), F32),
            jax.ShapeDtypeStruct((t, c), F32),
            jax.ShapeDtypeStruct((t, c), F32),
            jax.ShapeDtypeStruct((t, c), F32),
            jax.ShapeDtypeStruct((t, LOW_PAD), F32),
        ],
        scratch_shapes=[pltpu.VMEM((tm, d), BF16)],
        compiler_params=_cparams(("parallel", "arbitrary")),
        name="in_proj",
    )(x, ln_g, ln_b, w6, wlow)


def _rwkv_kernel(r_ref, k_ref, v_ref, low_ref, sr_ref, sk_ref, sv_ref, slow_ref,
                 mur_ref, muk_ref, muv_ref, mulow_ref,
                 w0_ref, a0_ref, kk_ref, ka_ref, rk_ref, gng_ref, gnb_ref,
                 w2_ref, a2_ref, g2_ref, bd_ref, tri_ref, wkv0_ref,
                 out_ref, wkv_ref,
                 s_ref, pr_ref, pk_ref, pv_ref, plow_ref, *, tc, hps, n_dbl):
    c = pl.program_id(2)
    nc = pl.num_programs(2)

    @pl.when(c == 0)
    def _():
        s_ref[...] = wkv0_ref[...]
        pr_ref[...] = sr_ref[...]
        pk_ref[...] = sk_ref[...]
        pv_ref[...] = sv_ref[...]
        plow_ref[...] = slow_ref[...]

    def token_shift(cur_ref, prev_ref, mu_ref):
        cur = cur_ref[...]
        row = lax.broadcasted_iota(jnp.int32, cur.shape, 0)
        prev = jnp.where(row == 0, prev_ref[...], pltpu.roll(cur, 1, 0))
        prev_ref[...] = cur[tc - 1:tc, :]
        return cur + (prev - cur) * mu_ref[...]

    r = token_shift(r_ref, pr_ref, mur_ref)
    k = token_shift(k_ref, pk_ref, muk_ref)
    v = token_shift(v_ref, pv_ref, muv_ref)
    xl = token_shift(low_ref, plow_ref, mulow_ref)
    wd = xl[:, 0:LANES]
    ad = xl[:, LANES:2 * LANES]
    gd = xl[:, 2 * LANES:LOW_PAD]

    w_log = -_softplus(-(w0_ref[...] + _dot_x3(jnp.tanh(wd), w2_ref[...]))) - 0.5
    lw = -jnp.exp(w_log)
    a = _sigmoid(a0_ref[...] + _dot_x3(ad, a2_ref[...]))
    g = _dot_x3(_sigmoid(gd), g2_ref[...])

    bd = bd_ref[...]
    nb_lane = bd.shape[0]

    def head_sum(x):
        parts = []
        for j in range(x.shape[1] // nb_lane):
            hi, lo = _split2(x[:, j * nb_lane:(j + 1) * nb_lane])
            parts.append(_dot(hi, bd) + _dot(lo, bd))
        return parts[0] if len(parts) == 1 else jnp.concatenate(parts, axis=1)

    kk = k * kk_ref[...]
    kk = kk / jnp.maximum(jnp.sqrt(head_sum(kk * kk)), 1e-12)
    k2 = k * (1.0 + (a - 1.0) * ka_ref[...])
    bonus = head_sum(r * k2 * rk_ref[...]) * v

    ti = lax.broadcasted_iota(jnp.int32, (tc, tc), 0)
    si = lax.broadcasted_iota(jnp.int32, (tc, tc), 1)
    incl = si <= ti
    strict = si < ti
    tri = tri_ref[...]
    l1, l2, l3 = _split3(lw)
    cs = _dot(tri, l1) + _dot(tri, l2) + _dot(tri, l3)
    c_last = cs[tc - 1:tc, :]
    e_pos = jnp.exp(cs)
    e_neg = jnp.exp(-cs)
    e_end = jnp.exp(c_last - cs)
    b = kk * a
    r_t = r * e_pos
    al_t = -kk * jnp.exp(cs - lw)
    be_t = b * e_neg
    k_t = k2 * e_neg
    be_h = b * e_end
    k_h = k2 * e_end
    g_last = jnp.exp(c_last)

    eye_t = (si == ti).astype(F32)
    di = lax.broadcasted_iota(jnp.int32, (HEAD, HEAD), 0)
    dj = lax.broadcasted_iota(jnp.int32, (HEAD, HEAD), 1)
    eye_h = di == dj

    hs = range(hps)
    cut = lambda x: [x[:, h * HEAD:(h + 1) * HEAD] for h in hs]
    al_h, r_h, v_h, bet, kt, beh, kh, gl = (cut(x) for x in (al_t, r_t, v, be_t, k_t, be_h, k_h, g_last))
    a_mat = [jnp.where(strict, _mm("tri", al_h[h], bet[h], _dot_nt), 0.0) for h in hs]
    b_mat = [jnp.where(strict, _mm("tri", al_h[h], kt[h], _dot_nt), 0.0) for h in hs]
    ar_mat = [jnp.where(incl, _mm("tri", r_h[h], bet[h], _dot_nt), 0.0) for h in hs]
    br_mat = [jnp.where(incl, _mm("tri", r_h[h], kt[h], _dot_nt), 0.0) for h in hs]
    bv = [_mm("wu", b_mat[h], v_h[h]) for h in hs]
    m = [eye_t + a_mat[h] for h in hs]
    pw = a_mat
    for _ in range(n_dbl):
        pw = [_mm("dbl", pw[h], pw[h]) for h in hs]
        m = [m[h] + _mm("dbl", m[h], pw[h]) for h in hs]
    w_m = [_mm("wu", m[h], al_h[h]) for h in hs]
    u_loc = [_mm("wu", m[h], bv[h]) for h in hs]
    r_hat = [r_h[h] + _mm("ro", ar_mat[h], w_m[h]) for h in hs]
    o_loc = [_mm("ro", ar_mat[h], u_loc[h]) + _mm("ro", br_mat[h], v_h[h]) for h in hs]
    p_mat = [_mm("pq", w_m[h], beh[h], _dot_tn) + jnp.where(eye_h, gl[h], 0.0) for h in hs]
    q_mat = [_mm("pq", u_loc[h], beh[h], _dot_tn) + _mm("pq", v_h[h], kh[h], _dot_tn) for h in hs]
    s0 = [s_ref[h] for h in hs]
    o = [o_loc[h] + _mm("state", r_hat[h], s0[h], _dot_nt) for h in hs]
    for h in hs:
        s_ref[h] = _mm("state", s0[h], p_mat[h]) + q_mat[h]
    outs = []
    for h in hs:
        mu = jnp.mean(o[h], -1, keepdims=True)
        oc = o[h] - mu
        var = jnp.mean(oc * oc, -1, keepdims=True)
        outs.append(oc * lax.rsqrt(var + GN_EPS))
    on = jnp.concatenate(outs, axis=1)
    out_ref[...] = ((on * gng_ref[...] + gnb_ref[...] + bonus) * g).astype(out_ref.dtype)

    @pl.when(c == nc - 1)
    def _():
        wkv_ref[...] = s_ref[...]


def _rwkv(rkv, low, shift0, wkv0, prm, tc):
    bsz, length, _ = rkv.shape
    b0 = shift0.shape[0]
    assert length % tc == 0 and b0 in (1, bsz)
    hps = RW_HEADS_PER_STEP
    w = hps * HEAD
    n_h = C_RW // HEAD
    ng = n_h // hps
    nblk = C_RW // w
    n_dbl = max((tc - 1).bit_length() - 1, 0)
    sb = (lambda b: b) if b0 == bsz else (lambda b: 0)
    seq = lambda off: pl.BlockSpec((None, tc, w), lambda b, g, c: (b, c, off * nblk + g))
    row0 = lambda off: pl.BlockSpec((None, 1, w), lambda b, g, c: (sb(b), 0, off * nblk + g))
    vec = lambda off: pl.BlockSpec((1, w), lambda b, g, c: (0, off * nblk + g))
    low_blk = (3 * C_RW) // LOW_PAD
    nbd = 2 * LANES
    bd = (jnp.arange(nbd)[:, None] // HEAD == jnp.arange(nbd)[None, :] // HEAD).astype(BF16)
    tri = (jnp.arange(tc)[None, :] <= jnp.arange(tc)[:, None]).astype(BF16)
    kernel = functools.partial(_rwkv_kernel, tc=tc, hps=hps, n_dbl=n_dbl)
    return pl.pallas_call(
        kernel,
        grid=(bsz, ng, length // tc),
        in_specs=[
            seq(0), seq(1), seq(2),
            pl.BlockSpec((None, tc, LOW_PAD), lambda b, g, c: (b, c, 0)),
            row0(0), row0(1), row0(2),
            pl.BlockSpec((None, 1, LOW_PAD), lambda b, g, c: (sb(b), 0, low_blk)),
            vec(0), vec(1), vec(2),
            pl.BlockSpec((1, LOW_PAD), lambda b, g, c: (0, low_blk)),
            vec(0), vec(0), vec(0), vec(0), vec(0), vec(0), vec(0),
            pl.BlockSpec((LANES, w), lambda b, g, c: (0, g)),
            pl.BlockSpec((LANES, w), lambda b, g, c: (0, g)),
            pl.BlockSpec((2 * LANES, w), lambda b, g, c: (0, g)),
            pl.BlockSpec((nbd, nbd), lambda b, g, c: (0, 0)),
            pl.BlockSpec((tc, tc), lambda b, g, c: (0, 0)),
            pl.BlockSpec((None, hps, HEAD, HEAD), lambda b, g, c: (sb(b), g, 0, 0)),
        ],
        out_specs=[
            pl.BlockSpec((None, tc, w), lambda b, g, c: (b, c, g)),
            pl.BlockSpec((None, hps, HEAD, HEAD), lambda b, g, c: (b, g, 0, 0)),
        ],
        out_shape=[
            jax.ShapeDtypeStruct((bsz, length, C_RW), BF16),
            jax.ShapeDtypeStruct((bsz, n_h, HEAD, HEAD), F32),
        ],
        scratch_shapes=[
            pltpu.VMEM((hps, HEAD, HEAD), F32),
            pltpu.VMEM((1, w), F32), pltpu.VMEM((1, w), F32), pltpu.VMEM((1, w), F32),
            pltpu.VMEM((1, LOW_PAD), F32),
        ],
        compiler_params=_cparams(("parallel", "parallel", "arbitrary")),
        name="rwkv",
    )(rkv, rkv, rkv, low, shift0, shift0, shift0, shift0,
      prm["mu"], prm["mu"], prm["mu"], prm["mu"],
      prm["w0"], prm["a0"], prm["k_k"], prm["k_a"], prm["r_k"], prm["gn_g"], prm["gn_b"],
      prm["w2"], prm["a2"], prm["g2"], bd, tri, wkv0)


def _attn_kernel(q_ref, k_ref, v_ref, kp_ref, vp_ref, tri_ref, trip_ref, g_ref, out_ref, *scratch,
                 tq, pb, n_pre, pipe_main, pipe_pre):
    i = pl.program_id(2)
    q0 = pl.multiple_of(i * tq, tq)
    lane = lax.broadcasted_iota(jnp.int32, (tq, LANES), 1)
    first = lane < HEAD
    q2 = q_ref[pl.ds(q0, tq), :] * (HEAD ** -0.5 * LOG2E)
    q_heads = (jnp.where(first, q2, 0.0).astype(BF16), jnp.where(first, 0.0, q2).astype(BF16))
    tri = tri_ref[...]
    trip = trip_ref[...]
    bufs, kmax_ref = scratch[:-1], scratch[-1]

    @pl.when(i == 0)
    def _():
        kmax_ref[...] = jnp.maximum(jnp.max(jnp.abs(k_ref[...]), axis=0, keepdims=True),
                                    jnp.max(jnp.abs(kp_ref[...]), axis=0, keepdims=True))

    reach_w = jnp.abs(q2) * kmax_ref[...] * 1.02
    zb_a = jnp.sum(jnp.where(first, reach_w, 0.0), -1, keepdims=True)
    z_bound = (zb_a, jnp.sum(reach_w, -1, keepdims=True) - zb_a)
    row = lax.broadcasted_iota(jnp.int32, (tq, tq), 0)
    colm = lax.broadcasted_iota(jnp.int32, (tq, tq), 1)
    causal = colm < row

    def suffix_sum(sp, tri_m):
        return _dot(sp.astype(BF16), tri_m)

    def block(qh, kb, vb, tri_m, carry, acc, mask):
        z = _dot_nt(qh, kb)
        sp = _softplus2(z)
        if mask is not None:
            sp = jnp.where(mask, sp, 0.0)
        suffix = suffix_sum(sp, tri_m)
        p = jnp.exp2((z + carry) - suffix)
        if mask is not None:
            p = jnp.where(mask, p, 0.0)
        acc = acc + _dot(p.astype(BF16), vb)
        return carry - suffix[:, 0:1], acc

    def piped(kr, vr, bs, tri_m, count, block_of, bufs, state, first_scored):
        z_s, d_s, tot_s = bufs
        last = jnp.maximum(count - 1, 0)
        rows = lambda m: pl.ds(pl.multiple_of(block_of(jnp.clip(m, 0, last)) * bs, bs), bs)

        def stages(n, par, st, accumulate=True):
            kb = kr[rows(n), :].astype(BF16)
            for h in range(2):
                z_s[par, h] = _dot_nt(q_heads[h], kb)
            live = jnp.logical_and(n >= 2, n - 2 < count)
            vb = vr[rows(n - 2), :].astype(BF16)
            for h in range(2 if accumulate else 0):
                p = jnp.exp2(d_s[h] + jnp.where(live, st[2 * h], -1e30))
                st[2 * h + 1] = st[2 * h + 1] + _dot(p.astype(BF16), vb)
                st[2 * h] = st[2 * h] - jnp.where(live, tot_s[h], 0.0)
            for h in range(2):
                z = z_s[1 - par, h]
                suffix = suffix_sum(_softplus2(z), tri_m)
                d_s[h] = z - suffix
                tot_s[h] = suffix[:, 0:1]
            return st

        state = list(state)
        if first_scored:
            start = 2
            state = stages(1, 1, state, accumulate=False)
        else:
            start = 0
            z_s[1] = jnp.zeros_like(z_s[1])
            d_s[...] = jnp.zeros_like(d_s)
            tot_s[...] = jnp.zeros_like(tot_s)
        trips = (count + 3 - start) // 2

        def body(c):
            j, st = c[0], list(c[2:])
            for k in range(2):
                st = stages(start + 2 * j + k, (start + k) % 2, st)
            reach = jnp.maximum(jnp.max(st[0] + z_bound[0]), jnp.max(st[2] + z_bound[1]))
            return (j + 1, reach < EXP2_UNDERFLOW, *st)

        cond = lambda c: jnp.logical_and(c[0] < trips, jnp.logical_not(c[1]))
        out = lax.while_loop(cond, body, (jnp.int32(0), jnp.bool_(False), *state))
        return out[2:]

    state = [jnp.zeros((tq, 1), F32), jnp.zeros((tq, LANES), F32)] * 2
    kd = k_ref[pl.ds(q0, tq), :].astype(BF16)
    if pipe_main:
        for h in range(2):
            bufs[0][0, h] = jnp.where(causal, _dot_nt(q_heads[h], kd), -1e30)
        state = piped(k_ref, v_ref, tq, tri, i + 1, lambda m: i - m, bufs[0:3], state, True)
    else:
        vd = v_ref[pl.ds(q0, tq), :].astype(BF16)
        state[0:2] = block(q_heads[0], kd, vd, tri, state[0], state[1], causal)
        state[2:4] = block(q_heads[1], kd, vd, tri, state[2], state[3], causal)

    if pipe_pre:
        state = piped(kp_ref, vp_ref, pb, trip, n_pre, lambda m: n_pre - 1 - m, bufs[-3:], state, False)
    else:
        for t in range(n_pre):
            rows = pl.ds((n_pre - 1 - t) * pb, pb)
            kb = kp_ref[rows, :].astype(BF16)
            vb = vp_ref[rows, :].astype(BF16)
            state = list(state)
            state[0:2] = block(q_heads[0], kb, vb, trip, state[0], state[1], None)
            state[2:4] = block(q_heads[1], kb, vb, trip, state[2], state[3], None)

    o = jnp.where(first, state[1], state[3])
    sq = o * o
    s_a = jnp.sum(jnp.where(first, sq, 0.0), -1, keepdims=True)
    s_b = jnp.sum(sq, -1, keepdims=True) - s_a
    inv = jnp.where(first, lax.rsqrt(s_a / HEAD + RMS_EPS), lax.rsqrt(s_b / HEAD + RMS_EPS))
    out_ref[...] = (o * inv * g_ref[...]).astype(out_ref.dtype)


def _attention(q, k, v, kp, vp, gain, tq, pb):
    bsz, length, c = q.shape
    b0, plen, _ = kp.shape
    assert length % tq == 0 and plen % pb == 0 and b0 in (1, bsz)
    sb = (lambda b: b) if b0 == bsz else (lambda b: 0)
    n_pre = plen // pb
    n_main = length // tq
    pipe_main = n_main > 1
    pipe_pre = n_pre >= 3
    tri1 = lambda n: (jnp.arange(n)[:, None] >= jnp.arange(n)[None, :]).astype(BF16)
    stage_bufs = lambda bs: [pltpu.VMEM((2, 2, tq, bs), F32), pltpu.VMEM((2, tq, bs), F32),
                             pltpu.VMEM((2, tq, 1), F32)]
    full = pl.BlockSpec((None, length, LANES), lambda b, h, i: (b, 0, h))
    pre = pl.BlockSpec((None, plen, LANES), lambda b, h, i: (sb(b), 0, h))
    kernel = functools.partial(_attn_kernel, tq=tq, pb=pb, n_pre=n_pre, pipe_main=pipe_main, pipe_pre=pipe_pre)
    return pl.pallas_call(
        kernel,
        grid=(bsz, c // LANES, n_main),
        in_specs=[
            full, full, full, pre, pre,
            pl.BlockSpec((tq, tq), lambda b, h, i: (0, 0)),
            pl.BlockSpec((pb, pb), lambda b, h, i: (0, 0)),
            pl.BlockSpec((1, LANES), lambda b, h, i: (0, h)),
        ],
        out_specs=pl.BlockSpec((None, tq, LANES), lambda b, h, i: (b, i, h)),
        out_shape=jax.ShapeDtypeStruct((bsz, length, c), BF16),
        scratch_shapes=((stage_bufs(tq) if pipe_main else []) + (stage_bufs(pb) if pipe_pre else [])
                        + [pltpu.VMEM((1, LANES), F32)]),
        compiler_params=_cparams(("parallel", "parallel", "arbitrary")),
        name="attn",
    )(q, k, v, kp, vp, tri1(tq), tri1(pb), gain)


def _route_columns(logits_t, bias):
    ne, n = logits_t.shape
    gsz = ne // N_GROUPS
    neg = -jnp.inf
    scores = _sigmoid(logits_t)
    sel = scores + bias
    ig = lax.broadcasted_iota(jnp.int32, (gsz, n), 0)
    gs = []
    for g in range(N_GROUPS):
        xg = sel[g * gsz:(g + 1) * gsz, :]
        m1 = jnp.max(xg, axis=0, keepdims=True)
        first = jnp.min(jnp.where(xg == m1, ig, gsz), axis=0, keepdims=True)
        m2 = jnp.max(jnp.where(ig == first, neg, xg), axis=0, keepdims=True)
        gs.append(m1 + m2)
    kept = []
    for g in range(N_GROUPS):
        rank = jnp.zeros((1, n), jnp.int32)
        for g2 in range(N_GROUPS):
            if g2 == g:
                continue
            ahead = (gs[g2] >= gs[g]) if g2 < g else (gs[g2] > gs[g])
            rank = rank + ahead.astype(jnp.int32)
        kept.append(jnp.where(rank < TOPK_GROUPS, sel[g * gsz:(g + 1) * gsz, :], neg))
    cand = jnp.concatenate(kept, axis=0)
    ie = lax.broadcasted_iota(jnp.int32, (ne, n), 0)
    ids, gates = [], []
    chosen = jnp.zeros((ne, n), F32)
    for _ in range(TOP_K):
        m = jnp.max(cand, axis=0, keepdims=True)
        idx = jnp.min(jnp.where(cand == m, ie, ne), axis=0, keepdims=True)
        hit = ie == idx
        gates.append(jnp.sum(jnp.where(hit, scores, 0.0), axis=0, keepdims=True))
        ids.append(idx)
        cand = jnp.where(hit, neg, cand)
        chosen = chosen + jnp.where(hit, 1.0, 0.0)
    gate = jnp.concatenate(gates, axis=0)
    gate = gate / jnp.sum(gate, axis=0, keepdims=True) * ROUTED_SCALE
    return jnp.concatenate(ids, axis=0), gate, jnp.sum(chosen, axis=1, keepdims=True)


def _out_proj_kernel(x_ref, rw_ref, sb_ref, wa_ref, wb_ref, g0_ref, b0_ref, g1_ref, b1_ref,
                     wrh_ref, wrl_ref, rb_ref, h_ref, hb_ref, idx_ref, gate_ref, cnt_ref):
    xn = _layer_norm(x_ref[...], g0_ref[...], b0_ref[...])
    mix = _dot(rw_ref[...], wa_ref[...]) + _dot(sb_ref[...], wb_ref[...])
    h = _layer_norm(DN_ALPHA * xn + mix, g1_ref[...], b1_ref[...])
    h_ref[...] = h
    hi, lo = _split2(h)
    hb_ref[...] = hi
    wrh = wrh_ref[...]
    logits_t = _dot_nt(wrh, hi) + _dot_nt(wrh, lo) + _dot_nt(wrl_ref[...], hi)
    idx, gate, cnt = _route_columns(logits_t, rb_ref[...])
    idx_ref[...] = idx
    gate_ref[...] = gate
    cnt_ref[...] = cnt


def _out_proj(x, rw, sbo, wa, wb, g0, b0, g1, b1, wrh, wrl, rbias, tm):
    t, d = x.shape
    ne = wrh.shape[0]
    assert t % tm == 0
    row = lambda w: pl.BlockSpec((tm, w), lambda i: (i, 0))
    col = pl.BlockSpec((TOP_K, tm), lambda i: (0, i))
    const = lambda a: pl.BlockSpec(a.shape, lambda i: (0, 0))
    return pl.pallas_call(
        _out_proj_kernel,
        grid=(t // tm,),
        in_specs=[row(d), row(C_RW), row(C_SB), const(wa), const(wb), const(g0), const(b0), const(g1), const(b1),
                  const(wrh), const(wrl), const(rbias)],
        out_specs=[row(d), row(d), col, col, pl.BlockSpec((None, ne, 1), lambda i: (i, 0, 0))],
        out_shape=[jax.ShapeDtypeStruct((t, d), F32), jax.ShapeDtypeStruct((t, d), BF16),
                   jax.ShapeDtypeStruct((TOP_K, t), jnp.int32), jax.ShapeDtypeStruct((TOP_K, t), F32),
                   jax.ShapeDtypeStruct((t // tm, ne, 1), F32)],
        compiler_params=_cparams(("parallel",)),
        name="out_proj",
    )(x, rw, sbo, wa, wb, g0, b0, g1, b1, wrh, wrl, rbias)


def _expert_kernel(be_ref, nx_ref, nv_ref, tok_ref, nxt_ref, rw_ref, h_hbm, wg_hbm, wu_hbm, wd_hbm, out_ref,
                   xbuf, sems, xb_ref, wg_buf, wu_buf, wd_buf, wsems, wgb_ref, wub_ref, wdb_ref, state_ref):
    i = pl.program_id(0)
    nv = nv_ref[0]
    e = be_ref[i]
    slot = lax.rem(i, 2)
    blk = xbuf.shape[1]

    def row_copy(src_row, r, s):
        return pltpu.make_async_copy(h_hbm.at[pl.ds(src_row, 1), :], xbuf.at[s, pl.ds(r, 1), :], sems.at[s])

    def weight_copies(expert, p):
        return [pltpu.make_async_copy(src.at[expert], dst.at[p], wsems.at[p])
                for src, dst in ((wg_hbm, wg_buf), (wu_hbm, wu_buf), (wd_hbm, wd_buf))]

    @pl.when(i == 0)
    def _():
        state_ref[0] = -1
        state_ref[1] = 0
        for cp in weight_copies(e, 0):
            cp.start(priority=WEIGHT_DMA_PRIORITY)
        for r in range(blk):
            row_copy(tok_ref[0, r], r, 0).start(priority=GATHER_DMA_PRIORITY)

    @pl.when(i < nv)
    def _():
        @pl.when(e != state_ref[0])
        def _():
            p = state_ref[1]
            for cp in weight_copies(e, p):
                cp.wait()

            @pl.when(nx_ref[i] != e)
            def _():
                for cp in weight_copies(nx_ref[i], 1 - p):
                    cp.start(priority=WEIGHT_DMA_PRIORITY)

            wgb_ref[...] = wg_buf[p].astype(BF16)
            wub_ref[...] = wu_buf[p].astype(BF16)
            wdb_ref[...] = wd_buf[p].astype(BF16)
            state_ref[0] = e
            state_ref[1] = 1 - p

        for r in range(blk):
            row_copy(0, r, slot).wait()
        xb_ref[...] = xbuf[slot].astype(BF16)
        for r in range(blk):
            row_copy(nxt_ref[0, r], r, 1 - slot).start(priority=r % 2)
        x = xb_ref[...]
        hg = _dot(x, wgb_ref[...])
        hu = _dot(x, wub_ref[...])
        hid = (hg * _sigmoid(hg) * hu).astype(BF16)
        out_ref[...] = (_dot(hid, wdb_ref[...]) * rw_ref[...]).astype(out_ref.dtype)

        @pl.when(i == nv - 1)
        def _():
            for r in range(blk):
                row_copy(0, r, 1 - slot).wait()

    @pl.when(i >= nv)
    def _():
        out_ref[...] = jnp.zeros_like(out_ref)


def _experts(h, row_tok, row_w, blk_exp, n_valid, w_gate, w_up, w_down):
    _, d = h.shape
    _, _, de = w_gate.shape
    blk = EXPERT_BLOCK
    nb = row_tok.shape[0] // blk
    tok3 = row_tok.reshape(nb, 1, blk)
    after = jnp.sum((blk_exp[None, :] <= blk_exp[:, None]).astype(jnp.int32), axis=1)
    nxt_exp = jnp.where(after < n_valid[0], blk_exp[jnp.minimum(after, nb - 1)], blk_exp)
    live = lambda i, nv: jnp.minimum(i, nv[0] - 1)
    idx_spec = lambda step: pl.BlockSpec((None, 1, blk), lambda i, be, nx, nv: (live(i + step, nv), 0, 0),
                                         memory_space=pltpu.SMEM)
    hbm = pl.BlockSpec(memory_space=pl.ANY)
    grid_spec = pltpu.PrefetchScalarGridSpec(
        num_scalar_prefetch=3,
        grid=(nb,),
        in_specs=[
            idx_spec(0), idx_spec(1),
            pl.BlockSpec((blk, 1), lambda i, be, nx, nv: (live(i, nv), 0)),
            hbm, hbm, hbm, hbm,
        ],
        out_specs=pl.BlockSpec((blk, d), lambda i, be, nx, nv: (i, 0)),
        scratch_shapes=[pltpu.VMEM((2, blk, d), F32), pltpu.SemaphoreType.DMA((2,)), pltpu.VMEM((blk, d), BF16),
                        pltpu.VMEM((2, d, de), F32), pltpu.VMEM((2, d, de), F32), pltpu.VMEM((2, de, d), F32),
                        pltpu.SemaphoreType.DMA((2,)),
                        pltpu.VMEM((d, de), BF16), pltpu.VMEM((d, de), BF16), pltpu.VMEM((de, d), BF16),
                        pltpu.SMEM((2,), jnp.int32)],
    )
    return pl.pallas_call(
        _expert_kernel,
        grid_spec=grid_spec,
        out_shape=jax.ShapeDtypeStruct((nb * blk, d), BF16),
        compiler_params=_cparams(("arbitrary",)),
        name="experts",
    )(blk_exp, nxt_exp, n_valid, tok3, tok3, row_w, h, w_gate, w_up, w_down)


def _final_kernel(h_ref, hb_ref, routed_ref, wg_ref, wu_ref, wd_ref, g_ref, b_ref, out_ref):
    x = hb_ref[...]
    hg = _dot(x, wg_ref[...])
    hu = _dot(x, wu_ref[...])
    shared = _dot((hg * _sigmoid(hg) * hu).astype(BF16), wd_ref[...])
    y = DN_ALPHA * h_ref[...] + routed_ref[...] + shared
    out_ref[...] = _layer_norm(y, g_ref[...], b_ref[...])


def _final(h, hb, routed, wg, wu, wd, g, b, tm):
    t, d = h.shape
    assert t % tm == 0
    row = pl.BlockSpec((tm, d), lambda i: (i, 0))
    const = lambda a: pl.BlockSpec(a.shape, lambda i: (0, 0))
    return pl.pallas_call(
        _final_kernel,
        grid=(t // tm,),
        in_specs=[row, row, row, const(wg), const(wu), const(wd), const(g), const(b)],
        out_specs=row,
        out_shape=jax.ShapeDtypeStruct((t, d), F32),
        compiler_params=_cparams(("parallel",)),
        name="final",
    )(h, hb, routed, wg, wu, wd, g, b)


def _dispatch_plan(idx_t, gate_t, counts):
    ne = counts.shape[0]
    t = idx_t.shape[1]
    blk = EXPERT_BLOCK
    n_assign = t * TOP_K
    nb = n_assign // blk + ne
    n_rows = nb * blk
    n_fill = n_rows - n_assign
    e_flat = idx_t.reshape(n_assign)
    iota = jnp.arange(n_assign, dtype=jnp.int32)
    padded = (counts + blk - 1) // blk * blk
    n_valid = (jnp.sum(padded) // blk).astype(jnp.int32).reshape(1)
    fill_end = jnp.cumsum(padded - counts)
    fill_e = jnp.sum((fill_end[None, :] <= jnp.arange(n_fill, dtype=jnp.int32)[:, None]).astype(jnp.int32), axis=1)
    zeros = jnp.zeros((n_fill,), jnp.int32)
    keys = jnp.concatenate([2 * e_flat, 2 * fill_e + 1])
    toks = jnp.concatenate([iota % t, zeros])
    wts = jnp.concatenate([gate_t.reshape(n_assign), zeros.astype(F32)])
    aid = jnp.concatenate([iota, zeros + n_assign])
    keys, row_tok, row_w, row_aid = lax.sort((keys, toks, wts, aid), num_keys=1)
    blk_exp = jnp.minimum(keys[::blk] // 2, ne - 1)
    _, pos = lax.sort((row_aid, jnp.arange(n_rows, dtype=jnp.int32)), num_keys=1)
    return row_tok, row_w.reshape(n_rows, 1), blk_exp, n_valid, pos[:n_assign].reshape(TOP_K, t)


def _pick_tile(n, cap):
    t = cap
    while n % t:
        t //= 2
    return t


def kernel(x_prompt, x_sample, cache_sb_k, cache_sb_v, state_rwkv_wkv, state_rwkv_shift, meta_tokens, ln0_g, ln0_b, w_in, rw_mu, rw_w0, rw_w2, rw_a0, rw_a2, rw_g2, rw_k_k, rw_k_a, rw_r_k, rw_gn_g, rw_gn_b, sb_norm_g, w_out, ln1_g, ln1_b, w_router, router_bias, w_exp_gate, w_exp_up, w_exp_down, w_sh_gate, w_sh_up, w_sh_down, ln2_g, ln2_b):
    bsz, seq, d = x_prompt.shape
    dbs, dseq, _ = x_sample.shape
    past = cache_sb_k.shape[2]
    n_h = C_RW // HEAD
    ne = w_router.shape[-1]
    rw_cols = 3 * C_RW + D_DECAY + D_AAA + D_GATE
    o_wd, o_ad, o_gd = 3 * C_RW, 3 * C_RW + D_DECAY, 3 * C_RW + D_DECAY + D_AAA

    def pad_low(a):
        z = lambda n: jnp.zeros(a.shape[:-1] + (n,), a.dtype)
        return jnp.concatenate([a[..., :o_wd], a[..., o_wd:o_ad], z(LANES - D_DECAY), a[..., o_ad:o_gd],
                                z(LANES - D_AAA), a[..., o_gd:rw_cols], z(2 * LANES - D_GATE)], -1)

    def unpad_low(rkv_row, low_row):
        return jnp.concatenate([rkv_row, low_row[..., 0:D_DECAY], low_row[..., LANES:LANES + D_AAA],
                                low_row[..., 2 * LANES:2 * LANES + D_GATE]], -1)

    wi = w_in[0]
    w_rw = pad_low(wi[:, :rw_cols])
    w6 = jnp.concatenate([w_rw[:, :3 * C_RW], wi[:, rw_cols:]], 1).astype(BF16)
    wlow = w_rw[:, 3 * C_RW:].astype(BF16)
    pad_rows = lambda a, n: jnp.concatenate([a, jnp.zeros((n - a.shape[0], a.shape[1]), a.dtype)], 0)
    row = lambda a: a.reshape(1, -1)
    prm = {
        "mu": pad_low(rw_mu[0]).reshape(1, -1),
        "w0": row(rw_w0[0]), "a0": row(rw_a0[0]), "k_k": row(rw_k_k[0]), "k_a": row(rw_k_a[0]),
        "r_k": row(rw_r_k[0]), "gn_g": row(rw_gn_g[0]), "gn_b": row(rw_gn_b[0]),
        "w2": pad_rows(rw_w2[0], LANES), "a2": pad_rows(rw_a2[0], LANES), "g2": pad_rows(rw_g2[0], 2 * LANES),
    }
    g0, b0 = row(ln0_g), row(ln0_b)
    g1, b1 = row(ln1_g[0]), row(ln1_b[0])
    g2, b2 = row(ln2_g[0]), row(ln2_b[0])
    sb_gain = row(sb_norm_g[0])
    wo = w_out[0].astype(BF16)
    wo_a, wo_b = wo[:C_RW], wo[C_RW:]
    wr_hi, wr_lo = _split2(w_router[0].T)
    rbias = router_bias[0].astype(F32).reshape(ne, 1)
    wsg, wsu, wsd = w_sh_gate[0].astype(BF16), w_sh_up[0].astype(BF16), w_sh_down[0].astype(BF16)

    xp = x_prompt.reshape(bsz * seq, d)
    xs = x_sample.reshape(dbs * dseq, d)
    xm = meta_tokens.astype(x_prompt.dtype)
    rkv_m, _, k_m, v_m, low_m = _in_proj(xm, g0, b0, w6, wlow, N_META)
    rkv_p, q_p, k_p, v_p, low_p = _in_proj(xp, g0, b0, w6, wlow, _pick_tile(bsz * seq, 512))
    rkv_s, q_s, k_s, v_s, low_s = _in_proj(xs, g0, b0, w6, wlow, _pick_tile(dbs * dseq, 512))

    zero_shift = jnp.zeros((1, 1, 3 * C_RW + LOW_PAD), F32)
    zero_wkv = jnp.zeros((1, n_h, HEAD, HEAD), F32)
    _, wkv_m = _rwkv(rkv_m[None], low_m[None], zero_shift, zero_wkv, prm, N_META)
    shift_m = jnp.concatenate([rkv_m[-1:], low_m[-1:]], -1)[None]
    rw_p, wkv_p = _rwkv(rkv_p.reshape(bsz, seq, -1), low_p.reshape(bsz, seq, -1), shift_m, wkv_m, prm,
                        _pick_tile(seq, 64))
    shift_s0 = pad_low(state_rwkv_shift[0])
    rw_s, wkv_s = _rwkv(rkv_s.reshape(dbs, dseq, -1), low_s.reshape(dbs, dseq, -1), shift_s0,
                        state_rwkv_wkv[0], prm, dseq)

    c3 = lambda a, b_, l_: a.reshape(b_, l_, C_SB)
    sb_p = _attention(c3(q_p, bsz, seq), c3(k_p, bsz, seq), c3(v_p, bsz, seq), k_m[None], v_m[None], sb_gain,
                      _pick_tile(seq, 256), N_META)
    sb_s = _attention(c3(q_s, dbs, dseq), c3(k_s, dbs, dseq), c3(v_s, dbs, dseq),
                      cache_sb_k[0].reshape(dbs, past, C_SB), cache_sb_v[0].reshape(dbs, past, C_SB), sb_gain,
                      dseq, _pick_tile(past, 256))

    x_all = jnp.concatenate([xp, xs], 0)
    rw_all = jnp.concatenate([rw_p.reshape(bsz * seq, C_RW), rw_s.reshape(dbs * dseq, C_RW)], 0)
    sb_all = jnp.concatenate([sb_p.reshape(bsz * seq, C_SB), sb_s.reshape(dbs * dseq, C_SB)], 0)
    t_all = x_all.shape[0]
    tm = _pick_tile(t_all, 256)
    h, hb, idx_t, gate_t, cnt = _out_proj(x_all, rw_all, sb_all, wo_a, wo_b, g0, b0, g1, b1, wr_hi, wr_lo, rbias, tm)

    counts = jnp.sum(cnt[:, :, 0], axis=0).astype(jnp.int32)
    row_tok, row_w, blk_exp, n_valid, pos = _dispatch_plan(idx_t, gate_t, counts)
    expert_out = _experts(h, row_tok, row_w, blk_exp, n_valid, w_exp_gate[0], w_exp_up[0], w_exp_down[0])
    routed = jnp.sum(jnp.take(expert_out, pos, axis=0).astype(F32), axis=0)
    y = _final(h, hb, routed, wsg, wsu, wsd, g2, b2, tm)

    y_prompt = y[:bsz * seq].reshape(bsz, seq, d)
    y_sample = y[bsz * seq:].reshape(dbs, dseq, d)
    heads = lambda a, b_, l_: a.reshape(b_, l_, C_SB // HEAD, HEAD)

    def with_meta(m, p):
        m4 = jnp.broadcast_to(heads(m, 1, N_META), (bsz, N_META, C_SB // HEAD, HEAD))
        return jnp.concatenate([m4, heads(p, bsz, seq)], 1)[None]

    k_prompt = with_meta(k_m, k_p)
    v_prompt = with_meta(v_m, v_p)
    last = lambda a, b_, l_: a.reshape(b_, l_, -1)[:, -1:]
    shift_prompt = unpad_low(last(rkv_p, bsz, seq), last(low_p, bsz, seq))[None]
    shift_sample = unpad_low(last(rkv_s, dbs, dseq), last(low_s, dbs, dseq))[None]
    return (y_prompt, y_sample, k_prompt, v_prompt, wkv_p[None], shift_prompt,
            heads(k_s, dbs, dseq)[None], heads(v_s, dbs, dseq)[None], wkv_s[None], shift_sample)
```
